```python
import jax, jax.numpy as jnp
from jax import lax
import numpy as np

D_MODEL = 1024
BATCH = 2
SEQ = 8192
DEPTH = 1

PLE_DIM = 256
MIX_WIDTH = D_MODEL
HEAD_DIM = 64
ATT_WIDTH = MIX_WIDTH // 2
N_ATT_HEADS = ATT_WIDTH // HEAD_DIM
SG_WIDTH = MIX_WIDTH - ATT_WIDTH
N_SG_GROUPS = 8
SG_GROUP_DIM = SG_WIDTH // N_SG_GROUPS
CHUNK = 128
Q_BLOCK = 128
D_FF = 4 * D_MODEL
IN_PROJ_WIDTH = 3 * ATT_WIDTH + N_ATT_HEADS + 2 * SG_WIDTH
EPS = 1e-6

kernel_name = "hybrid_fox_gmlp_sandwich_ple"


def rms_norm(x, g):
    xf = x.astype(jnp.float32)
    y = xf * lax.rsqrt(jnp.mean(xf * xf, axis=-1, keepdims=True) + EPS)
    return (y * g.astype(jnp.float32)).astype(x.dtype)


def layer_norm(x, g, b):
    xf = x.astype(jnp.float32)
    mu = jnp.mean(xf, axis=-1, keepdims=True)
    xc = xf - mu
    y = xc * lax.rsqrt(jnp.mean(xc * xc, axis=-1, keepdims=True) + EPS)
    return (y * g.astype(jnp.float32) + b.astype(jnp.float32)).astype(x.dtype)


def forgetting_attention(q, k, v, f_logit):
    B, S, H, Dh = q.shape
    nb = S // Q_BLOCK
    log_f = jax.nn.log_sigmoid(f_logit.astype(jnp.float32))
    c = jnp.transpose(jnp.cumsum(log_f, axis=1), (0, 2, 1))
    qh = jnp.transpose(q, (0, 2, 1, 3)).astype(jnp.float32) * (Dh ** -0.5)
    kh = jnp.transpose(k, (0, 2, 1, 3)).astype(jnp.float32)
    vh = jnp.transpose(v, (0, 2, 1, 3)).astype(jnp.float32)
    q_blocks = jnp.transpose(qh.reshape(B, H, nb, Q_BLOCK, Dh), (2, 0, 1, 3, 4))
    cq_blocks = jnp.transpose(c.reshape(B, H, nb, Q_BLOCK), (2, 0, 1, 3))
    k_pos = jnp.arange(S)

    def one_block(args):
        qb, cqb, bi = args
        q_pos = bi * Q_BLOCK + jnp.arange(Q_BLOCK)
        logits = (jnp.einsum('bhqd,bhkd->bhqk', qb, kh)
                  + cqb[..., :, None] - c[..., None, :])
        causal = k_pos[None, :] <= q_pos[:, None]
        logits = jnp.where(causal, logits, -jnp.inf)
        probs = jax.nn.softmax(logits, axis=-1)
        return jnp.einsum('bhqk,bhkd->bhqd', probs, vh)

    out = lax.map(one_block, (q_blocks, cq_blocks, jnp.arange(nb)))
    out = jnp.transpose(out, (1, 0, 3, 2, 4)).reshape(B, S, H * Dh)
    return out.astype(q.dtype)


def chunked_spatial_gating(u, v, ln_g, ln_b, w_s, b_s):
    B, S, _ = u.shape
    nc = S // CHUNK
    u = jax.nn.gelu(u)
    v = layer_norm(jax.nn.gelu(v), ln_g, ln_b)
    vc = v.reshape(B, nc, CHUNK, N_SG_GROUPS, SG_GROUP_DIM)
    mask = jnp.tril(jnp.ones((CHUNK, CHUNK), dtype=w_s.dtype))
    w = w_s * mask[None]
    mixed = (jnp.einsum('gts,bcsgd->bctgd', w, vc)
             + jnp.transpose(b_s)[None, None, :, :, None])
    return u * mixed.reshape(B, S, SG_WIDTH)


def setup_inputs(seed: int = 0) -> dict:
    key = jax.random.key(seed)
    ks = jax.random.split(key, 24)
    n = jax.random.normal
    f32 = jnp.float32

    def gain(k, shape):
        return 1.0 + 0.05 * n(k, shape, f32)

    x = n(ks[0], (BATCH, SEQ, D_MODEL), f32)
    p = n(ks[1], (DEPTH, BATCH, SEQ, PLE_DIM), f32)
    w_in = n(ks[2], (DEPTH, D_MODEL, IN_PROJ_WIDTH), f32) * D_MODEL ** -0.5
    f_bias = (jnp.linspace(1.0, 5.0, N_ATT_HEADS, dtype=f32)[None, :]
              + 0.1 * n(ks[3], (DEPTH, N_ATT_HEADS), f32))
    sg_ln_g = gain(ks[4], (DEPTH, SG_WIDTH))
    sg_ln_b = 0.02 * n(ks[5], (DEPTH, SG_WIDTH), f32)
    sg_w = n(ks[6], (DEPTH, N_SG_GROUPS, CHUNK, CHUNK), f32) * CHUNK ** -0.5
    sg_b = 1.0 + 0.02 * n(ks[7], (DEPTH, N_SG_GROUPS, CHUNK), f32)
    att_out_g = gain(ks[8], (DEPTH, ATT_WIDTH))
    sg_out_g = gain(ks[9], (DEPTH, SG_WIDTH))
    w_out = n(ks[10], (DEPTH, MIX_WIDTH, D_MODEL), f32) * MIX_WIDTH ** -0.5
    pre_mix_g = gain(ks[11], (DEPTH, D_MODEL))
    post_mix_g = gain(ks[12], (DEPTH, D_MODEL))
    pre_ffn_g = gain(ks[13], (DEPTH, D_MODEL))
    post_ffn_g = gain(ks[14], (DEPTH, D_MODEL))
    w_ff1 = n(ks[15], (DEPTH, D_MODEL, D_FF), f32) * D_MODEL ** -0.5
    w_ff2 = n(ks[16], (DEPTH, D_FF, D_MODEL), f32) * D_FF ** -0.5
    ple_w = n(ks[17], (DEPTH, PLE_DIM, D_MODEL), f32) * PLE_DIM ** -0.5
    ple_gate_w = n(ks[18], (DEPTH, D_MODEL, D_MODEL), f32) * D_MODEL ** -0.5
    ple_gate_b = 0.02 * n(ks[19], (DEPTH, D_MODEL), f32)
    return {"x": x, "p": p, "w_in": w_in, "f_bias": f_bias,
            "sg_ln_g": sg_ln_g, "sg_ln_b": sg_ln_b, "sg_w": sg_w, "sg_b": sg_b,
            "att_out_g": att_out_g, "sg_out_g": sg_out_g, "w_out": w_out,
            "pre_mix_g": pre_mix_g, "post_mix_g": post_mix_g,
            "pre_ffn_g": pre_ffn_g, "post_ffn_g": post_ffn_g,
            "w_ff1": w_ff1, "w_ff2": w_ff2,
            "ple_w": ple_w, "ple_gate_w": ple_gate_w, "ple_gate_b": ple_gate_b}


def reference(x, p, w_in, f_bias, sg_ln_g, sg_ln_b, sg_w, sg_b, att_out_g, sg_out_g,
              w_out, pre_mix_g, post_mix_g, pre_ffn_g, post_ffn_g, w_ff1, w_ff2,
              ple_w, ple_gate_w, ple_gate_b):
    B, S, _ = x.shape
    splits = [ATT_WIDTH, 2 * ATT_WIDTH, 3 * ATT_WIDTH,
              3 * ATT_WIDTH + N_ATT_HEADS, 3 * ATT_WIDTH + N_ATT_HEADS + SG_WIDTH]
    h = x
    for i in range(DEPTH):
        a = rms_norm(h, pre_mix_g[i])
        z = a @ w_in[i]
        q, k, v, f, u_sg, v_sg = jnp.split(z, splits, axis=-1)
        q = q.reshape(B, S, N_ATT_HEADS, HEAD_DIM)
        k = k.reshape(B, S, N_ATT_HEADS, HEAD_DIM)
        v = v.reshape(B, S, N_ATT_HEADS, HEAD_DIM)
        y_att = forgetting_attention(q, k, v, f + f_bias[i])
        y_sg = chunked_spatial_gating(u_sg, v_sg, sg_ln_g[i], sg_ln_b[i],
                                      sg_w[i], sg_b[i])
        y = jnp.concatenate([rms_norm(y_att, att_out_g[i]),
                             rms_norm(y_sg, sg_out_g[i])], axis=-1)
        h = h + rms_norm(y @ w_out[i], post_mix_g[i])
        c = rms_norm(h, pre_ffn_g[i])
        ff = jnp.square(jax.nn.relu(c @ w_ff1[i])) @ w_ff2[i]
        h = h + rms_norm(ff, post_ffn_g[i])
        gate = jax.nn.sigmoid(h @ ple_gate_w[i] + ple_gate_b[i])
        h = h + gate * (p[i] @ ple_w[i])
    return h
```

```python
import functools

import jax
import jax.numpy as jnp
from jax import lax
from jax.experimental import pallas as pl
from jax.experimental.pallas import tpu as pltpu

D_MODEL = 1024
PLE_DIM = 256
HEAD_DIM = 64
ATT_WIDTH = 512
N_HEADS = 8
SG_WIDTH = 512
N_SG_GROUPS = 8
CHUNK = 128
D_FF = 4096
EPS = 1e-6

LANES = 128
HEAD_PAIRS = N_HEADS // 2
NEG_BIG = -1e30

T_IN = 512
T_ATT = 512
T_MLP = 512
FF_CHUNK = 1024

VMEM_LIMIT = 56 * 1024 * 1024

F32 = jnp.float32
BF16 = jnp.bfloat16


def _rms(x, g):
    return x * lax.rsqrt(jnp.mean(x * x, axis=-1, keepdims=True) + EPS) * g


def _gelu_tanh(x):
    return x * (0.5 * (1.0 + jnp.tanh(0.7978845608028654 * (x + 0.044715 * (x * x * x)))))


def _split3(x):
    hi = x.astype(BF16).astype(F32)
    r = x - hi
    mid = r.astype(BF16).astype(F32)
    lo = (r - mid).astype(BF16).astype(F32)
    return hi, mid, lo


def _inproj_kernel(x_ref, g_ref, wqkv_ref, wf_ref, fb_ref, wuv_ref, lng_ref, lnb_ref,
                   sgw_ref, sgb_ref, sgog_ref,
                   qp_ref, kp_ref, vp_ref, ysg_ref,
                   carry_ref, ysg_scr):
    t = pl.program_id(1)

    @pl.when(t == 0)
    def _():
        carry_ref[...] = jnp.zeros_like(carry_ref)

    x = x_ref[0]
    ab = _rms(x, g_ref[...]).astype(BF16)

    f = jnp.dot(ab, wf_ref[...], preferred_element_type=F32) + fb_ref[...]
    logf = jnp.minimum(f, 0.0) - jnp.log1p(jnp.exp(-jnp.abs(f)))
    lane = lax.broadcasted_iota(jnp.int32, (T_IN, LANES), 1)
    hi, mid, lo = _split3(logf)
    pieces = jnp.where(lane < 8, hi, jnp.where(lane < 16, mid, lo)).astype(BF16)
    row_i = lax.broadcasted_iota(jnp.int32, (T_IN, T_IN), 0)
    col_i = lax.broadcasted_iota(jnp.int32, (T_IN, T_IN), 1)
    tril = jnp.where(col_i <= row_i, 1.0, 0.0).astype(BF16)
    c3 = jnp.dot(tril, pieces, preferred_element_type=F32) + carry_ref[...]
    carry_ref[...] = c3[T_IN - 1:T_IN, :]

    zqkv = jnp.dot(ab, wqkv_ref[...], preferred_element_type=F32)
    for h in range(N_HEADS):
        blk = h // 2
        even = (h % 2 == 0)
        base = HEAD_DIM if even else 0
        in_head = (lane < HEAD_DIM) if even else (lane >= HEAD_DIM)
        ch = c3[:, h:h + 1] + c3[:, 8 + h:9 + h] + c3[:, 16 + h:17 + h]
        chi, cmid, clo = _split3(jnp.broadcast_to(ch, (T_IN, LANES)))
        zq = zqkv[:, blk * LANES:(blk + 1) * LANES] * (HEAD_DIM ** -0.5)
        zk = zqkv[:, ATT_WIDTH + blk * LANES:ATT_WIDTH + (blk + 1) * LANES]
        zv = zqkv[:, 2 * ATT_WIDTH + blk * LANES:2 * ATT_WIDTH + (blk + 1) * LANES]
        q_extra = jnp.where(lane == base, chi,
                  jnp.where(lane == base + 1, cmid,
                  jnp.where(lane == base + 2, clo,
                  jnp.where((lane >= base + 3) & (lane < base + 6), 1.0, 0.0))))
        k_extra = jnp.where((lane >= base) & (lane < base + 3), 1.0,
                  jnp.where(lane == base + 3, -chi,
                  jnp.where(lane == base + 4, -cmid,
                  jnp.where(lane == base + 5, -clo, 0.0))))
        v_extra = jnp.where(lane == base, 1.0, 0.0)
        qp_ref[0, h] = jnp.where(in_head, zq, q_extra).astype(BF16)
        kp_ref[0, h] = jnp.where(in_head, zk, k_extra).astype(BF16)
        vp_ref[0, h] = jnp.where(in_head, zv, v_extra).astype(BF16)

    zuv = jnp.dot(ab, wuv_ref[...], preferred_element_type=F32)
    u = _gelu_tanh(zuv[:, :SG_WIDTH])
    v = _gelu_tanh(zuv[:, SG_WIDTH:])
    vc = v - jnp.mean(v, axis=-1, keepdims=True)
    vln = vc * lax.rsqrt(jnp.mean(vc * vc, axis=-1, keepdims=True) + EPS) * lng_ref[...] + lnb_ref[...]
    vb = vln.astype(BF16)
    n_chunks = T_IN // CHUNK
    ci = lax.broadcasted_iota(jnp.int32, (CHUNK, CHUNK), 0)
    cj = lax.broadcasted_iota(jnp.int32, (CHUNK, CHUNK), 1)
    lane_c = lax.broadcasted_iota(jnp.int32, (CHUNK, LANES), 1)
    for pb in range(SG_WIDTH // LANES):
        vcat = jnp.concatenate(
            [vb[c * CHUNK:(c + 1) * CHUNK, pb * LANES:(pb + 1) * LANES] for c in range(n_chunks)], axis=1)
        res = []
        for g in (2 * pb, 2 * pb + 1):
            w = jnp.where(cj <= ci, sgw_ref[g], 0.0).astype(BF16)
            res.append(jnp.dot(w, vcat, preferred_element_type=F32))
        for c in range(n_chunks):
            mixed = jnp.where(lane_c < HEAD_DIM,
                              res[0][:, c * LANES:(c + 1) * LANES],
                              res[1][:, c * LANES:(c + 1) * LANES])
            mixed = mixed + sgb_ref[:, pb * LANES:(pb + 1) * LANES]
            ysg_scr[c * CHUNK:(c + 1) * CHUNK, pb * LANES:(pb + 1) * LANES] = (
                u[c * CHUNK:(c + 1) * CHUNK, pb * LANES:(pb + 1) * LANES] * mixed)
    ysg_ref[0] = _rms(ysg_scr[...], sgog_ref[...]).astype(BF16)


def _attn_kernel(q_ref, k_ref, v_ref, o_ref, acc_ref, m_ref):
    qi = pl.program_id(2)
    acc_ref[...] = jnp.zeros_like(acc_ref)
    m_ref[...] = jnp.full_like(m_ref, NEG_BIG)

    def block(j, masked):
        start = pl.multiple_of(j * T_ATT, T_ATT)
        for hh in range(2):
            q = q_ref[0, hh]
            kb = k_ref[0, hh, pl.ds(start, T_ATT), :]
            vb = v_ref[0, hh, pl.ds(start, T_ATT), :]
            s = lax.dot_general(q, kb, (((1,), (1,)), ((), ())), preferred_element_type=F32)
            if masked:
                r = lax.broadcasted_iota(jnp.int32, (T_ATT, T_ATT), 0)
                c = lax.broadcasted_iota(jnp.int32, (T_ATT, T_ATT), 1)
                s = jnp.where(c <= r, s, NEG_BIG)
            m_old = m_ref[hh]
            m_new = jnp.maximum(m_old, jnp.max(s, axis=-1, keepdims=True))
            alpha = jnp.exp(m_old - m_new)
            p = jnp.exp(s - m_new).astype(BF16)
            acc_ref[hh] = alpha * acc_ref[hh] + jnp.dot(p, vb, preferred_element_type=F32)
            m_ref[hh] = m_new

    def body(j, carry):
        block(j, False)
        return carry

    lax.fori_loop(0, qi, body, 0)
    block(qi, True)

    a0 = acc_ref[0]
    a1 = acc_ref[1]
    lane = lax.broadcasted_iota(jnp.int32, (T_ATT, LANES), 1)
    o_ref[0] = jnp.where(lane < HEAD_DIM, a0 / a0[:, HEAD_DIM:HEAD_DIM + 1], a1 / a1[:, 0:1])


def _mlp_kernel(x_ref, ya_ref, ys_ref, p_ref, woa_ref, wos_ref, w1_ref, w2_ref, wg_ref, wpe_ref,
                gatt_ref, gpm_ref, gpf_ref, gqf_ref, bg_ref,
                o_ref, h1_ref, cn_ref, acc_ref):
    c = pl.program_id(1)

    @pl.when(c == 0)
    def _():
        yan = _rms(ya_ref[...], gatt_ref[...]).astype(BF16)
        y = (jnp.dot(yan, woa_ref[...], preferred_element_type=F32)
             + jnp.dot(ys_ref[...], wos_ref[...], preferred_element_type=F32))
        h1 = x_ref[...] + _rms(y, gpm_ref[...])
        h1_ref[...] = h1
        cn_ref[...] = _rms(h1, gpf_ref[...]).astype(BF16)
        acc_ref[...] = jnp.zeros_like(acc_ref)

    hid = jnp.dot(cn_ref[...], w1_ref[...], preferred_element_type=F32)
    hid = jnp.square(jnp.maximum(hid, 0.0)).astype(BF16)
    acc_ref[...] += jnp.dot(hid, w2_ref[...], preferred_element_type=F32)

    @pl.when(c == pl.num_programs(1) - 1)
    def _():
        h2 = h1_ref[...] + _rms(acc_ref[...], gqf_ref[...])
        gate = jax.nn.sigmoid(jnp.dot(h2.astype(BF16), wg_ref[...], preferred_element_type=F32) + bg_ref[...])
        pe = jnp.dot(p_ref[...].astype(BF16), wpe_ref[...], preferred_element_type=F32)
        o_ref[...] = h2 + gate * pe


def _layer(h, p, w_in, f_bias, sg_ln_g, sg_ln_b, sg_w, sg_b, att_out_g, sg_out_g, w_out,
           pre_mix_g, post_mix_g, pre_ffn_g, post_ffn_g, w_ff1, w_ff2, ple_w, ple_gate_w, ple_gate_b):
    B, S, D = h.shape
    R = B * S
    row = lambda a: a.reshape(1, -1).astype(F32)

    wqkv = w_in[:, :3 * ATT_WIDTH].astype(BF16)
    wf8 = w_in[:, 3 * ATT_WIDTH:3 * ATT_WIDTH + N_HEADS]
    wf = jnp.concatenate([wf8, wf8, wf8, jnp.zeros((D, LANES - 3 * N_HEADS), w_in.dtype)], axis=1).astype(BF16)
    fb = jnp.concatenate([f_bias, f_bias, f_bias, jnp.zeros((LANES - 3 * N_HEADS,), f_bias.dtype)]).reshape(1, LANES)
    wuv = w_in[:, 3 * ATT_WIDTH + N_HEADS:].astype(BF16)
    sgb = jnp.repeat(jnp.transpose(sg_b), HEAD_DIM, axis=1).astype(F32)

    const2 = lambda shape: pl.BlockSpec(shape, lambda b, t: (0,) * len(shape))
    qp, kp, vp, ysg = pl.pallas_call(
        _inproj_kernel,
        grid=(B, S // T_IN),
        in_specs=[
            pl.BlockSpec((1, T_IN, D), lambda b, t: (b, t, 0)),
            const2((1, D)),
            const2((D, 3 * ATT_WIDTH)),
            const2((D, LANES)),
            const2((1, LANES)),
            const2((D, 2 * SG_WIDTH)),
            const2((1, SG_WIDTH)),
            const2((1, SG_WIDTH)),
            const2((N_SG_GROUPS, CHUNK, CHUNK)),
            const2((CHUNK, SG_WIDTH)),
            const2((1, SG_WIDTH)),
        ],
        out_specs=[
            pl.BlockSpec((1, N_HEADS, T_IN, LANES), lambda b, t: (b, 0, t, 0)),
            pl.BlockSpec((1, N_HEADS, T_IN, LANES), lambda b, t: (b, 0, t, 0)),
            pl.BlockSpec((1, N_HEADS, T_IN, LANES), lambda b, t: (b, 0, t, 0)),
            pl.BlockSpec((1, T_IN, SG_WIDTH), lambda b, t: (b, t, 0)),
        ],
        out_shape=[
            jax.ShapeDtypeStruct((B, N_HEADS, S, LANES), BF16),
            jax.ShapeDtypeStruct((B, N_HEADS, S, LANES), BF16),
            jax.ShapeDtypeStruct((B, N_HEADS, S, LANES), BF16),
            jax.ShapeDtypeStruct((B, S, SG_WIDTH), BF16),
        ],
        scratch_shapes=[pltpu.VMEM((1, LANES), F32), pltpu.VMEM((T_IN, SG_WIDTH), F32)],
        compiler_params=pltpu.CompilerParams(
            dimension_semantics=("arbitrary", "arbitrary"), vmem_limit_bytes=VMEM_LIMIT),
        name="inproj",
    )(h, row(pre_mix_g), wqkv, wf, fb, wuv, row(sg_ln_g), row(sg_ln_b), sg_w.astype(F32), sgb, row(sg_out_g))

    yatt = pl.pallas_call(
        _attn_kernel,
        grid=(B, HEAD_PAIRS, S // T_ATT),
        in_specs=[
            pl.BlockSpec((1, 2, T_ATT, LANES), lambda b, hp, i: (b, hp, i, 0)),
            pl.BlockSpec((1, 2, S, LANES), lambda b, hp, i: (b, hp, 0, 0)),
            pl.BlockSpec((1, 2, S, LANES), lambda b, hp, i: (b, hp, 0, 0)),
        ],
        out_specs=pl.BlockSpec((1, T_ATT, LANES), lambda b, hp, i: (b, i, hp)),
        out_shape=jax.ShapeDtypeStruct((B, S, ATT_WIDTH), F32),
        scratch_shapes=[pltpu.VMEM((2, T_ATT, LANES), F32), pltpu.VMEM((2, T_ATT, 1), F32)],
        compiler_params=pltpu.CompilerParams(
            dimension_semantics=("arbitrary", "arbitrary", "arbitrary"), vmem_limit_bytes=VMEM_LIMIT),
        name="fox_attn",
    )(qp, kp, vp)

    n_ff = D_FF // FF_CHUNK
    rows = lambda w: pl.BlockSpec((T_MLP, w), lambda i, c: (i, 0))
    const = lambda shape: pl.BlockSpec(shape, lambda i, c: (0,) * len(shape))
    out = pl.pallas_call(
        _mlp_kernel,
        grid=(R // T_MLP, n_ff),
        in_specs=[
            rows(D), rows(ATT_WIDTH), rows(SG_WIDTH), rows(PLE_DIM),
            const((ATT_WIDTH, D)), const((SG_WIDTH, D)),
            pl.BlockSpec((D, FF_CHUNK), lambda i, c: (0, c)),
            pl.BlockSpec((FF_CHUNK, D), lambda i, c: (c, 0)),
            const((D, D)), const((PLE_DIM, D)),
            const((1, ATT_WIDTH)), const((1, D)), const((1, D)), const((1, D)), const((1, D)),
        ],
        out_specs=rows(D),
        out_shape=jax.ShapeDtypeStruct((R, D), F32),
        scratch_shapes=[pltpu.VMEM((T_MLP, D), F32), pltpu.VMEM((T_MLP, D), BF16), pltpu.VMEM((T_MLP, D), F32)],
        compiler_params=pltpu.CompilerParams(
            dimension_semantics=("arbitrary", "arbitrary"), vmem_limit_bytes=VMEM_LIMIT),
        name="mlp",
    )(h.reshape(R, D), yatt.reshape(R, ATT_WIDTH), ysg.reshape(R, SG_WIDTH), p.reshape(R, PLE_DIM),
      w_out[:ATT_WIDTH].astype(BF16), w_out[ATT_WIDTH:].astype(BF16),
      w_ff1.astype(BF16), w_ff2.astype(BF16), ple_gate_w.astype(BF16), ple_w.astype(BF16),
      row(att_out_g), row(post_mix_g), row(pre_ffn_g), row(post_ffn_g), row(ple_gate_b))
    return out.reshape(B, S, D)


def kernel(x, p, w_in, f_bias, sg_ln_g, sg_ln_b, sg_w, sg_b, att_out_g, sg_out_g, w_out,
           pre_mix_g, post_mix_g, pre_ffn_g, post_ffn_g, w_ff1, w_ff2, ple_w, ple_gate_w, ple_gate_b):
    h = x
    for i in range(p.shape[0]):
        h = _layer(h, p[i], w_in[i], f_bias[i], sg_ln_g[i], sg_ln_b[i], sg_w[i], sg_b[i],
                   att_out_g[i], sg_out_g[i], w_out[i], pre_mix_g[i], post_mix_g[i],
                   pre_ffn_g[i], post_ffn_g[i], w_ff1[i], w_ff2[i], ple_w[i], ple_gate_w[i],
                   ple_gate_b[i])
    return h
```

```python
import functools

import jax
import jax.numpy as jnp
from jax import lax
from jax.experimental import pallas as pl
from jax.experimental.pallas import tpu as pltpu

D_MODEL = 1024
PLE_DIM = 256
HEAD_DIM = 64
ATT_WIDTH = 512
N_HEADS = 8
SG_WIDTH = 512
N_SG_GROUPS = 8
CHUNK = 128
D_FF = 4096
EPS = 1e-6

LANES = 128
HEAD_PAIRS = N_HEADS // 2
NEG_BIG = -1e30
LOG2E = 1.4426950408889634

T_IN = 512
T_ATT = 512
T_MLP = 512
FF_CHUNK = 1024

VMEM_LIMIT = 56 * 1024 * 1024

F32 = jnp.float32
BF16 = jnp.bfloat16


def _rms(x, g):
    return x * lax.rsqrt(jnp.mean(x * x, axis=-1, keepdims=True) + EPS) * g


def _gelu_tanh(x):
    return x * (0.5 * (1.0 + jnp.tanh(0.7978845608028654 * (x + 0.044715 * (x * x * x)))))


def _split3(x):
    hi = x.astype(BF16).astype(F32)
    r = x - hi
    mid = r.astype(BF16).astype(F32)
    lo = (r - mid).astype(BF16).astype(F32)
    return hi, mid, lo


def _inproj_kernel(x_ref, g_ref, wqkv_ref, wf_ref, fb_ref, wuv_ref, lng_ref, lnb_ref,
                   sgw_ref, sgb_ref, sgog_ref,
                   qp_ref, kp_ref, vp_ref, ysg_ref,
                   carry_ref, ysg_scr):
    t = pl.program_id(1)

    @pl.when(t == 0)
    def _():
        carry_ref[...] = jnp.zeros_like(carry_ref)

    x = x_ref[0]
    ab = _rms(x, g_ref[...]).astype(BF16)

    f = jnp.dot(ab, wf_ref[...], preferred_element_type=F32) + fb_ref[...]
    logf = jnp.minimum(f, 0.0) - jnp.log1p(jnp.exp(-jnp.abs(f)))
    lane = lax.broadcasted_iota(jnp.int32, (T_IN, LANES), 1)
    hi, mid, lo = _split3(logf)
    pieces = jnp.where(lane < 8, hi, jnp.where(lane < 16, mid, lo)).astype(BF16)
    row_i = lax.broadcasted_iota(jnp.int32, (T_IN, T_IN), 0)
    col_i = lax.broadcasted_iota(jnp.int32, (T_IN, T_IN), 1)
    tril = jnp.where(col_i <= row_i, 1.0, 0.0).astype(BF16)
    c3 = jnp.dot(tril, pieces, preferred_element_type=F32) + carry_ref[...]
    carry_ref[...] = c3[T_IN - 1:T_IN, :]

    zqkv = jnp.dot(ab, wqkv_ref[...], preferred_element_type=F32)
    for h in range(N_HEADS):
        blk = h // 2
        even = (h % 2 == 0)
        base = HEAD_DIM if even else 0
        in_head = (lane < HEAD_DIM) if even else (lane >= HEAD_DIM)
        ch = c3[:, h:h + 1] + c3[:, 8 + h:9 + h] + c3[:, 16 + h:17 + h]
        chi, cmid, clo = _split3(jnp.broadcast_to(ch, (T_IN, LANES)) * LOG2E)
        zq = zqkv[:, blk * LANES:(blk + 1) * LANES] * (HEAD_DIM ** -0.5 * LOG2E)
        zk = zqkv[:, ATT_WIDTH + blk * LANES:ATT_WIDTH + (blk + 1) * LANES]
        zv = zqkv[:, 2 * ATT_WIDTH + blk * LANES:2 * ATT_WIDTH + (blk + 1) * LANES]
        q_extra = jnp.where(lane == base, chi,
                  jnp.where(lane == base + 1, cmid,
                  jnp.where(lane == base + 2, clo,
                  jnp.where((lane >= base + 3) & (lane < base + 6), 1.0, 0.0))))
        k_extra = jnp.where((lane >= base) & (lane < base + 3), 1.0,
                  jnp.where(lane == base + 3, -chi,
                  jnp.where(lane == base + 4, -cmid,
                  jnp.where(lane == base + 5, -clo, 0.0))))
        v_extra = jnp.where(lane == base, 1.0, 0.0)
        qp_ref[0, h] = jnp.where(in_head, zq, q_extra).astype(BF16)
        kp_ref[0, h] = jnp.where(in_head, zk, k_extra).astype(BF16)
        vp_ref[0, h] = jnp.where(in_head, zv, v_extra).astype(BF16)

    zuv = jnp.dot(ab, wuv_ref[...], preferred_element_type=F32)
    u = _gelu_tanh(zuv[:, :SG_WIDTH])
    v = _gelu_tanh(zuv[:, SG_WIDTH:])
    vc = v - jnp.mean(v, axis=-1, keepdims=True)
    vln = vc * lax.rsqrt(jnp.mean(vc * vc, axis=-1, keepdims=True) + EPS) * lng_ref[...] + lnb_ref[...]
    vb = vln.astype(BF16)
    n_chunks = T_IN // CHUNK
    ci = lax.broadcasted_iota(jnp.int32, (CHUNK, CHUNK), 0)
    cj = lax.broadcasted_iota(jnp.int32, (CHUNK, CHUNK), 1)
    lane_c = lax.broadcasted_iota(jnp.int32, (CHUNK, LANES), 1)
    for pb in range(SG_WIDTH // LANES):
        vcat = jnp.concatenate(
            [vb[c * CHUNK:(c + 1) * CHUNK, pb * LANES:(pb + 1) * LANES] for c in range(n_chunks)], axis=1)
        res = []
        for g in (2 * pb, 2 * pb + 1):
            w = jnp.where(cj <= ci, sgw_ref[g], 0.0).astype(BF16)
            res.append(jnp.dot(w, vcat, preferred_element_type=F32))
        for c in range(n_chunks):
            mixed = jnp.where(lane_c < HEAD_DIM,
                              res[0][:, c * LANES:(c + 1) * LANES],
                              res[1][:, c * LANES:(c + 1) * LANES])
            mixed = mixed + sgb_ref[:, pb * LANES:(pb + 1) * LANES]
            ysg_scr[c * CHUNK:(c + 1) * CHUNK, pb * LANES:(pb + 1) * LANES] = (
                u[c * CHUNK:(c + 1) * CHUNK, pb * LANES:(pb + 1) * LANES] * mixed)
    ysg_ref[0] = _rms(ysg_scr[...], sgog_ref[...]).astype(BF16)


def _attn_kernel(q_ref, k_ref, v_ref, o_ref, acc_ref, m_ref, s_scr, mc_scr):
    qi = pl.program_id(2)
    acc_ref[...] = jnp.zeros_like(acc_ref)
    m_ref[...] = jnp.full_like(m_ref, NEG_BIG)

    def scores(b, slot, masked):
        start = pl.multiple_of(b * T_ATT, T_ATT)
        for hh in range(2):
            q = q_ref[0, hh]
            kb = k_ref[0, hh, pl.ds(start, T_ATT), :]
            s = lax.dot_general(q, kb, (((1,), (1,)), ((), ())), preferred_element_type=F32)
            if masked:
                r = lax.broadcasted_iota(jnp.int32, (T_ATT, T_ATT), 0)
                c = lax.broadcasted_iota(jnp.int32, (T_ATT, T_ATT), 1)
                s = jnp.where(c <= r, s, NEG_BIG)
            s_scr[slot, hh] = s
            mc_scr[slot, hh] = jnp.broadcast_to(jnp.max(s, axis=-1, keepdims=True), (T_ATT, LANES))

    def update(b, slot):
        start = pl.multiple_of(b * T_ATT, T_ATT)
        for hh in range(2):
            vb = v_ref[0, hh, pl.ds(start, T_ATT), :]
            m_old = m_ref[hh]
            m_new = jnp.maximum(m_old, mc_scr[slot, hh])
            alpha = jnp.exp2(m_old - m_new)
            p = jnp.exp2(s_scr[slot, hh] - pltpu.repeat(m_new, T_ATT // LANES, axis=1)).astype(BF16)
            acc_ref[hh] = alpha * acc_ref[hh] + jnp.dot(p, vb, preferred_element_type=F32)
            m_ref[hh] = m_new

    @pl.when(qi == 0)
    def _():
        scores(0, 0, True)
        update(0, 0)

    @pl.when(qi > 0)
    def _():
        scores(0, 0, False)
        trips = (qi - 1) // 2

        def body(i, carry):
            b = 2 * i
            scores(b + 1, 1, False)
            update(b, 0)
            scores(b + 2, 0, False)
            update(b + 1, 1)
            return carry

        lax.fori_loop(0, trips, body, 0)
        b = 2 * trips

        @pl.when(qi - b == 1)
        def _():
            scores(qi, 1, True)
            update(b, 0)
            update(qi, 1)

        @pl.when(qi - b == 2)
        def _():
            scores(b + 1, 1, False)
            update(b, 0)
            scores(qi, 0, True)
            update(b + 1, 1)
            update(qi, 0)

    a0 = acc_ref[0]
    a1 = acc_ref[1]
    lane = lax.broadcasted_iota(jnp.int32, (T_ATT, LANES), 1)
    o_ref[0] = jnp.where(lane < HEAD_DIM, a0 / a0[:, HEAD_DIM:HEAD_DIM + 1], a1 / a1[:, 0:1])


def _mlp_kernel(x_ref, ya_ref, ys_ref, p_ref, woa_ref, wos_ref, w1_ref, w2_ref, wg_ref, wpe_ref,
                gatt_ref, gpm_ref, gpf_ref, gqf_ref, bg_ref,
                o_ref, h1_ref, cn_ref, acc_ref):
    c = pl.program_id(1)

    @pl.when(c == 0)
    def _():
        yan = _rms(ya_ref[...], gatt_ref[...]).astype(BF16)
        y = (jnp.dot(yan, woa_ref[...], preferred_element_type=F32)
             + jnp.dot(ys_ref[...], wos_ref[...], preferred_element_type=F32))
        h1 = x_ref[...] + _rms(y, gpm_ref[...])
        h1_ref[...] = h1
        cn_ref[...] = _rms(h1, gpf_ref[...]).astype(BF16)
        acc_ref[...] = jnp.zeros_like(acc_ref)

    hid = jnp.dot(cn_ref[...], w1_ref[...], preferred_element_type=F32)
    hid = jnp.square(jnp.maximum(hid, 0.0)).astype(BF16)
    acc_ref[...] += jnp.dot(hid, w2_ref[...], preferred_element_type=F32)

    @pl.when(c == pl.num_programs(1) - 1)
    def _():
        h2 = h1_ref[...] + _rms(acc_ref[...], gqf_ref[...])
        gate = jax.nn.sigmoid(jnp.dot(h2.astype(BF16), wg_ref[...], preferred_element_type=F32) + bg_ref[...])
        pe = jnp.dot(p_ref[...].astype(BF16), wpe_ref[...], preferred_element_type=F32)
        o_ref[...] = h2 + gate * pe


def _layer(h, p, w_in, f_bias, sg_ln_g, sg_ln_b, sg_w, sg_b, att_out_g, sg_out_g, w_out,
           pre_mix_g, post_mix_g, pre_ffn_g, post_ffn_g, w_ff1, w_ff2, ple_w, ple_gate_w, ple_gate_b):
    B, S, D = h.shape
    R = B * S
    row = lambda a: a.reshape(1, -1).astype(F32)

    wqkv = w_in[:, :3 * ATT_WIDTH].astype(BF16)
    wf8 = w_in[:, 3 * ATT_WIDTH:3 * ATT_WIDTH + N_HEADS]
    wf = jnp.concatenate([wf8, wf8, wf8, jnp.zeros((D, LANES - 3 * N_HEADS), w_in.dtype)], axis=1).astype(BF16)
    fb = jnp.concatenate([f_bias, f_bias, f_bias, jnp.zeros((LANES - 3 * N_HEADS,), f_bias.dtype)]).reshape(1, LANES)
    wuv = w_in[:, 3 * ATT_WIDTH + N_HEADS:].astype(BF16)
    sgb = jnp.repeat(jnp.transpose(sg_b), HEAD_DIM, axis=1).astype(F32)

    const2 = lambda shape: pl.BlockSpec(shape, lambda b, t: (0,) * len(shape))
    qp, kp, vp, ysg = pl.pallas_call(
        _inproj_kernel,
        grid=(B, S // T_IN),
        in_specs=[
            pl.BlockSpec((1, T_IN, D), lambda b, t: (b, t, 0)),
            const2((1, D)),
            const2((D, 3 * ATT_WIDTH)),
            const2((D, LANES)),
            const2((1, LANES)),
            const2((D, 2 * SG_WIDTH)),
            const2((1, SG_WIDTH)),
            const2((1, SG_WIDTH)),
            const2((N_SG_GROUPS, CHUNK, CHUNK)),
            const2((CHUNK, SG_WIDTH)),
            const2((1, SG_WIDTH)),
        ],
        out_specs=[
            pl.BlockSpec((1, N_HEADS, T_IN, LANES), lambda b, t: (b, 0, t, 0)),
            pl.BlockSpec((1, N_HEADS, T_IN, LANES), lambda b, t: (b, 0, t, 0)),
            pl.BlockSpec((1, N_HEADS, T_IN, LANES), lambda b, t: (b, 0, t, 0)),
            pl.BlockSpec((1, T_IN, SG_WIDTH), lambda b, t: (b, t, 0)),
        ],
        out_shape=[
            jax.ShapeDtypeStruct((B, N_HEADS, S, LANES), BF16),
            jax.ShapeDtypeStruct((B, N_HEADS, S, LANES), BF16),
            jax.ShapeDtypeStruct((B, N_HEADS, S, LANES), BF16),
            jax.ShapeDtypeStruct((B, S, SG_WIDTH), BF16),
        ],
        scratch_shapes=[pltpu.VMEM((1, LANES), F32), pltpu.VMEM((T_IN, SG_WIDTH), F32)],
        compiler_params=pltpu.CompilerParams(
            dimension_semantics=("arbitrary", "arbitrary"), vmem_limit_bytes=VMEM_LIMIT),
        name="inproj",
    )(h, row(pre_mix_g), wqkv, wf, fb, wuv, row(sg_ln_g), row(sg_ln_b), sg_w.astype(F32), sgb, row(sg_out_g))

    yatt = pl.pallas_call(
        _attn_kernel,
        grid=(B, HEAD_PAIRS, S // T_ATT),
        in_specs=[
            pl.BlockSpec((1, 2, T_ATT, LANES), lambda b, hp, i: (b, hp, i, 0)),
            pl.BlockSpec((1, 2, S, LANES), lambda b, hp, i: (b, hp, 0, 0)),
            pl.BlockSpec((1, 2, S, LANES), lambda b, hp, i: (b, hp, 0, 0)),
        ],
        out_specs=pl.BlockSpec((1, T_ATT, LANES), lambda b, hp, i: (b, i, hp)),
        out_shape=jax.ShapeDtypeStruct((B, S, ATT_WIDTH), F32),
        scratch_shapes=[pltpu.VMEM((2, T_ATT, LANES), F32), pltpu.VMEM((2, T_ATT, LANES), F32),
                        pltpu.VMEM((2, 2, T_ATT, T_ATT), F32), pltpu.VMEM((2, 2, T_ATT, LANES), F32)],
        compiler_params=pltpu.CompilerParams(
            dimension_semantics=("arbitrary", "arbitrary", "arbitrary"), vmem_limit_bytes=VMEM_LIMIT),
        name="fox_attn",
    )(qp, kp, vp)

    n_ff = D_FF // FF_CHUNK
    rows = lambda w: pl.BlockSpec((T_MLP, w), lambda i, c: (i, 0))
    const = lambda shape: pl.BlockSpec(shape, lambda i, c: (0,) * len(shape))
    out = pl.pallas_call(
        _mlp_kernel,
        grid=(R // T_MLP, n_ff),
        in_specs=[
            rows(D), rows(ATT_WIDTH), rows(SG_WIDTH), rows(PLE_DIM),
            const((ATT_WIDTH, D)), const((SG_WIDTH, D)),
            pl.BlockSpec((D, FF_CHUNK), lambda i, c: (0, c)),
            pl.BlockSpec((FF_CHUNK, D), lambda i, c: (c, 0)),
            const((D, D)), const((PLE_DIM, D)),
            const((1, ATT_WIDTH)), const((1, D)), const((1, D)), const((1, D)), const((1, D)),
        ],
        out_specs=rows(D),
        out_shape=jax.ShapeDtypeStruct((R, D), F32),
        scratch_shapes=[pltpu.VMEM((T_MLP, D), F32), pltpu.VMEM((T_MLP, D), BF16), pltpu.VMEM((T_MLP, D), F32)],
        compiler_params=pltpu.CompilerParams(
            dimension_semantics=("arbitrary", "arbitrary"), vmem_limit_bytes=VMEM_LIMIT),
        name="mlp",
    )(h.reshape(R, D), yatt.reshape(R, ATT_WIDTH), ysg.reshape(R, SG_WIDTH), p.reshape(R, PLE_DIM),
      w_out[:ATT_WIDTH].astype(BF16), w_out[ATT_WIDTH:].astype(BF16),
      w_ff1.astype(BF16), w_ff2.astype(BF16), ple_gate_w.astype(BF16), ple_w.astype(BF16),
      row(att_out_g), row(post_mix_g), row(pre_ffn_g), row(post_ffn_g), row(ple_gate_b))
    return out.reshape(B, S, D)


def kernel(x, p, w_in, f_bias, sg_ln_g, sg_ln_b, sg_w, sg_b, att_out_g, sg_out_g, w_out,
           pre_mix_g, post_mix_g, pre_ffn_g, post_ffn_g, w_ff1, w_ff2, ple_w, ple_gate_w, ple_gate_b):
    h = x
    for i in range(p.shape[0]):
        h = _layer(h, p[i], w_in[i], f_bias[i], sg_ln_g[i], sg_ln_b[i], sg_w[i], sg_b[i],
                   att_out_g[i], sg_out_g[i], w_out[i], pre_mix_g[i], post_mix_g[i],
                   pre_ffn_g[i], post_ffn_g[i], w_ff1[i], w_ff2[i], ple_w[i], ple_gate_w[i],
                   ple_gate_b[i])
    return h
```

```python
import functools

import jax
import jax.numpy as jnp
from jax import lax
from jax.experimental import pallas as pl
from jax.experimental.pallas import tpu as pltpu

D_MODEL = 1024
PLE_DIM = 256
HEAD_DIM = 64
ATT_WIDTH = 512
N_HEADS = 8
SG_WIDTH = 512
N_SG_GROUPS = 8
CHUNK = 128
D_FF = 4096
EPS = 1e-6

LANES = 128
HEAD_PAIRS = N_HEADS // 2
NEG_BIG = -1e30
LOG2E = 1.4426950408889634

T_IN = 512
T_ATT = 512
ATT_UNROLL = 4
T_MLP = 512
FF_CHUNK = 1024

VMEM_LIMIT = 56 * 1024 * 1024

F32 = jnp.float32
BF16 = jnp.bfloat16


def _rms(x, g):
    return x * lax.rsqrt(jnp.mean(x * x, axis=-1, keepdims=True) + EPS) * g


def _gelu_tanh(x):
    return x * (0.5 * (1.0 + jnp.tanh(0.7978845608028654 * (x + 0.044715 * (x * x * x)))))


def _split3(x):
    hi = x.astype(BF16).astype(F32)
    r = x - hi
    mid = r.astype(BF16).astype(F32)
    lo = (r - mid).astype(BF16).astype(F32)
    return hi, mid, lo


def _inproj_kernel(x_ref, g_ref, wqkv_ref, wf_ref, fb_ref, tril_ref, wuv_ref, lng_ref, lnb_ref,
                   sgw_ref, sgb_ref, sgog_ref,
                   qp_ref, kp_ref, vp_ref, ysg_ref,
                   carry_ref, ysg_scr):
    t = pl.program_id(1)

    @pl.when(t == 0)
    def _():
        carry_ref[...] = jnp.zeros_like(carry_ref)

    x = x_ref[0]
    ab = _rms(x, g_ref[...]).astype(BF16)

    f = jnp.dot(ab, wf_ref[...], preferred_element_type=F32) + fb_ref[...]
    logf = jnp.minimum(f, 0.0) - jnp.log1p(jnp.exp(-jnp.abs(f)))
    lane = lax.broadcasted_iota(jnp.int32, (T_IN, LANES), 1)
    hi, mid, lo = _split3(logf)
    pieces = jnp.where(lane < 24, hi, jnp.where(lane < 48, mid, lo)).astype(BF16)
    c3 = jnp.dot(tril_ref[...], pieces, preferred_element_type=F32) + carry_ref[...]
    carry_ref[...] = c3[T_IN - 1:T_IN, :]
    csum = c3 + pltpu.roll(c3, LANES - 24, axis=1) + pltpu.roll(c3, LANES - 48, axis=1)
    chi, cmid, clo = _split3(csum * LOG2E)
    cp = jnp.where(lane < 8, chi, jnp.where(lane < 16, cmid, clo))
    q_bias_lo = jnp.where(lane < 24, cp, jnp.where(lane < 48, 1.0, 0.0))
    k_bias_lo = jnp.where(lane < 24, 1.0, jnp.where(lane < 48, -pltpu.roll(cp, 24, axis=1), 0.0))
    q_bias_hi = pltpu.roll(q_bias_lo, HEAD_DIM, axis=1)
    k_bias_hi = pltpu.roll(k_bias_lo, HEAD_DIM, axis=1)

    zqkv = jnp.dot(ab, wqkv_ref[...], preferred_element_type=F32)
    for h in range(N_HEADS):
        blk = h // 2
        even = (h % 2 == 0)
        in_head = (lane < HEAD_DIM) if even else (lane >= HEAD_DIM)
        own_bias = ((lane & 7) == h) & ((lane & (HEAD_DIM - 1)) < 48)
        zq = zqkv[:, blk * LANES:(blk + 1) * LANES] * (HEAD_DIM ** -0.5 * LOG2E)
        zk = zqkv[:, ATT_WIDTH + blk * LANES:ATT_WIDTH + (blk + 1) * LANES]
        zv = zqkv[:, 2 * ATT_WIDTH + blk * LANES:2 * ATT_WIDTH + (blk + 1) * LANES]
        v_extra = jnp.where(lane == (HEAD_DIM if even else 0), 1.0, 0.0)
        qp_ref[0, h] = jnp.where(in_head, zq, q_bias_hi if even else q_bias_lo).astype(BF16)
        kp_ref[0, h] = jnp.where(in_head, zk,
                                 jnp.where(own_bias, k_bias_hi if even else k_bias_lo, 0.0)).astype(BF16)
        vp_ref[0, h] = jnp.where(in_head, zv, v_extra).astype(BF16)

    zuv = jnp.dot(ab, wuv_ref[...], preferred_element_type=F32)
    u = _gelu_tanh(zuv[:, :SG_WIDTH])
    v = _gelu_tanh(zuv[:, SG_WIDTH:])
    vc = v - jnp.mean(v, axis=-1, keepdims=True)
    vln = vc * lax.rsqrt(jnp.mean(vc * vc, axis=-1, keepdims=True) + EPS) * lng_ref[...] + lnb_ref[...]
    vb = vln.astype(BF16)
    n_chunks = T_IN // CHUNK
    ci = lax.broadcasted_iota(jnp.int32, (CHUNK, CHUNK), 0)
    cj = lax.broadcasted_iota(jnp.int32, (CHUNK, CHUNK), 1)
    lane_c = lax.broadcasted_iota(jnp.int32, (CHUNK, LANES), 1)
    for pb in range(SG_WIDTH // LANES):
        vcat = jnp.concatenate(
            [vb[c * CHUNK:(c + 1) * CHUNK, pb * LANES:(pb + 1) * LANES] for c in range(n_chunks)], axis=1)
        res = []
        for g in (2 * pb, 2 * pb + 1):
            w = jnp.where(cj <= ci, sgw_ref[g], 0.0).astype(BF16)
            res.append(jnp.dot(w, vcat, preferred_element_type=F32))
        for c in range(n_chunks):
            mixed = jnp.where(lane_c < HEAD_DIM,
                              res[0][:, c * LANES:(c + 1) * LANES],
                              res[1][:, c * LANES:(c + 1) * LANES])
            mixed = mixed + sgb_ref[:, pb * LANES:(pb + 1) * LANES]
            ysg_scr[c * CHUNK:(c + 1) * CHUNK, pb * LANES:(pb + 1) * LANES] = (
                u[c * CHUNK:(c + 1) * CHUNK, pb * LANES:(pb + 1) * LANES] * mixed)
    ysg_ref[0] = _rms(ysg_scr[...], sgog_ref[...]).astype(BF16)


def _attn_kernel(q_ref, k_ref, v_ref, o_ref, acc_ref, m_ref, s_scr, mc_scr):
    qi = pl.program_id(2)
    acc_ref[...] = jnp.zeros_like(acc_ref)
    m_ref[...] = jnp.full_like(m_ref, NEG_BIG)

    def scores(b, slot, masked):
        start = pl.multiple_of(b * T_ATT, T_ATT)
        for hh in range(2):
            q = q_ref[0, hh]
            kb = k_ref[0, hh, pl.ds(start, T_ATT), :]
            s = lax.dot_general(q, kb, (((1,), (1,)), ((), ())), preferred_element_type=F32)
            if masked:
                r = lax.broadcasted_iota(jnp.int32, (T_ATT, T_ATT), 0)
                c = lax.broadcasted_iota(jnp.int32, (T_ATT, T_ATT), 1)
                s = jnp.where(c <= r, s, NEG_BIG)
            s_scr[slot, hh] = s
            mc_scr[slot, hh] = jnp.broadcast_to(jnp.max(s, axis=-1, keepdims=True), (T_ATT, LANES))

    def update(b, slot):
        start = pl.multiple_of(b * T_ATT, T_ATT)
        for hh in range(2):
            vb = v_ref[0, hh, pl.ds(start, T_ATT), :]
            m_old = m_ref[hh]
            m_new = jnp.maximum(m_old, mc_scr[slot, hh])
            alpha = jnp.exp2(m_old - m_new)
            m_wide = jnp.concatenate([m_new] * (T_ATT // LANES), axis=1)
            p = jnp.exp2(s_scr[slot, hh] - m_wide).astype(BF16)
            acc_ref[hh] = alpha * acc_ref[hh] + jnp.dot(p, vb, preferred_element_type=F32)
            m_ref[hh] = m_new

    @pl.when(qi == 0)
    def _():
        scores(0, 0, True)
        update(0, 0)

    @pl.when(qi > 0)
    def _():
        scores(0, 0, False)
        trips = (qi - 1) // ATT_UNROLL

        def body(i, carry):
            b = ATT_UNROLL * i
            for u in range(ATT_UNROLL):
                scores(b + u + 1, (u + 1) % 2, False)
                update(b + u, u % 2)
            return carry

        lax.fori_loop(0, trips, body, 0)
        b = ATT_UNROLL * trips

        for rem in range(1, ATT_UNROLL + 1):
            @pl.when(qi - b == rem)
            def _():
                for u in range(rem):
                    scores(b + u + 1, (u + 1) % 2, u == rem - 1)
                    update(b + u, u % 2)
                update(qi, rem % 2)

    a0 = acc_ref[0]
    a1 = acc_ref[1]
    lane = lax.broadcasted_iota(jnp.int32, (T_ATT, LANES), 1)
    o_ref[0] = jnp.where(lane < HEAD_DIM, a0 / a0[:, HEAD_DIM:HEAD_DIM + 1], a1 / a1[:, 0:1])


def _mlp_kernel(x_ref, ya_ref, ys_ref, p_ref, woa_ref, wos_ref, w1_ref, w2_ref, wg_ref, wpe_ref,
                gatt_ref, gpm_ref, gpf_ref, gqf_ref, bg_ref,
                o_ref, h1_ref, cn_ref, acc_ref):
    c = pl.program_id(1)

    @pl.when(c == 0)
    def _():
        yan = _rms(ya_ref[...], gatt_ref[...]).astype(BF16)
        y = (jnp.dot(yan, woa_ref[...], preferred_element_type=F32)
             + jnp.dot(ys_ref[...], wos_ref[...], preferred_element_type=F32))
        h1 = x_ref[...] + _rms(y, gpm_ref[...])
        h1_ref[...] = h1
        cn_ref[...] = _rms(h1, gpf_ref[...]).astype(BF16)
        acc_ref[...] = jnp.zeros_like(acc_ref)

    hid = jnp.dot(cn_ref[...], w1_ref[...], preferred_element_type=F32)
    hid = jnp.square(jnp.maximum(hid, 0.0)).astype(BF16)
    acc_ref[...] += jnp.dot(hid, w2_ref[...], preferred_element_type=F32)

    @pl.when(c == pl.num_programs(1) - 1)
    def _():
        h2 = h1_ref[...] + _rms(acc_ref[...], gqf_ref[...])
        gate = jax.nn.sigmoid(jnp.dot(h2.astype(BF16), wg_ref[...], preferred_element_type=F32) + bg_ref[...])
        pe = jnp.dot(p_ref[...].astype(BF16), wpe_ref[...], preferred_element_type=F32)
        o_ref[...] = h2 + gate * pe


def _layer(h, p, w_in, f_bias, sg_ln_g, sg_ln_b, sg_w, sg_b, att_out_g, sg_out_g, w_out,
           pre_mix_g, post_mix_g, pre_ffn_g, post_ffn_g, w_ff1, w_ff2, ple_w, ple_gate_w, ple_gate_b):
    B, S, D = h.shape
    R = B * S
    row = lambda a: a.reshape(1, -1).astype(F32)

    wqkv = w_in[:, :3 * ATT_WIDTH].astype(BF16)
    wf8 = w_in[:, 3 * ATT_WIDTH:3 * ATT_WIDTH + N_HEADS]
    n_rep = 9
    wf = jnp.pad(jnp.tile(wf8, (1, n_rep)), ((0, 0), (0, LANES - n_rep * N_HEADS))).astype(BF16)
    fb = jnp.pad(jnp.tile(f_bias, n_rep), (0, LANES - n_rep * N_HEADS)).reshape(1, LANES).astype(F32)
    tril = jnp.tril(jnp.ones((T_IN, T_IN), BF16))
    wuv = w_in[:, 3 * ATT_WIDTH + N_HEADS:].astype(BF16)
    sgb = jnp.repeat(jnp.transpose(sg_b), HEAD_DIM, axis=1).astype(F32)

    const2 = lambda shape: pl.BlockSpec(shape, lambda b, t: (0,) * len(shape))
    qp, kp, vp, ysg = pl.pallas_call(
        _inproj_kernel,
        grid=(B, S // T_IN),
        in_specs=[
            pl.BlockSpec((1, T_IN, D), lambda b, t: (b, t, 0)),
            const2((1, D)),
            const2((D, 3 * ATT_WIDTH)),
            const2((D, LANES)),
            const2((1, LANES)),
            const2((T_IN, T_IN)),
            const2((D, 2 * SG_WIDTH)),
            const2((1, SG_WIDTH)),
            const2((1, SG_WIDTH)),
            const2((N_SG_GROUPS, CHUNK, CHUNK)),
            const2((CHUNK, SG_WIDTH)),
            const2((1, SG_WIDTH)),
        ],
        out_specs=[
            pl.BlockSpec((1, N_HEADS, T_IN, LANES), lambda b, t: (b, 0, t, 0)),
            pl.BlockSpec((1, N_HEADS, T_IN, LANES), lambda b, t: (b, 0, t, 0)),
            pl.BlockSpec((1, N_HEADS, T_IN, LANES), lambda b, t: (b, 0, t, 0)),
            pl.BlockSpec((1, T_IN, SG_WIDTH), lambda b, t: (b, t, 0)),
        ],
        out_shape=[
            jax.ShapeDtypeStruct((B, N_HEADS, S, LANES), BF16),
            jax.ShapeDtypeStruct((B, N_HEADS, S, LANES), BF16),
            jax.ShapeDtypeStruct((B, N_HEADS, S, LANES), BF16),
            jax.ShapeDtypeStruct((B, S, SG_WIDTH), BF16),
        ],
        scratch_shapes=[pltpu.VMEM((1, LANES), F32), pltpu.VMEM((T_IN, SG_WIDTH), F32)],
        compiler_params=pltpu.CompilerParams(
            dimension_semantics=("arbitrary", "arbitrary"), vmem_limit_bytes=VMEM_LIMIT),
        name="inproj",
    )(h, row(pre_mix_g), wqkv, wf, fb, tril, wuv, row(sg_ln_g), row(sg_ln_b), sg_w.astype(F32), sgb, row(sg_out_g))

    yatt = pl.pallas_call(
        _attn_kernel,
        grid=(B, HEAD_PAIRS, S // T_ATT),
        in_specs=[
            pl.BlockSpec((1, 2, T_ATT, LANES), lambda b, hp, i: (b, hp, i, 0)),
            pl.BlockSpec((1, 2, S, LANES), lambda b, hp, i: (b, hp, 0, 0)),
            pl.BlockSpec((1, 2, S, LANES), lambda b, hp, i: (b, hp, 0, 0)),
        ],
        out_specs=pl.BlockSpec((1, T_ATT, LANES), lambda b, hp, i: (b, i, hp)),
        out_shape=jax.ShapeDtypeStruct((B, S, ATT_WIDTH), F32),
        scratch_shapes=[pltpu.VMEM((2, T_ATT, LANES), F32), pltpu.VMEM((2, T_ATT, LANES), F32),
                        pltpu.VMEM((2, 2, T_ATT, T_ATT), F32), pltpu.VMEM((2, 2, T_ATT, LANES), F32)],
        compiler_params=pltpu.CompilerParams(
            dimension_semantics=("arbitrary", "arbitrary", "arbitrary"), vmem_limit_bytes=VMEM_LIMIT),
        name="fox_attn",
    )(qp, kp, vp)

    n_ff = D_FF // FF_CHUNK
    rows = lambda w: pl.BlockSpec((T_MLP, w), lambda i, c: (i, 0))
    const = lambda shape: pl.BlockSpec(shape, lambda i, c: (0,) * len(shape))
    out = pl.pallas_call(
        _mlp_kernel,
        grid=(R // T_MLP, n_ff),
        in_specs=[
            rows(D), rows(ATT_WIDTH), rows(SG_WIDTH), rows(PLE_DIM),
            const((ATT_WIDTH, D)), const((SG_WIDTH, D)),
            pl.BlockSpec((D, FF_CHUNK), lambda i, c: (0, c)),
            pl.BlockSpec((FF_CHUNK, D), lambda i, c: (c, 0)),
            const((D, D)), const((PLE_DIM, D)),
            const((1, ATT_WIDTH)), const((1, D)), const((1, D)), const((1, D)), const((1, D)),
        ],
        out_specs=rows(D),
        out_shape=jax.ShapeDtypeStruct((R, D), F32),
        scratch_shapes=[pltpu.VMEM((T_MLP, D), F32), pltpu.VMEM((T_MLP, D), BF16), pltpu.VMEM((T_MLP, D), F32)],
        compiler_params=pltpu.CompilerParams(
            dimension_semantics=("arbitrary", "arbitrary"), vmem_limit_bytes=VMEM_LIMIT),
        name="mlp",
    )(h.reshape(R, D), yatt.reshape(R, ATT_WIDTH), ysg.reshape(R, SG_WIDTH), p.reshape(R, PLE_DIM),
      w_out[:ATT_WIDTH].astype(BF16), w_out[ATT_WIDTH:].astype(BF16),
      w_ff1.astype(BF16), w_ff2.astype(BF16), ple_gate_w.astype(BF16), ple_w.astype(BF16),
      row(att_out_g), row(post_mix_g), row(pre_ffn_g), row(post_ffn_g), row(ple_gate_b))
    return out.reshape(B, S, D)


def kernel(x, p, w_in, f_bias, sg_ln_g, sg_ln_b, sg_w, sg_b, att_out_g, sg_out_g, w_out,
           pre_mix_g, post_mix_g, pre_ffn_g, post_ffn_g, w_ff1, w_ff2, ple_w, ple_gate_w, ple_gate_b):
    h = x
    for i in range(p.shape[0]):
        h = _layer(h, p[i], w_in[i], f_bias[i], sg_ln_g[i], sg_ln_b[i], sg_w[i], sg_b[i],
                   att_out_g[i], sg_out_g[i], w_out[i], pre_mix_g[i], post_mix_g[i],
                   pre_ffn_g[i], post_ffn_g[i], w_ff1[i], w_ff2[i], ple_w[i], ple_gate_w[i],
                   ple_gate_b[i])
    return h
```

```python
import functools

import jax
import jax.numpy as jnp
from jax import lax
from jax.experimental import pallas as pl
from jax.experimental.pallas import tpu as pltpu

D_MODEL = 1024
PLE_DIM = 256
HEAD_DIM = 64
ATT_WIDTH = 512
N_HEADS = 8
SG_WIDTH = 512
N_SG_GROUPS = 8
CHUNK = 128
D_FF = 4096
EPS = 1e-6

LANES = 128
HEAD_PAIRS = N_HEADS // 2
NEG_BIG = -1e30
LOG2E = 1.4426950408889634

T_IN = 512
T_ATT = 512
ATT_UNROLL = 4
T_MLP = 512
FF_CHUNK = 1024

VMEM_LIMIT = 56 * 1024 * 1024

F32 = jnp.float32
BF16 = jnp.bfloat16


def _rms(x, g):
    return x * lax.rsqrt(jnp.mean(x * x, axis=-1, keepdims=True) + EPS) * g


def _gelu_tanh(x):
    return x * (0.5 * (1.0 + jnp.tanh(0.7978845608028654 * (x + 0.044715 * (x * x * x)))))


def _split3(x):
    hi = x.astype(BF16).astype(F32)
    r = x - hi
    mid = r.astype(BF16).astype(F32)
    lo = (r - mid).astype(BF16).astype(F32)
    return hi, mid, lo


def _inproj_kernel(x_ref, g_ref, wqkv_ref, wf_ref, fb_ref, tril_ref, wuv_ref, lng_ref, lnb_ref,
                   sgw_ref, sgb_ref, sgog_ref,
                   qp_ref, kp_ref, vp_ref, ysg_ref,
                   carry_ref, ysg_scr):
    t = pl.program_id(1)

    @pl.when(t == 0)
    def _():
        carry_ref[...] = jnp.zeros_like(carry_ref)

    x = x_ref[0]
    ab = _rms(x, g_ref[...]).astype(BF16)
    lane = lax.broadcasted_iota(jnp.int32, (T_IN, LANES), 1)

    zuv = jnp.dot(ab, wuv_ref[...], preferred_element_type=F32)
    f = jnp.dot(ab, wf_ref[...], preferred_element_type=F32) + fb_ref[...]
    zqkv = jnp.dot(ab, wqkv_ref[...], preferred_element_type=F32)

    logf = jnp.minimum(f, 0.0) - jnp.log1p(jnp.exp(-jnp.abs(f)))
    hi, mid, lo = _split3(logf)
    pieces = jnp.where(lane < 24, hi, jnp.where(lane < 48, mid, lo)).astype(BF16)
    c3 = jnp.dot(tril_ref[...], pieces, preferred_element_type=F32) + carry_ref[...]
    carry_ref[...] = c3[T_IN - 1:T_IN, :]

    u = _gelu_tanh(zuv[:, :SG_WIDTH])
    v = _gelu_tanh(zuv[:, SG_WIDTH:])
    vc = v - jnp.mean(v, axis=-1, keepdims=True)
    vln = vc * lax.rsqrt(jnp.mean(vc * vc, axis=-1, keepdims=True) + EPS) * lng_ref[...] + lnb_ref[...]
    vb = vln.astype(BF16)
    n_chunks = T_IN // CHUNK
    ci = lax.broadcasted_iota(jnp.int32, (CHUNK, CHUNK), 0)
    cj = lax.broadcasted_iota(jnp.int32, (CHUNK, CHUNK), 1)
    lane_c = lax.broadcasted_iota(jnp.int32, (CHUNK, LANES), 1)
    for pb in range(SG_WIDTH // LANES):
        vcat = jnp.concatenate(
            [vb[c * CHUNK:(c + 1) * CHUNK, pb * LANES:(pb + 1) * LANES] for c in range(n_chunks)], axis=1)
        res = []
        for g in (2 * pb, 2 * pb + 1):
            w = jnp.where(cj <= ci, sgw_ref[g], 0.0).astype(BF16)
            res.append(jnp.dot(w, vcat, preferred_element_type=F32))
        for c in range(n_chunks):
            mixed = jnp.where(lane_c < HEAD_DIM,
                              res[0][:, c * LANES:(c + 1) * LANES],
                              res[1][:, c * LANES:(c + 1) * LANES])
            mixed = mixed + sgb_ref[:, pb * LANES:(pb + 1) * LANES]
            ysg_scr[c * CHUNK:(c + 1) * CHUNK, pb * LANES:(pb + 1) * LANES] = (
                u[c * CHUNK:(c + 1) * CHUNK, pb * LANES:(pb + 1) * LANES] * mixed)
    ysg_ref[0] = _rms(ysg_scr[...], sgog_ref[...]).astype(BF16)

    csum = c3 + pltpu.roll(c3, LANES - 24, axis=1) + pltpu.roll(c3, LANES - 48, axis=1)
    chi, cmid, clo = _split3(csum * LOG2E)
    cp = jnp.where(lane < 8, chi, jnp.where(lane < 16, cmid, clo))
    q_bias_lo = jnp.where(lane < 24, cp, jnp.where(lane < 48, 1.0, 0.0))
    k_bias_lo = jnp.where(lane < 24, 1.0, jnp.where(lane < 48, -pltpu.roll(cp, 24, axis=1), 0.0))
    q_bias_hi = pltpu.roll(q_bias_lo, HEAD_DIM, axis=1)
    k_bias_hi = pltpu.roll(k_bias_lo, HEAD_DIM, axis=1)

    for h in range(N_HEADS):
        blk = h // 2
        even = (h % 2 == 0)
        in_head = (lane < HEAD_DIM) if even else (lane >= HEAD_DIM)
        own_bias = ((lane & 7) == h) & ((lane & (HEAD_DIM - 1)) < 48)
        zq = zqkv[:, blk * LANES:(blk + 1) * LANES] * (HEAD_DIM ** -0.5 * LOG2E)
        zk = zqkv[:, ATT_WIDTH + blk * LANES:ATT_WIDTH + (blk + 1) * LANES]
        zv = zqkv[:, 2 * ATT_WIDTH + blk * LANES:2 * ATT_WIDTH + (blk + 1) * LANES]
        v_extra = jnp.where(lane == (HEAD_DIM if even else 0), 1.0, 0.0)
        qp_ref[0, h] = jnp.where(in_head, zq, q_bias_hi if even else q_bias_lo).astype(BF16)
        kp_ref[0, h] = jnp.where(in_head, zk,
                                 jnp.where(own_bias, k_bias_hi if even else k_bias_lo, 0.0)).astype(BF16)
        vp_ref[0, h] = jnp.where(in_head, zv, v_extra).astype(BF16)


def _attn_kernel(q_ref, k_ref, v_ref, o_ref, acc_ref, m_ref, s_scr, mc_scr):
    qi = pl.program_id(2)
    acc_ref[...] = jnp.zeros_like(acc_ref)
    m_ref[...] = jnp.full_like(m_ref, NEG_BIG)

    def scores(b, slot, masked):
        start = pl.multiple_of(b * T_ATT, T_ATT)
        for hh in range(2):
            q = q_ref[0, hh]
            kb = k_ref[0, hh, pl.ds(start, T_ATT), :]
            s = lax.dot_general(q, kb, (((1,), (1,)), ((), ())), preferred_element_type=F32)
            if masked:
                r = lax.broadcasted_iota(jnp.int32, (T_ATT, T_ATT), 0)
                c = lax.broadcasted_iota(jnp.int32, (T_ATT, T_ATT), 1)
                s = jnp.where(c <= r, s, NEG_BIG)
            s_scr[slot, hh] = s
            mc_scr[slot, hh] = jnp.broadcast_to(jnp.max(s, axis=-1, keepdims=True), (T_ATT, LANES))

    def update(b, slot):
        start = pl.multiple_of(b * T_ATT, T_ATT)
        for hh in range(2):
            vb = v_ref[0, hh, pl.ds(start, T_ATT), :]
            m_old = m_ref[hh]
            m_new = jnp.maximum(m_old, mc_scr[slot, hh])
            alpha = jnp.exp2(m_old - m_new)
            m_wide = jnp.concatenate([m_new] * (T_ATT // LANES), axis=1)
            p = jnp.exp2(s_scr[slot, hh] - m_wide).astype(BF16)
            acc_ref[hh] = alpha * acc_ref[hh] + jnp.dot(p, vb, preferred_element_type=F32)
            m_ref[hh] = m_new

    @pl.when(qi == 0)
    def _():
        scores(0, 0, True)
        update(0, 0)

    @pl.when(qi > 0)
    def _():
        scores(0, 0, False)
        trips = (qi - 1) // ATT_UNROLL

        def body(i, carry):
            b = ATT_UNROLL * i
            for u in range(ATT_UNROLL):
                scores(b + u + 1, (u + 1) % 2, False)
                update(b + u, u % 2)
            return carry

        lax.fori_loop(0, trips, body, 0)
        b = ATT_UNROLL * trips

        for rem in range(1, ATT_UNROLL + 1):
            @pl.when(qi - b == rem)
            def _():
                for u in range(rem):
                    scores(b + u + 1, (u + 1) % 2, u == rem - 1)
                    update(b + u, u % 2)
                update(qi, rem % 2)

    a0 = acc_ref[0]
    a1 = acc_ref[1]
    lane = lax.broadcasted_iota(jnp.int32, (T_ATT, LANES), 1)
    o_ref[0] = jnp.where(lane < HEAD_DIM, a0 / a0[:, HEAD_DIM:HEAD_DIM + 1], a1 / a1[:, 0:1])


def _mlp_kernel(x_ref, ya_ref, ys_ref, p_ref, woa_ref, wos_ref, w1_ref, w2_ref, wg_ref, wpe_ref,
                gatt_ref, gpm_ref, gpf_ref, gqf_ref, bg_ref, o_ref):
    yan = _rms(ya_ref[...], gatt_ref[...]).astype(BF16)
    y = (jnp.dot(yan, woa_ref[...], preferred_element_type=F32)
         + jnp.dot(ys_ref[...], wos_ref[...], preferred_element_type=F32))
    h1 = x_ref[...] + _rms(y, gpm_ref[...])
    cn = _rms(h1, gpf_ref[...]).astype(BF16)
    ff = None
    for c in range(D_FF // FF_CHUNK):
        cols = slice(c * FF_CHUNK, (c + 1) * FF_CHUNK)
        hid = jnp.dot(cn, w1_ref[:, cols], preferred_element_type=F32)
        hid = jnp.square(jnp.maximum(hid, 0.0)).astype(BF16)
        part = jnp.dot(hid, w2_ref[cols, :], preferred_element_type=F32)
        ff = part if ff is None else ff + part
    h2 = h1 + _rms(ff, gqf_ref[...])
    gate = jax.nn.sigmoid(jnp.dot(h2.astype(BF16), wg_ref[...], preferred_element_type=F32) + bg_ref[...])
    pe = jnp.dot(p_ref[...].astype(BF16), wpe_ref[...], preferred_element_type=F32)
    o_ref[...] = h2 + gate * pe


def _layer(h, p, w_in, f_bias, sg_ln_g, sg_ln_b, sg_w, sg_b, att_out_g, sg_out_g, w_out,
           pre_mix_g, post_mix_g, pre_ffn_g, post_ffn_g, w_ff1, w_ff2, ple_w, ple_gate_w, ple_gate_b):
    B, S, D = h.shape
    R = B * S
    row = lambda a: a.reshape(1, -1).astype(F32)

    wqkv = w_in[:, :3 * ATT_WIDTH].astype(BF16)
    wf8 = w_in[:, 3 * ATT_WIDTH:3 * ATT_WIDTH + N_HEADS]
    n_rep = 9
    wf = jnp.pad(jnp.tile(wf8, (1, n_rep)), ((0, 0), (0, LANES - n_rep * N_HEADS))).astype(BF16)
    fb = jnp.pad(jnp.tile(f_bias, n_rep), (0, LANES - n_rep * N_HEADS)).reshape(1, LANES).astype(F32)
    tril = jnp.tril(jnp.ones((T_IN, T_IN), BF16))
    wuv = w_in[:, 3 * ATT_WIDTH + N_HEADS:].astype(BF16)
    sgb = jnp.repeat(jnp.transpose(sg_b), HEAD_DIM, axis=1).astype(F32)

    const2 = lambda shape: pl.BlockSpec(shape, lambda b, t: (0,) * len(shape))
    qp, kp, vp, ysg = pl.pallas_call(
        _inproj_kernel,
        grid=(B, S // T_IN),
        in_specs=[
            pl.BlockSpec((1, T_IN, D), lambda b, t: (b, t, 0)),
            const2((1, D)),
            const2((D, 3 * ATT_WIDTH)),
            const2((D, LANES)),
            const2((1, LANES)),
            const2((T_IN, T_IN)),
            const2((D, 2 * SG_WIDTH)),
            const2((1, SG_WIDTH)),
            const2((1, SG_WIDTH)),
            const2((N_SG_GROUPS, CHUNK, CHUNK)),
            const2((CHUNK, SG_WIDTH)),
            const2((1, SG_WIDTH)),
        ],
        out_specs=[
            pl.BlockSpec((1, N_HEADS, T_IN, LANES), lambda b, t: (b, 0, t, 0)),
            pl.BlockSpec((1, N_HEADS, T_IN, LANES), lambda b, t: (b, 0, t, 0)),
            pl.BlockSpec((1, N_HEADS, T_IN, LANES), lambda b, t: (b, 0, t, 0)),
            pl.BlockSpec((1, T_IN, SG_WIDTH), lambda b, t: (b, t, 0)),
        ],
        out_shape=[
            jax.ShapeDtypeStruct((B, N_HEADS, S, LANES), BF16),
            jax.ShapeDtypeStruct((B, N_HEADS, S, LANES), BF16),
            jax.ShapeDtypeStruct((B, N_HEADS, S, LANES), BF16),
            jax.ShapeDtypeStruct((B, S, SG_WIDTH), BF16),
        ],
        scratch_shapes=[pltpu.VMEM((1, LANES), F32), pltpu.VMEM((T_IN, SG_WIDTH), F32)],
        compiler_params=pltpu.CompilerParams(
            dimension_semantics=("arbitrary", "arbitrary"), vmem_limit_bytes=VMEM_LIMIT),
        name="inproj",
    )(h, row(pre_mix_g), wqkv, wf, fb, tril, wuv, row(sg_ln_g), row(sg_ln_b), sg_w.astype(F32), sgb, row(sg_out_g))

    yatt = pl.pallas_call(
        _attn_kernel,
        grid=(B, HEAD_PAIRS, S // T_ATT),
        in_specs=[
            pl.BlockSpec((1, 2, T_ATT, LANES), lambda b, hp, i: (b, hp, i, 0)),
            pl.BlockSpec((1, 2, S, LANES), lambda b, hp, i: (b, hp, 0, 0)),
            pl.BlockSpec((1, 2, S, LANES), lambda b, hp, i: (b, hp, 0, 0)),
        ],
        out_specs=pl.BlockSpec((1, T_ATT, LANES), lambda b, hp, i: (b, i, hp)),
        out_shape=jax.ShapeDtypeStruct((B, S, ATT_WIDTH), F32),
        scratch_shapes=[pltpu.VMEM((2, T_ATT, LANES), F32), pltpu.VMEM((2, T_ATT, LANES), F32),
                        pltpu.VMEM((2, 2, T_ATT, T_ATT), F32), pltpu.VMEM((2, 2, T_ATT, LANES), F32)],
        compiler_params=pltpu.CompilerParams(
            dimension_semantics=("arbitrary", "arbitrary", "arbitrary"), vmem_limit_bytes=VMEM_LIMIT),
        name="fox_attn",
    )(qp, kp, vp)

    rows = lambda w: pl.BlockSpec((T_MLP, w), lambda i: (i, 0))
    const = lambda shape: pl.BlockSpec(shape, lambda i: (0,) * len(shape), pipeline_mode=pl.Buffered(1))
    out = pl.pallas_call(
        _mlp_kernel,
        grid=(R // T_MLP,),
        in_specs=[
            rows(D), rows(ATT_WIDTH), rows(SG_WIDTH), rows(PLE_DIM),
            const((ATT_WIDTH, D)), const((SG_WIDTH, D)),
            const((D, D_FF)), const((D_FF, D)),
            const((D, D)), const((PLE_DIM, D)),
            const((1, ATT_WIDTH)), const((1, D)), const((1, D)), const((1, D)), const((1, D)),
        ],
        out_specs=rows(D),
        out_shape=jax.ShapeDtypeStruct((R, D), F32),
        compiler_params=pltpu.CompilerParams(
            dimension_semantics=("arbitrary",), vmem_limit_bytes=VMEM_LIMIT),
        name="mlp",
    )(h.reshape(R, D), yatt.reshape(R, ATT_WIDTH), ysg.reshape(R, SG_WIDTH), p.reshape(R, PLE_DIM),
      w_out[:ATT_WIDTH].astype(BF16), w_out[ATT_WIDTH:].astype(BF16),
      w_ff1.astype(BF16), w_ff2.astype(BF16), ple_gate_w.astype(BF16), ple_w.astype(BF16),
      row(att_out_g), row(post_mix_g), row(pre_ffn_g), row(post_ffn_g), row(ple_gate_b))
    return out.reshape(B, S, D)


def kernel(x, p, w_in, f_bias, sg_ln_g, sg_ln_b, sg_w, sg_b, att_out_g, sg_out_g, w_out,
           pre_mix_g, post_mix_g, pre_ffn_g, post_ffn_g, w_ff1, w_ff2, ple_w, ple_gate_w, ple_gate_b):
    h = x
    for i in range(p.shape[0]):
        h = _layer(h, p[i], w_in[i], f_bias[i], sg_ln_g[i], sg_ln_b[i], sg_w[i], sg_b[i],
                   att_out_g[i], sg_out_g[i], w_out[i], pre_mix_g[i], post_mix_g[i],
                   pre_ffn_g[i], post_ffn_g[i], w_ff1[i], w_ff2[i], ple_w[i], ple_gate_w[i],
                   ple_gate_b[i])
    return h
```

```python
import functools

import jax
import jax.numpy as jnp
from jax import lax
from jax.experimental import pallas as pl
from jax.experimental.pallas import tpu as pltpu

D_MODEL = 1024
PLE_DIM = 256
HEAD_DIM = 64
ATT_WIDTH = 512
N_HEADS = 8
SG_WIDTH = 512
N_SG_GROUPS = 8
CHUNK = 128
D_FF = 4096
EPS = 1e-6

LANES = 128
HEAD_PAIRS = N_HEADS // 2
NEG_BIG = -1e30
LOG2E = 1.4426950408889634

T_IN = 512
T_SUB = 256
T_Q = 2048
T_K = 512
STAIRS = T_Q // T_K
ATT_UNROLL = 2
assert ATT_UNROLL == 2 and STAIRS % 2 == 0
T_MLP = 512
FF_CHUNK = 1024

VMEM_LIMIT = 56 * 1024 * 1024

F32 = jnp.float32
BF16 = jnp.bfloat16


def _rms(x, g):
    return x * lax.rsqrt(jnp.mean(x * x, axis=-1, keepdims=True) + EPS) * g


def _gelu_tanh(x):
    return x * (0.5 * (1.0 + jnp.tanh(0.7978845608028654 * (x + 0.044715 * (x * x * x)))))


def _split3(x):
    hi = x.astype(BF16).astype(F32)
    r = x - hi
    mid = r.astype(BF16).astype(F32)
    lo = (r - mid).astype(BF16).astype(F32)
    return hi, mid, lo


def _inproj_kernel(x_ref, g_ref, wqkv_ref, wf_ref, fb_ref, tril_ref, wuv_ref, lng_ref, lnb_ref,
                   sgw_ref, sgb_ref, sgog_ref,
                   qp_ref, kp_ref, vp_ref, ysg_ref,
                   carry_ref, ysg_scr):
    t = pl.program_id(1)

    @pl.when(t == 0)
    def _():
        carry_ref[...] = jnp.zeros_like(carry_ref)

    for r0 in range(0, T_IN, T_SUB):
        _inproj_rows(r0, x_ref, g_ref, wqkv_ref, wf_ref, fb_ref, tril_ref, wuv_ref, lng_ref, lnb_ref,
                     sgw_ref, sgb_ref, sgog_ref, qp_ref, kp_ref, vp_ref, ysg_ref, carry_ref, ysg_scr)


def _inproj_rows(r0, x_ref, g_ref, wqkv_ref, wf_ref, fb_ref, tril_ref, wuv_ref, lng_ref, lnb_ref,
                 sgw_ref, sgb_ref, sgog_ref, qp_ref, kp_ref, vp_ref, ysg_ref, carry_ref, ysg_scr):
    rows = slice(r0, r0 + T_SUB)
    x = x_ref[0, rows, :]
    ab = _rms(x, g_ref[...]).astype(BF16)
    lane = lax.broadcasted_iota(jnp.int32, (T_SUB, LANES), 1)

    zuv = jnp.dot(ab, wuv_ref[...], preferred_element_type=F32)
    f = jnp.dot(ab, wf_ref[...], preferred_element_type=F32) + fb_ref[...]
    zqkv = jnp.dot(ab, wqkv_ref[...], preferred_element_type=F32)

    logf = jnp.minimum(f, 0.0) - jnp.log1p(jnp.exp(-jnp.abs(f)))
    hi, mid, lo = _split3(logf)
    pieces = jnp.where(lane < 24, hi, jnp.where(lane < 48, mid, lo)).astype(BF16)
    c3 = jnp.dot(tril_ref[...], pieces, preferred_element_type=F32) + carry_ref[...]
    carry_ref[...] = c3[T_SUB - 1:T_SUB, :]

    u = _gelu_tanh(zuv[:, :SG_WIDTH])
    v = _gelu_tanh(zuv[:, SG_WIDTH:])
    vc = v - jnp.mean(v, axis=-1, keepdims=True)
    vln = vc * lax.rsqrt(jnp.mean(vc * vc, axis=-1, keepdims=True) + EPS) * lng_ref[...] + lnb_ref[...]
    vb = vln.astype(BF16)
    n_chunks = T_SUB // CHUNK
    ci = lax.broadcasted_iota(jnp.int32, (CHUNK, CHUNK), 0)
    cj = lax.broadcasted_iota(jnp.int32, (CHUNK, CHUNK), 1)
    lane_c = lax.broadcasted_iota(jnp.int32, (CHUNK, LANES), 1)
    for pb in range(SG_WIDTH // LANES):
        vcat = jnp.concatenate(
            [vb[c * CHUNK:(c + 1) * CHUNK, pb * LANES:(pb + 1) * LANES] for c in range(n_chunks)], axis=1)
        res = []
        for g in (2 * pb, 2 * pb + 1):
            w = jnp.where(cj <= ci, sgw_ref[g], 0.0).astype(BF16)
            res.append(jnp.dot(w, vcat, preferred_element_type=F32))
        for c in range(n_chunks):
            mixed = jnp.where(lane_c < HEAD_DIM,
                              res[0][:, c * LANES:(c + 1) * LANES],
                              res[1][:, c * LANES:(c + 1) * LANES])
            mixed = mixed + sgb_ref[:, pb * LANES:(pb + 1) * LANES]
            ysg_scr[r0 + c * CHUNK:r0 + (c + 1) * CHUNK, pb * LANES:(pb + 1) * LANES] = (
                u[c * CHUNK:(c + 1) * CHUNK, pb * LANES:(pb + 1) * LANES] * mixed)
    ysg_ref[0, rows, :] = _rms(ysg_scr[rows, :], sgog_ref[...]).astype(BF16)

    csum = c3 + pltpu.roll(c3, LANES - 24, axis=1) + pltpu.roll(c3, LANES - 48, axis=1)
    chi, cmid, clo = _split3(csum * LOG2E)
    cp = jnp.where(lane < 8, chi, jnp.where(lane < 16, cmid, clo))
    q_bias_lo = jnp.where(lane < 24, cp, jnp.where(lane < 48, 1.0, 0.0))
    k_bias_lo = jnp.where(lane < 24, 1.0, jnp.where(lane < 48, -pltpu.roll(cp, 24, axis=1), 0.0))
    q_bias_hi = pltpu.roll(q_bias_lo, HEAD_DIM, axis=1)
    k_bias_hi = pltpu.roll(k_bias_lo, HEAD_DIM, axis=1)

    for h in range(N_HEADS):
        blk = h // 2
        even = (h % 2 == 0)
        in_head = (lane < HEAD_DIM) if even else (lane >= HEAD_DIM)
        own_bias = ((lane & 7) == h) & ((lane & (HEAD_DIM - 1)) < 48)
        zq = zqkv[:, blk * LANES:(blk + 1) * LANES] * (HEAD_DIM ** -0.5 * LOG2E)
        zk = zqkv[:, ATT_WIDTH + blk * LANES:ATT_WIDTH + (blk + 1) * LANES]
        zv = zqkv[:, 2 * ATT_WIDTH + blk * LANES:2 * ATT_WIDTH + (blk + 1) * LANES]
        v_extra = jnp.where(lane == (HEAD_DIM if even else 0), 1.0, 0.0)
        qp_ref[0, h, rows, :] = jnp.where(in_head, zq, q_bias_hi if even else q_bias_lo).astype(BF16)
        kp_ref[0, h, rows, :] = jnp.where(
            in_head, zk, jnp.where(own_bias, k_bias_hi if even else k_bias_lo, 0.0)).astype(BF16)
        vp_ref[0, h, rows, :] = jnp.where(in_head, zv, v_extra).astype(BF16)


def _attn_kernel(q_ref, k_ref, v_ref, o_ref, acc_ref, m_ref, s_scr, mc_scr):
    n_full = pl.program_id(2) * STAIRS
    acc_ref[...] = jnp.zeros_like(acc_ref)
    m_ref[...] = jnp.full_like(m_ref, NEG_BIG)

    def scores(b, slot, stair=None):
        row0 = 0 if stair is None else stair * T_K
        start = pl.multiple_of(b * T_K, T_K)
        for hh in range(2):
            q = q_ref[0, hh, row0:, :]
            kb = k_ref[0, hh, pl.ds(start, T_K), :]
            s = lax.dot_general(q, kb, (((1,), (1,)), ((), ())), preferred_element_type=F32)
            parts = [(row0, T_Q, s)]
            if stair is not None:
                r = lax.broadcasted_iota(jnp.int32, (T_K, T_K), 0)
                c = lax.broadcasted_iota(jnp.int32, (T_K, T_K), 1)
                parts = [(row0, row0 + T_K, jnp.where(c <= r, s[:T_K], NEG_BIG))]
                if row0 + T_K < T_Q:
                    parts.append((row0 + T_K, T_Q, s[T_K:]))
            for lo, hi, sp in parts:
                s_scr[slot, hh, lo:hi, :] = sp
                mc_scr[slot, hh, lo:hi, :] = jnp.broadcast_to(
                    jnp.max(sp, axis=-1, keepdims=True), (hi - lo, LANES))

    def update(b, slot, row0=0):
        start = pl.multiple_of(b * T_K, T_K)
        for hh in range(2):
            vb = v_ref[0, hh, pl.ds(start, T_K), :]
            m_old = m_ref[hh, row0:, :]
            m_new = jnp.maximum(m_old, mc_scr[slot, hh, row0:, :])
            alpha = jnp.exp2(m_old - m_new)
            m_wide = jnp.concatenate([m_new] * (T_K // LANES), axis=1)
            p = jnp.exp2(s_scr[slot, hh, row0:, :] - m_wide).astype(BF16)
            acc_ref[hh, row0:, :] = alpha * acc_ref[hh, row0:, :] + jnp.dot(p, vb, preferred_element_type=F32)
            m_ref[hh, row0:, :] = m_new

    def staircase(first):
        for k in range(STAIRS - 1):
            scores(first + k + 1, (k + 1) % 2, stair=k + 1)
            update(first + k, k % 2, row0=k * T_K)
        update(first + STAIRS - 1, (STAIRS - 1) % 2, row0=(STAIRS - 1) * T_K)

    @pl.when(n_full == 0)
    def _():
        scores(0, 0, stair=0)
        staircase(0)

    @pl.when(n_full > 0)
    def _():
        scores(0, 0)
        trips = (n_full - 1) // ATT_UNROLL

        def body(i, carry):
            b = ATT_UNROLL * i
            for u in range(ATT_UNROLL):
                scores(b + u + 1, (u + 1) % 2)
                update(b + u, u % 2)
            return carry

        lax.fori_loop(0, trips, body, 0)
        scores(n_full - 1, 1)
        update(n_full - 2, 0)
        scores(n_full, 0, stair=0)
        update(n_full - 1, 1)
        staircase(n_full)

    a0 = acc_ref[0]
    a1 = acc_ref[1]
    lane = lax.broadcasted_iota(jnp.int32, (T_Q, LANES), 1)
    o_ref[0] = jnp.where(lane < HEAD_DIM, a0 / a0[:, HEAD_DIM:HEAD_DIM + 1], a1 / a1[:, 0:1])


def _mlp_kernel(x_ref, ya_ref, ys_ref, p_ref, woa_ref, wos_ref, w1_ref, w2_ref, wg_ref, wpe_ref,
                gatt_ref, gpm_ref, gpf_ref, gqf_ref, bg_ref, o_ref):
    yan = _rms(ya_ref[...], gatt_ref[...]).astype(BF16)
    y = (jnp.dot(yan, woa_ref[...], preferred_element_type=F32)
         + jnp.dot(ys_ref[...], wos_ref[...], preferred_element_type=F32))
    h1 = x_ref[...] + _rms(y, gpm_ref[...])
    cn = _rms(h1, gpf_ref[...]).astype(BF16)
    ff = None
    for c in range(D_FF // FF_CHUNK):
        cols = slice(c * FF_CHUNK, (c + 1) * FF_CHUNK)
        hid = jnp.dot(cn, w1_ref[:, cols], preferred_element_type=F32)
        hid = jnp.square(jnp.maximum(hid, 0.0)).astype(BF16)
        part = jnp.dot(hid, w2_ref[cols, :], preferred_element_type=F32)
        ff = part if ff is None else ff + part
    h2 = h1 + _rms(ff, gqf_ref[...])
    gate = jax.nn.sigmoid(jnp.dot(h2.astype(BF16), wg_ref[...], preferred_element_type=F32) + bg_ref[...])
    pe = jnp.dot(p_ref[...].astype(BF16), wpe_ref[...], preferred_element_type=F32)
    o_ref[...] = h2 + gate * pe


def _layer(h, p, w_in, f_bias, sg_ln_g, sg_ln_b, sg_w, sg_b, att_out_g, sg_out_g, w_out,
           pre_mix_g, post_mix_g, pre_ffn_g, post_ffn_g, w_ff1, w_ff2, ple_w, ple_gate_w, ple_gate_b):
    B, S, D = h.shape
    R = B * S
    row = lambda a: a.reshape(1, -1).astype(F32)

    wqkv = w_in[:, :3 * ATT_WIDTH].astype(BF16)
    wf8 = w_in[:, 3 * ATT_WIDTH:3 * ATT_WIDTH + N_HEADS]
    n_rep = 9
    wf = jnp.pad(jnp.tile(wf8, (1, n_rep)), ((0, 0), (0, LANES - n_rep * N_HEADS))).astype(BF16)
    fb = jnp.pad(jnp.tile(f_bias, n_rep), (0, LANES - n_rep * N_HEADS)).reshape(1, LANES).astype(F32)
    tril = jnp.tril(jnp.ones((T_SUB, T_SUB), BF16))
    wuv = w_in[:, 3 * ATT_WIDTH + N_HEADS:].astype(BF16)
    sgb = jnp.repeat(jnp.transpose(sg_b), HEAD_DIM, axis=1).astype(F32)

    const2 = lambda shape: pl.BlockSpec(shape, lambda b, t: (0,) * len(shape))
    qp, kp, vp, ysg = pl.pallas_call(
        _inproj_kernel,
        grid=(B, S // T_IN),
        in_specs=[
            pl.BlockSpec((1, T_IN, D), lambda b, t: (b, t, 0)),
            const2((1, D)),
            const2((D, 3 * ATT_WIDTH)),
            const2((D, LANES)),
            const2((1, LANES)),
            const2((T_SUB, T_SUB)),
            const2((D, 2 * SG_WIDTH)),
            const2((1, SG_WIDTH)),
            const2((1, SG_WIDTH)),
            const2((N_SG_GROUPS, CHUNK, CHUNK)),
            const2((CHUNK, SG_WIDTH)),
            const2((1, SG_WIDTH)),
        ],
        out_specs=[
            pl.BlockSpec((1, N_HEADS, T_IN, LANES), lambda b, t: (b, 0, t, 0)),
            pl.BlockSpec((1, N_HEADS, T_IN, LANES), lambda b, t: (b, 0, t, 0)),
            pl.BlockSpec((1, N_HEADS, T_IN, LANES), lambda b, t: (b, 0, t, 0)),
            pl.BlockSpec((1, T_IN, SG_WIDTH), lambda b, t: (b, t, 0)),
        ],
        out_shape=[
            jax.ShapeDtypeStruct((B, N_HEADS, S, LANES), BF16),
            jax.ShapeDtypeStruct((B, N_HEADS, S, LANES), BF16),
            jax.ShapeDtypeStruct((B, N_HEADS, S, LANES), BF16),
            jax.ShapeDtypeStruct((B, S, SG_WIDTH), BF16),
        ],
        scratch_shapes=[pltpu.VMEM((1, LANES), F32), pltpu.VMEM((T_IN, SG_WIDTH), F32)],
        compiler_params=pltpu.CompilerParams(
            dimension_semantics=("arbitrary", "arbitrary"), vmem_limit_bytes=VMEM_LIMIT),
        name="inproj",
    )(h, row(pre_mix_g), wqkv, wf, fb, tril, wuv, row(sg_ln_g), row(sg_ln_b), sg_w.astype(F32), sgb, row(sg_out_g))

    yatt = pl.pallas_call(
        _attn_kernel,
        grid=(B, HEAD_PAIRS, S // T_Q),
        in_specs=[
            pl.BlockSpec((1, 2, T_Q, LANES), lambda b, hp, i: (b, hp, i, 0)),
            pl.BlockSpec((1, 2, S, LANES), lambda b, hp, i: (b, hp, 0, 0)),
            pl.BlockSpec((1, 2, S, LANES), lambda b, hp, i: (b, hp, 0, 0)),
        ],
        out_specs=pl.BlockSpec((1, T_Q, LANES), lambda b, hp, i: (b, i, hp)),
        out_shape=jax.ShapeDtypeStruct((B, S, ATT_WIDTH), F32),
        scratch_shapes=[pltpu.VMEM((2, T_Q, LANES), F32), pltpu.VMEM((2, T_Q, LANES), F32),
                        pltpu.VMEM((2, 2, T_Q, T_K), F32), pltpu.VMEM((2, 2, T_Q, LANES), F32)],
        compiler_params=pltpu.CompilerParams(
            dimension_semantics=("arbitrary", "arbitrary", "arbitrary"), vmem_limit_bytes=VMEM_LIMIT),
        name="fox_attn",
    )(qp, kp, vp)

    rows = lambda w: pl.BlockSpec((T_MLP, w), lambda i: (i, 0))
    const = lambda shape: pl.BlockSpec(shape, lambda i: (0,) * len(shape), pipeline_mode=pl.Buffered(1))
    out = pl.pallas_call(
        _mlp_kernel,
        grid=(R // T_MLP,),
        in_specs=[
            rows(D), rows(ATT_WIDTH), rows(SG_WIDTH), rows(PLE_DIM),
            const((ATT_WIDTH, D)), const((SG_WIDTH, D)),
            const((D, D_FF)), const((D_FF, D)),
            const((D, D)), const((PLE_DIM, D)),
            const((1, ATT_WIDTH)), const((1, D)), const((1, D)), const((1, D)), const((1, D)),
        ],
        out_specs=rows(D),
        out_shape=jax.ShapeDtypeStruct((R, D), F32),
        compiler_params=pltpu.CompilerParams(
            dimension_semantics=("arbitrary",), vmem_limit_bytes=VMEM_LIMIT),
        name="mlp",
    )(h.reshape(R, D), yatt.reshape(R, ATT_WIDTH), ysg.reshape(R, SG_WIDTH), p.reshape(R, PLE_DIM),
      w_out[:ATT_WIDTH].astype(BF16), w_out[ATT_WIDTH:].astype(BF16),
      w_ff1.astype(BF16), w_ff2.astype(BF16), ple_gate_w.astype(BF16), ple_w.astype(BF16),
      row(att_out_g), row(post_mix_g), row(pre_ffn_g), row(post_ffn_g), row(ple_gate_b))
    return out.reshape(B, S, D)


def kernel(x, p, w_in, f_bias, sg_ln_g, sg_ln_b, sg_w, sg_b, att_out_g, sg_out_g, w_out,
           pre_mix_g, post_mix_g, pre_ffn_g, post_ffn_g, w_ff1, w_ff2, ple_w, ple_gate_w, ple_gate_b):
    h = x
    for i in range(p.shape[0]):
        h = _layer(h, p[i], w_in[i], f_bias[i], sg_ln_g[i], sg_ln_b[i], sg_w[i], sg_b[i],
                   att_out_g[i], sg_out_g[i], w_out[i], pre_mix_g[i], post_mix_g[i],
                   pre_ffn_g[i], post_ffn_g[i], w_ff1[i], w_ff2[i], ple_w[i], ple_gate_w[i],
                   ple_gate_b[i])
    return h
```

```python
import functools

import jax
import jax.numpy as jnp
from jax import lax
from jax.experimental import pallas as pl
from jax.experimental.pallas import tpu as pltpu

D_MODEL = 1024
PLE_DIM = 256
HEAD_DIM = 64
ATT_WIDTH = 512
N_HEADS = 8
SG_WIDTH = 512
N_SG_GROUPS = 8
CHUNK = 128
D_FF = 4096
EPS = 1e-6

LANES = 128
HEAD_PAIRS = N_HEADS // 2
NEG_BIG = -1e30
LOG2E = 1.4426950408889634

T_IN = 512
T_SUB = 512
T_Q = 2048
T_K = 512
STAIRS = T_Q // T_K
ATT_UNROLL = 2
assert ATT_UNROLL == 2 and STAIRS % 2 == 0
T_MLP = 512
FF_CHUNK = 1024

VMEM_LIMIT = 56 * 1024 * 1024

F32 = jnp.float32
BF16 = jnp.bfloat16


def _rms(x, g):
    return x * lax.rsqrt(jnp.mean(x * x, axis=-1, keepdims=True) + EPS) * g


def _gelu_tanh(x):
    return x * (0.5 * (1.0 + jnp.tanh(0.7978845608028654 * (x + 0.044715 * (x * x * x)))))


def _split3(x):
    hi = x.astype(BF16).astype(F32)
    r = x - hi
    mid = r.astype(BF16).astype(F32)
    lo = (r - mid).astype(BF16).astype(F32)
    return hi, mid, lo


def _inproj_kernel(x_ref, g_ref, wqkv_ref, wf_ref, fb_ref, tril_ref, wuv_ref, lng_ref, lnb_ref,
                   sgw_ref, sgb_ref, sgog_ref,
                   qp_ref, kp_ref, vp_ref, ysg_ref,
                   carry_ref, ysg_scr):
    t = pl.program_id(1)

    @pl.when(t == 0)
    def _():
        carry_ref[...] = jnp.zeros_like(carry_ref)

    for r0 in range(0, T_IN, T_SUB):
        _inproj_rows(r0, x_ref, g_ref, wqkv_ref, wf_ref, fb_ref, tril_ref, wuv_ref, lng_ref, lnb_ref,
                     sgw_ref, sgb_ref, sgog_ref, qp_ref, kp_ref, vp_ref, ysg_ref, carry_ref, ysg_scr)


def _inproj_rows(r0, x_ref, g_ref, wqkv_ref, wf_ref, fb_ref, tril_ref, wuv_ref, lng_ref, lnb_ref,
                 sgw_ref, sgb_ref, sgog_ref, qp_ref, kp_ref, vp_ref, ysg_ref, carry_ref, ysg_scr):
    rows = slice(r0, r0 + T_SUB)
    x = x_ref[0, rows, :]
    ab = _rms(x, g_ref[...]).astype(BF16)
    lane = lax.broadcasted_iota(jnp.int32, (T_SUB, LANES), 1)

    zuv = jnp.dot(ab, wuv_ref[...], preferred_element_type=F32)
    f = jnp.dot(ab, wf_ref[...], preferred_element_type=F32) + fb_ref[...]
    zqkv = jnp.dot(ab, wqkv_ref[...], preferred_element_type=F32)

    logf = jnp.minimum(f, 0.0) - jnp.log1p(jnp.exp(-jnp.abs(f)))
    hi, mid, lo = _split3(logf)
    pieces = jnp.where(lane < 24, hi, jnp.where(lane < 48, mid, lo)).astype(BF16)
    c3 = jnp.dot(tril_ref[...], pieces, preferred_element_type=F32) + carry_ref[...]
    carry_ref[...] = c3[T_SUB - 1:T_SUB, :]

    u = _gelu_tanh(zuv[:, :SG_WIDTH])
    v = _gelu_tanh(zuv[:, SG_WIDTH:])
    vc = v - jnp.mean(v, axis=-1, keepdims=True)
    vln = vc * lax.rsqrt(jnp.mean(vc * vc, axis=-1, keepdims=True) + EPS) * lng_ref[...] + lnb_ref[...]
    vb = vln.astype(BF16)
    n_chunks = T_SUB // CHUNK
    ci = lax.broadcasted_iota(jnp.int32, (CHUNK, CHUNK), 0)
    cj = lax.broadcasted_iota(jnp.int32, (CHUNK, CHUNK), 1)
    lane_c = lax.broadcasted_iota(jnp.int32, (CHUNK, LANES), 1)
    for pb in range(SG_WIDTH // LANES):
        vcat = jnp.concatenate(
            [vb[c * CHUNK:(c + 1) * CHUNK, pb * LANES:(pb + 1) * LANES] for c in range(n_chunks)], axis=1)
        res = []
        for g in (2 * pb, 2 * pb + 1):
            w = jnp.where(cj <= ci, sgw_ref[g], 0.0).astype(BF16)
            res.append(jnp.dot(w, vcat, preferred_element_type=F32))
        for c in range(n_chunks):
            mixed = jnp.where(lane_c < HEAD_DIM,
                              res[0][:, c * LANES:(c + 1) * LANES],
                              res[1][:, c * LANES:(c + 1) * LANES])
            mixed = mixed + sgb_ref[:, pb * LANES:(pb + 1) * LANES]
            ysg_scr[r0 + c * CHUNK:r0 + (c + 1) * CHUNK, pb * LANES:(pb + 1) * LANES] = (
                u[c * CHUNK:(c + 1) * CHUNK, pb * LANES:(pb + 1) * LANES] * mixed)
    ysg_ref[0, rows, :] = _rms(ysg_scr[rows, :], sgog_ref[...]).astype(BF16)

    csum = c3 + pltpu.roll(c3, LANES - 24, axis=1) + pltpu.roll(c3, LANES - 48, axis=1)
    chi, cmid, clo = _split3(csum * LOG2E)
    cp = jnp.where(lane < 8, chi, jnp.where(lane < 16, cmid, clo))
    q_bias_lo = jnp.where(lane < 24, cp, jnp.where(lane < 48, 1.0, 0.0))
    k_bias_lo = jnp.where(lane < 24, 1.0, jnp.where(lane < 48, -pltpu.roll(cp, 24, axis=1), 0.0))
    q_bias_hi = pltpu.roll(q_bias_lo, HEAD_DIM, axis=1)
    k_bias_hi = pltpu.roll(k_bias_lo, HEAD_DIM, axis=1)

    for h in range(N_HEADS):
        blk = h // 2
        even = (h % 2 == 0)
        in_head = (lane < HEAD_DIM) if even else (lane >= HEAD_DIM)
        own_bias = ((lane & 7) == h) & ((lane & (HEAD_DIM - 1)) < 48)
        zq = zqkv[:, blk * LANES:(blk + 1) * LANES] * (HEAD_DIM ** -0.5 * LOG2E)
        zk = zqkv[:, ATT_WIDTH + blk * LANES:ATT_WIDTH + (blk + 1) * LANES]
        zv = zqkv[:, 2 * ATT_WIDTH + blk * LANES:2 * ATT_WIDTH + (blk + 1) * LANES]
        v_extra = jnp.where(lane == (HEAD_DIM if even else 0), 1.0, 0.0)
        qp_ref[0, h, rows, :] = jnp.where(in_head, zq, q_bias_hi if even else q_bias_lo).astype(BF16)
        kp_ref[0, h, rows, :] = jnp.where(
            in_head, zk, jnp.where(own_bias, k_bias_hi if even else k_bias_lo, 0.0)).astype(BF16)
        vp_ref[0, h, rows, :] = jnp.where(in_head, zv, v_extra).astype(BF16)


def _attn_kernel(q_ref, k_ref, v_ref, o_ref, acc_ref, m_ref, s_scr, mc_scr):
    n_full = pl.program_id(2) * STAIRS

    def scores(b, slot, stair=None):
        row0 = 0 if stair is None else stair * T_K
        start = pl.multiple_of(b * T_K, T_K)
        for hh in range(2):
            q = q_ref[0, hh, row0:, :]
            kb = k_ref[0, hh, pl.ds(start, T_K), :]
            s = lax.dot_general(q, kb, (((1,), (1,)), ((), ())), preferred_element_type=F32)
            parts = [(row0, T_Q, s)]
            if stair is not None:
                r = lax.broadcasted_iota(jnp.int32, (T_K, T_K), 0)
                c = lax.broadcasted_iota(jnp.int32, (T_K, T_K), 1)
                parts = [(row0, row0 + T_K, jnp.where(c <= r, s[:T_K], NEG_BIG))]
                if row0 + T_K < T_Q:
                    parts.append((row0 + T_K, T_Q, s[T_K:]))
            for lo, hi, sp in parts:
                s_scr[slot, hh, lo:hi, :] = sp
                mc_scr[slot, hh, lo:hi, :] = jnp.broadcast_to(
                    jnp.max(sp, axis=-1, keepdims=True), (hi - lo, LANES))

    def update(b, slot, row0=0, first=False):
        start = pl.multiple_of(b * T_K, T_K)
        for hh in range(2):
            vb = v_ref[0, hh, pl.ds(start, T_K), :]
            m_blk = mc_scr[slot, hh, row0:, :]
            if first:
                m_new = m_blk
            else:
                m_old = m_ref[hh, row0:, :]
                m_new = jnp.maximum(m_old, m_blk)
            m_wide = jnp.concatenate([m_new] * (T_K // LANES), axis=1)
            p = jnp.exp2(s_scr[slot, hh, row0:, :] - m_wide).astype(BF16)
            pv = jnp.dot(p, vb, preferred_element_type=F32)
            if first:
                acc_ref[hh] = pv
            else:
                acc_ref[hh, row0:, :] = jnp.exp2(m_old - m_new) * acc_ref[hh, row0:, :] + pv
            m_ref[hh, row0:, :] = m_new

    def staircase(first_blk, is_tile_start):
        for k in range(STAIRS - 1):
            scores(first_blk + k + 1, (k + 1) % 2, stair=k + 1)
            update(first_blk + k, k % 2, row0=k * T_K, first=is_tile_start and k == 0)
        update(first_blk + STAIRS - 1, (STAIRS - 1) % 2, row0=(STAIRS - 1) * T_K)

    @pl.when(n_full == 0)
    def _():
        scores(0, 0, stair=0)
        staircase(0, True)

    @pl.when(n_full > 0)
    def _():
        scores(0, 0)
        scores(1, 1)
        update(0, 0, first=True)

        def body(i, carry):
            b = 1 + ATT_UNROLL * i
            scores(b + 1, 0)
            update(b, 1)
            scores(b + 2, 1)
            update(b + 1, 0)
            return carry

        lax.fori_loop(0, (n_full - 2) // ATT_UNROLL, body, 0)
        scores(n_full, 0, stair=0)
        update(n_full - 1, 1)
        staircase(n_full, False)

    a0 = acc_ref[0]
    a1 = acc_ref[1]
    lane = lax.broadcasted_iota(jnp.int32, (T_Q, LANES), 1)
    o_ref[0] = jnp.where(lane < HEAD_DIM, a0 / a0[:, HEAD_DIM:HEAD_DIM + 1], a1 / a1[:, 0:1])


def _mlp_kernel(x_ref, ya_ref, ys_ref, p_ref, woa_ref, wos_ref, w1_ref, w2_ref, wg_ref, wpe_ref,
                gatt_ref, gpm_ref, gpf_ref, gqf_ref, bg_ref, o_ref):
    yan = _rms(ya_ref[...], gatt_ref[...]).astype(BF16)
    y = (jnp.dot(yan, woa_ref[...], preferred_element_type=F32)
         + jnp.dot(ys_ref[...], wos_ref[...], preferred_element_type=F32))
    h1 = x_ref[...] + _rms(y, gpm_ref[...])
    cn = _rms(h1, gpf_ref[...]).astype(BF16)
    ff = None
    for c in range(D_FF // FF_CHUNK):
        cols = slice(c * FF_CHUNK, (c + 1) * FF_CHUNK)
        hid = jnp.dot(cn, w1_ref[:, cols], preferred_element_type=F32)
        hid = jnp.square(jnp.maximum(hid, 0.0)).astype(BF16)
        part = jnp.dot(hid, w2_ref[cols, :], preferred_element_type=F32)
        ff = part if ff is None else ff + part
    h2 = h1 + _rms(ff, gqf_ref[...])
    gate = jax.nn.sigmoid(jnp.dot(h2.astype(BF16), wg_ref[...], preferred_element_type=F32) + bg_ref[...])
    pe = jnp.dot(p_ref[...].astype(BF16), wpe_ref[...], preferred_element_type=F32)
    o_ref[...] = h2 + gate * pe


def _layer(h, p, w_in, f_bias, sg_ln_g, sg_ln_b, sg_w, sg_b, att_out_g, sg_out_g, w_out,
           pre_mix_g, post_mix_g, pre_ffn_g, post_ffn_g, w_ff1, w_ff2, ple_w, ple_gate_w, ple_gate_b):
    B, S, D = h.shape
    R = B * S
    row = lambda a: a.reshape(1, -1).astype(F32)

    wqkv = w_in[:, :3 * ATT_WIDTH].astype(BF16)
    wf8 = w_in[:, 3 * ATT_WIDTH:3 * ATT_WIDTH + N_HEADS]
    n_rep = 9
    wf = jnp.pad(jnp.tile(wf8, (1, n_rep)), ((0, 0), (0, LANES - n_rep * N_HEADS))).astype(BF16)
    fb = jnp.pad(jnp.tile(f_bias, n_rep), (0, LANES - n_rep * N_HEADS)).reshape(1, LANES).astype(F32)
    tril = jnp.tril(jnp.ones((T_SUB, T_SUB), BF16))
    wuv = w_in[:, 3 * ATT_WIDTH + N_HEADS:].astype(BF16)
    sgb = jnp.repeat(jnp.transpose(sg_b), HEAD_DIM, axis=1).astype(F32)

    const2 = lambda shape: pl.BlockSpec(shape, lambda b, t: (0,) * len(shape))
    qp, kp, vp, ysg = pl.pallas_call(
        _inproj_kernel,
        grid=(B, S // T_IN),
        in_specs=[
            pl.BlockSpec((1, T_IN, D), lambda b, t: (b, t, 0)),
            const2((1, D)),
            const2((D, 3 * ATT_WIDTH)),
            const2((D, LANES)),
            const2((1, LANES)),
            const2((T_SUB, T_SUB)),
            const2((D, 2 * SG_WIDTH)),
            const2((1, SG_WIDTH)),
            const2((1, SG_WIDTH)),
            const2((N_SG_GROUPS, CHUNK, CHUNK)),
            const2((CHUNK, SG_WIDTH)),
            const2((1, SG_WIDTH)),
        ],
        out_specs=[
            pl.BlockSpec((1, N_HEADS, T_IN, LANES), lambda b, t: (b, 0, t, 0)),
            pl.BlockSpec((1, N_HEADS, T_IN, LANES), lambda b, t: (b, 0, t, 0)),
            pl.BlockSpec((1, N_HEADS, T_IN, LANES), lambda b, t: (b, 0, t, 0)),
            pl.BlockSpec((1, T_IN, SG_WIDTH), lambda b, t: (b, t, 0)),
        ],
        out_shape=[
            jax.ShapeDtypeStruct((B, N_HEADS, S, LANES), BF16),
            jax.ShapeDtypeStruct((B, N_HEADS, S, LANES), BF16),
            jax.ShapeDtypeStruct((B, N_HEADS, S, LANES), BF16),
            jax.ShapeDtypeStruct((B, S, SG_WIDTH), BF16),
        ],
        scratch_shapes=[pltpu.VMEM((1, LANES), F32), pltpu.VMEM((T_IN, SG_WIDTH), F32)],
        compiler_params=pltpu.CompilerParams(
            dimension_semantics=("arbitrary", "arbitrary"), vmem_limit_bytes=VMEM_LIMIT),
        name="inproj",
    )(h, row(pre_mix_g), wqkv, wf, fb, tril, wuv, row(sg_ln_g), row(sg_ln_b), sg_w.astype(F32), sgb, row(sg_out_g))

    yatt = pl.pallas_call(
        _attn_kernel,
        grid=(B, HEAD_PAIRS, S // T_Q),
        in_specs=[
            pl.BlockSpec((1, 2, T_Q, LANES), lambda b, hp, i: (b, hp, i, 0)),
            pl.BlockSpec((1, 2, S, LANES), lambda b, hp, i: (b, hp, 0, 0)),
            pl.BlockSpec((1, 2, S, LANES), lambda b, hp, i: (b, hp, 0, 0)),
        ],
        out_specs=pl.BlockSpec((1, T_Q, LANES), lambda b, hp, i: (b, i, hp)),
        out_shape=jax.ShapeDtypeStruct((B, S, ATT_WIDTH), F32),
        scratch_shapes=[pltpu.VMEM((2, T_Q, LANES), F32), pltpu.VMEM((2, T_Q, LANES), F32),
                        pltpu.VMEM((2, 2, T_Q, T_K), F32), pltpu.VMEM((2, 2, T_Q, LANES), F32)],
        compiler_params=pltpu.CompilerParams(
            dimension_semantics=("arbitrary", "arbitrary", "arbitrary"), vmem_limit_bytes=VMEM_LIMIT),
        name="fox_attn",
    )(qp, kp, vp)

    rows = lambda w: pl.BlockSpec((T_MLP, w), lambda i: (i, 0))
    const = lambda shape: pl.BlockSpec(shape, lambda i: (0,) * len(shape), pipeline_mode=pl.Buffered(1))
    out = pl.pallas_call(
        _mlp_kernel,
        grid=(R // T_MLP,),
        in_specs=[
            rows(D), rows(ATT_WIDTH), rows(SG_WIDTH), rows(PLE_DIM),
            const((ATT_WIDTH, D)), const((SG_WIDTH, D)),
            const((D, D_FF)), const((D_FF, D)),
            const((D, D)), const((PLE_DIM, D)),
            const((1, ATT_WIDTH)), const((1, D)), const((1, D)), const((1, D)), const((1, D)),
        ],
        out_specs=rows(D),
        out_shape=jax.ShapeDtypeStruct((R, D), F32),
        compiler_params=pltpu.CompilerParams(
            dimension_semantics=("arbitrary",), vmem_limit_bytes=VMEM_LIMIT),
        name="mlp",
    )(h.reshape(R, D), yatt.reshape(R, ATT_WIDTH), ysg.reshape(R, SG_WIDTH), p.reshape(R, PLE_DIM),
      w_out[:ATT_WIDTH].astype(BF16), w_out[ATT_WIDTH:].astype(BF16),
      w_ff1.astype(BF16), w_ff2.astype(BF16), ple_gate_w.astype(BF16), ple_w.astype(BF16),
      row(att_out_g), row(post_mix_g), row(pre_ffn_g), row(post_ffn_g), row(ple_gate_b))
    return out.reshape(B, S, D)


def kernel(x, p, w_in, f_bias, sg_ln_g, sg_ln_b, sg_w, sg_b, att_out_g, sg_out_g, w_out,
           pre_mix_g, post_mix_g, pre_ffn_g, post_ffn_g, w_ff1, w_ff2, ple_w, ple_gate_w, ple_gate_b):
    h = x
    for i in range(p.shape[0]):
        h = _layer(h, p[i], w_in[i], f_bias[i], sg_ln_g[i], sg_ln_b[i], sg_w[i], sg_b[i],
                   att_out_g[i], sg_out_g[i], w_out[i], pre_mix_g[i], post_mix_g[i],
                   pre_ffn_g[i], post_ffn_g[i], w_ff1[i], w_ff2[i], ple_w[i], ple_gate_w[i],
                   ple_gate_b[i])
    return h
```

```python
import functools

import jax
import jax.numpy as jnp
from jax import lax
from jax.experimental import pallas as pl
from jax.experimental.pallas import tpu as pltpu

D_MODEL = 1024
PLE_DIM = 256
HEAD_DIM = 64
ATT_WIDTH = 512
N_HEADS = 8
SG_WIDTH = 512
N_SG_GROUPS = 8
CHUNK = 128
D_FF = 4096
EPS = 1e-6

LANES = 128
HEAD_PAIRS = N_HEADS // 2
NEG_BIG = -1e30
LOG2E = 1.4426950408889634

T_IN = 512
T_SUB = 512
T_Q = 2048
T_K = 512
STAIRS = T_Q // T_K
ATT_UNROLL = 4
ATT_TAIL = (STAIRS - 2) % ATT_UNROLL
assert ATT_UNROLL % 2 == 0 and STAIRS % ATT_UNROLL == 0 and ATT_TAIL % 2 == 0
T_MLP = 512
FF_CHUNK = 1024

VMEM_LIMIT = 56 * 1024 * 1024
VMEM_LIMIT_ATTN = 60 * 1024 * 1024

F32 = jnp.float32
BF16 = jnp.bfloat16


def _rms(x, g):
    return x * lax.rsqrt(jnp.mean(x * x, axis=-1, keepdims=True) + EPS) * g


def _gelu_tanh(x):
    return x * (0.5 * (1.0 + jnp.tanh(0.7978845608028654 * (x + 0.044715 * (x * x * x)))))


def _split3(x):
    hi = x.astype(BF16).astype(F32)
    r = x - hi
    mid = r.astype(BF16).astype(F32)
    lo = (r - mid).astype(BF16).astype(F32)
    return hi, mid, lo


def _inproj_kernel(x_ref, g_ref, wqkv_ref, wf_ref, fb_ref, tril_ref, wuv_ref, lng_ref, lnb_ref,
                   sgw_ref, sgb_ref, sgog_ref,
                   qp_ref, kp_ref, vp_ref, ysg_ref,
                   carry_ref, ysg_scr):
    t = pl.program_id(1)

    @pl.when(t == 0)
    def _():
        carry_ref[...] = jnp.zeros_like(carry_ref)

    for r0 in range(0, T_IN, T_SUB):
        _inproj_rows(r0, x_ref, g_ref, wqkv_ref, wf_ref, fb_ref, tril_ref, wuv_ref, lng_ref, lnb_ref,
                     sgw_ref, sgb_ref, sgog_ref, qp_ref, kp_ref, vp_ref, ysg_ref, carry_ref, ysg_scr)


def _inproj_rows(r0, x_ref, g_ref, wqkv_ref, wf_ref, fb_ref, tril_ref, wuv_ref, lng_ref, lnb_ref,
                 sgw_ref, sgb_ref, sgog_ref, qp_ref, kp_ref, vp_ref, ysg_ref, carry_ref, ysg_scr):
    rows = slice(r0, r0 + T_SUB)
    x = x_ref[0, rows, :]
    ab = _rms(x, g_ref[...]).astype(BF16)
    lane = lax.broadcasted_iota(jnp.int32, (T_SUB, LANES), 1)

    zuv = jnp.dot(ab, wuv_ref[...], preferred_element_type=F32)
    f = jnp.dot(ab, wf_ref[...], preferred_element_type=F32) + fb_ref[...]
    zqkv = jnp.dot(ab, wqkv_ref[...], preferred_element_type=F32)

    logf = jnp.minimum(f, 0.0) - jnp.log1p(jnp.exp(-jnp.abs(f)))
    hi, mid, lo = _split3(logf)
    pieces = jnp.where(lane < 24, hi, jnp.where(lane < 48, mid, lo)).astype(BF16)
    c3 = jnp.dot(tril_ref[...], pieces, preferred_element_type=F32) + carry_ref[...]
    carry_ref[...] = c3[T_SUB - 1:T_SUB, :]

    u = _gelu_tanh(zuv[:, :SG_WIDTH])
    v = _gelu_tanh(zuv[:, SG_WIDTH:])
    vc = v - jnp.mean(v, axis=-1, keepdims=True)
    vln = vc * lax.rsqrt(jnp.mean(vc * vc, axis=-1, keepdims=True) + EPS) * lng_ref[...] + lnb_ref[...]
    vb = vln.astype(BF16)
    n_chunks = T_SUB // CHUNK
    ci = lax.broadcasted_iota(jnp.int32, (CHUNK, CHUNK), 0)
    cj = lax.broadcasted_iota(jnp.int32, (CHUNK, CHUNK), 1)
    lane_c = lax.broadcasted_iota(jnp.int32, (CHUNK, LANES), 1)
    for pb in range(SG_WIDTH // LANES):
        vcat = jnp.concatenate(
            [vb[c * CHUNK:(c + 1) * CHUNK, pb * LANES:(pb + 1) * LANES] for c in range(n_chunks)], axis=1)
        res = []
        for g in (2 * pb, 2 * pb + 1):
            w = jnp.where(cj <= ci, sgw_ref[g], 0.0).astype(BF16)
            res.append(jnp.dot(w, vcat, preferred_element_type=F32))
        for c in range(n_chunks):
            mixed = jnp.where(lane_c < HEAD_DIM,
                              res[0][:, c * LANES:(c + 1) * LANES],
                              res[1][:, c * LANES:(c + 1) * LANES])
            mixed = mixed + sgb_ref[:, pb * LANES:(pb + 1) * LANES]
            ysg_scr[r0 + c * CHUNK:r0 + (c + 1) * CHUNK, pb * LANES:(pb + 1) * LANES] = (
                u[c * CHUNK:(c + 1) * CHUNK, pb * LANES:(pb + 1) * LANES] * mixed)
    ysg_ref[0, rows, :] = _rms(ysg_scr[rows, :], sgog_ref[...]).astype(BF16)

    csum = c3 + pltpu.roll(c3, LANES - 24, axis=1) + pltpu.roll(c3, LANES - 48, axis=1)
    chi, cmid, clo = _split3(csum * LOG2E)
    cp = jnp.where(lane < 8, chi, jnp.where(lane < 16, cmid, clo))
    q_bias_lo = jnp.where(lane < 24, cp, jnp.where(lane < 48, 1.0, 0.0))
    k_bias_lo = jnp.where(lane < 24, 1.0, jnp.where(lane < 48, -pltpu.roll(cp, 24, axis=1), 0.0))
    q_bias_hi = pltpu.roll(q_bias_lo, HEAD_DIM, axis=1)
    k_bias_hi = pltpu.roll(k_bias_lo, HEAD_DIM, axis=1)

    for h in range(N_HEADS):
        blk = h // 2
        even = (h % 2 == 0)
        in_head = (lane < HEAD_DIM) if even else (lane >= HEAD_DIM)
        own_bias = ((lane & 7) == h) & ((lane & (HEAD_DIM - 1)) < 48)
        zq = zqkv[:, blk * LANES:(blk + 1) * LANES] * (HEAD_DIM ** -0.5 * LOG2E)
        zk = zqkv[:, ATT_WIDTH + blk * LANES:ATT_WIDTH + (blk + 1) * LANES]
        zv = zqkv[:, 2 * ATT_WIDTH + blk * LANES:2 * ATT_WIDTH + (blk + 1) * LANES]
        v_extra = jnp.where(lane == (HEAD_DIM if even else 0), 1.0, 0.0)
        qp_ref[0, h, rows, :] = jnp.where(in_head, zq, q_bias_hi if even else q_bias_lo).astype(BF16)
        kp_ref[0, h, rows, :] = jnp.where(
            in_head, zk, jnp.where(own_bias, k_bias_hi if even else k_bias_lo, 0.0)).astype(BF16)
        vp_ref[0, h, rows, :] = jnp.where(in_head, zv, v_extra).astype(BF16)


def _attn_kernel(q_ref, k_ref, v_ref, o_ref, acc_ref, m_ref, s_scr, mc_scr):
    n_full = pl.program_id(2) * STAIRS

    def scores(b, slot, stair=None):
        row0 = 0 if stair is None else stair * T_K
        start = pl.multiple_of(b * T_K, T_K)
        for hh in range(2):
            q = q_ref[0, hh, row0:, :]
            kb = k_ref[0, hh, pl.ds(start, T_K), :]
            s = lax.dot_general(q, kb, (((1,), (1,)), ((), ())), preferred_element_type=F32)
            parts = [(row0, T_Q, s)]
            if stair is not None:
                r = lax.broadcasted_iota(jnp.int32, (T_K, T_K), 0)
                c = lax.broadcasted_iota(jnp.int32, (T_K, T_K), 1)
                parts = [(row0, row0 + T_K, jnp.where(c <= r, s[:T_K], NEG_BIG))]
                if row0 + T_K < T_Q:
                    parts.append((row0 + T_K, T_Q, s[T_K:]))
            for lo, hi, sp in parts:
                s_scr[slot, hh, lo:hi, :] = sp
                mc_scr[slot, hh, lo:hi, :] = jnp.broadcast_to(
                    jnp.max(sp, axis=-1, keepdims=True), (hi - lo, LANES))

    def update(b, slot, row0=0, first=False):
        start = pl.multiple_of(b * T_K, T_K)
        for hh in range(2):
            vb = v_ref[0, hh, pl.ds(start, T_K), :]
            m_blk = mc_scr[slot, hh, row0:, :]
            if first:
                m_new = m_blk
            else:
                m_old = m_ref[hh, row0:, :]
                m_new = jnp.maximum(m_old, m_blk)
            m_wide = jnp.concatenate([m_new] * (T_K // LANES), axis=1)
            p = jnp.exp2(s_scr[slot, hh, row0:, :] - m_wide).astype(BF16)
            pv = jnp.dot(p, vb, preferred_element_type=F32)
            if first:
                acc_ref[hh] = pv
            else:
                acc_ref[hh, row0:, :] = jnp.exp2(m_old - m_new) * acc_ref[hh, row0:, :] + pv
            m_ref[hh, row0:, :] = m_new

    def staircase(first_blk, is_tile_start):
        for k in range(STAIRS - 1):
            scores(first_blk + k + 1, (k + 1) % 2, stair=k + 1)
            update(first_blk + k, k % 2, row0=k * T_K, first=is_tile_start and k == 0)
        update(first_blk + STAIRS - 1, (STAIRS - 1) % 2, row0=(STAIRS - 1) * T_K)

    @pl.when(n_full == 0)
    def _():
        scores(0, 0, stair=0)
        staircase(0, True)

    @pl.when(n_full > 0)
    def _():
        scores(0, 0)
        scores(1, 1)
        update(0, 0, first=True)

        def body(i, carry):
            b = 1 + ATT_UNROLL * i
            for u in range(ATT_UNROLL):
                scores(b + u + 1, u % 2)
                update(b + u, (u + 1) % 2)
            return carry

        trips = (n_full - 2) // ATT_UNROLL
        lax.fori_loop(0, trips, body, 0)
        b = n_full - 1 - ATT_TAIL
        for u in range(ATT_TAIL):
            scores(b + u + 1, u % 2)
            update(b + u, (u + 1) % 2)
        scores(n_full, 0, stair=0)
        update(n_full - 1, 1)
        staircase(n_full, False)

    a0 = acc_ref[0]
    a1 = acc_ref[1]
    lane = lax.broadcasted_iota(jnp.int32, (T_Q, LANES), 1)
    o_ref[0] = jnp.where(lane < HEAD_DIM, a0 / a0[:, HEAD_DIM:HEAD_DIM + 1], a1 / a1[:, 0:1])


def _mlp_kernel(x_ref, ya_ref, ys_ref, p_ref, woa_ref, wos_ref, w1_ref, w2_ref, wg_ref, wpe_ref,
                gatt_ref, gpm_ref, gpf_ref, gqf_ref, bg_ref, o_ref):
    yan = _rms(ya_ref[...], gatt_ref[...]).astype(BF16)
    y = (jnp.dot(yan, woa_ref[...], preferred_element_type=F32)
         + jnp.dot(ys_ref[...], wos_ref[...], preferred_element_type=F32))
    h1 = x_ref[...] + _rms(y, gpm_ref[...])
    cn = _rms(h1, gpf_ref[...]).astype(BF16)
    ff = None
    for c in range(D_FF // FF_CHUNK):
        cols = slice(c * FF_CHUNK, (c + 1) * FF_CHUNK)
        hid = jnp.dot(cn, w1_ref[:, cols], preferred_element_type=F32)
        hid = jnp.square(jnp.maximum(hid, 0.0)).astype(BF16)
        part = jnp.dot(hid, w2_ref[cols, :], preferred_element_type=F32)
        ff = part if ff is None else ff + part
    h2 = h1 + _rms(ff, gqf_ref[...])
    gate = jax.nn.sigmoid(jnp.dot(h2.astype(BF16), wg_ref[...], preferred_element_type=F32) + bg_ref[...])
    pe = jnp.dot(p_ref[...].astype(BF16), wpe_ref[...], preferred_element_type=F32)
    o_ref[...] = h2 + gate * pe


def _layer(h, p, w_in, f_bias, sg_ln_g, sg_ln_b, sg_w, sg_b, att_out_g, sg_out_g, w_out,
           pre_mix_g, post_mix_g, pre_ffn_g, post_ffn_g, w_ff1, w_ff2, ple_w, ple_gate_w, ple_gate_b):
    B, S, D = h.shape
    R = B * S
    row = lambda a: a.reshape(1, -1).astype(F32)

    wqkv = w_in[:, :3 * ATT_WIDTH].astype(BF16)
    wf8 = w_in[:, 3 * ATT_WIDTH:3 * ATT_WIDTH + N_HEADS]
    n_rep = 9
    wf = jnp.pad(jnp.tile(wf8, (1, n_rep)), ((0, 0), (0, LANES - n_rep * N_HEADS))).astype(BF16)
    fb = jnp.pad(jnp.tile(f_bias, n_rep), (0, LANES - n_rep * N_HEADS)).reshape(1, LANES).astype(F32)
    tril = jnp.tril(jnp.ones((T_SUB, T_SUB), BF16))
    wuv = w_in[:, 3 * ATT_WIDTH + N_HEADS:].astype(BF16)
    sgb = jnp.repeat(jnp.transpose(sg_b), HEAD_DIM, axis=1).astype(F32)

    const2 = lambda shape: pl.BlockSpec(shape, lambda b, t: (0,) * len(shape))
    qp, kp, vp, ysg = pl.pallas_call(
        _inproj_kernel,
        grid=(B, S // T_IN),
        in_specs=[
            pl.BlockSpec((1, T_IN, D), lambda b, t: (b, t, 0)),
            const2((1, D)),
            const2((D, 3 * ATT_WIDTH)),
            const2((D, LANES)),
            const2((1, LANES)),
            const2((T_SUB, T_SUB)),
            const2((D, 2 * SG_WIDTH)),
            const2((1, SG_WIDTH)),
            const2((1, SG_WIDTH)),
            const2((N_SG_GROUPS, CHUNK, CHUNK)),
            const2((CHUNK, SG_WIDTH)),
            const2((1, SG_WIDTH)),
        ],
        out_specs=[
            pl.BlockSpec((1, N_HEADS, T_IN, LANES), lambda b, t: (b, 0, t, 0)),
            pl.BlockSpec((1, N_HEADS, T_IN, LANES), lambda b, t: (b, 0, t, 0)),
            pl.BlockSpec((1, N_HEADS, T_IN, LANES), lambda b, t: (b, 0, t, 0)),
            pl.BlockSpec((1, T_IN, SG_WIDTH), lambda b, t: (b, t, 0)),
        ],
        out_shape=[
            jax.ShapeDtypeStruct((B, N_HEADS, S, LANES), BF16),
            jax.ShapeDtypeStruct((B, N_HEADS, S, LANES), BF16),
            jax.ShapeDtypeStruct((B, N_HEADS, S, LANES), BF16),
            jax.ShapeDtypeStruct((B, S, SG_WIDTH), BF16),
        ],
        scratch_shapes=[pltpu.VMEM((1, LANES), F32), pltpu.VMEM((T_IN, SG_WIDTH), F32)],
        compiler_params=pltpu.CompilerParams(
            dimension_semantics=("arbitrary", "arbitrary"), vmem_limit_bytes=VMEM_LIMIT),
        name="inproj",
    )(h, row(pre_mix_g), wqkv, wf, fb, tril, wuv, row(sg_ln_g), row(sg_ln_b), sg_w.astype(F32), sgb, row(sg_out_g))

    yatt = pl.pallas_call(
        _attn_kernel,
        grid=(B, HEAD_PAIRS, S // T_Q),
        in_specs=[
            pl.BlockSpec((1, 2, T_Q, LANES), lambda b, hp, i: (b, hp, i, 0)),
            pl.BlockSpec((1, 2, S, LANES), lambda b, hp, i: (b, hp, 0, 0)),
            pl.BlockSpec((1, 2, S, LANES), lambda b, hp, i: (b, hp, 0, 0)),
        ],
        out_specs=pl.BlockSpec((1, T_Q, LANES), lambda b, hp, i: (b, i, hp)),
        out_shape=jax.ShapeDtypeStruct((B, S, ATT_WIDTH), F32),
        scratch_shapes=[pltpu.VMEM((2, T_Q, LANES), F32), pltpu.VMEM((2, T_Q, LANES), F32),
                        pltpu.VMEM((2, 2, T_Q, T_K), F32), pltpu.VMEM((2, 2, T_Q, LANES), F32)],
        compiler_params=pltpu.CompilerParams(
            dimension_semantics=("arbitrary", "arbitrary", "arbitrary"), vmem_limit_bytes=VMEM_LIMIT_ATTN),
        name="fox_attn",
    )(qp, kp, vp)

    rows = lambda w: pl.BlockSpec((T_MLP, w), lambda i: (i, 0))
    const = lambda shape: pl.BlockSpec(shape, lambda i: (0,) * len(shape), pipeline_mode=pl.Buffered(1))
    out = pl.pallas_call(
        _mlp_kernel,
        grid=(R // T_MLP,),
        in_specs=[
            rows(D), rows(ATT_WIDTH), rows(SG_WIDTH), rows(PLE_DIM),
            const((ATT_WIDTH, D)), const((SG_WIDTH, D)),
            const((D, D_FF)), const((D_FF, D)),
            const((D, D)), const((PLE_DIM, D)),
            const((1, ATT_WIDTH)), const((1, D)), const((1, D)), const((1, D)), const((1, D)),
        ],
        out_specs=rows(D),
        out_shape=jax.ShapeDtypeStruct((R, D), F32),
        compiler_params=pltpu.CompilerParams(
            dimension_semantics=("arbitrary",), vmem_limit_bytes=VMEM_LIMIT),
        name="mlp",
    )(h.reshape(R, D), yatt.reshape(R, ATT_WIDTH), ysg.reshape(R, SG_WIDTH), p.reshape(R, PLE_DIM),
      w_out[:ATT_WIDTH].astype(BF16), w_out[ATT_WIDTH:].astype(BF16),
      w_ff1.astype(BF16), w_ff2.astype(BF16), ple_gate_w.astype(BF16), ple_w.astype(BF16),
      row(att_out_g), row(post_mix_g), row(pre_ffn_g), row(post_ffn_g), row(ple_gate_b))
    return out.reshape(B, S, D)


def kernel(x, p, w_in, f_bias, sg_ln_g, sg_ln_b, sg_w, sg_b, att_out_g, sg_out_g, w_out,
           pre_mix_g, post_mix_g, pre_ffn_g, post_ffn_g, w_ff1, w_ff2, ple_w, ple_gate_w, ple_gate_b):
    h = x
    for i in range(p.shape[0]):
        h = _layer(h, p[i], w_in[i], f_bias[i], sg_ln_g[i], sg_ln_b[i], sg_w[i], sg_b[i],
                   att_out_g[i], sg_out_g[i], w_out[i], pre_mix_g[i], post_mix_g[i],
                   pre_ffn_g[i], post_ffn_g[i], w_ff1[i], w_ff2[i], ple_w[i], ple_gate_w[i],
                   ple_gate_b[i])
    return h
```

```python
import functools

import jax
import jax.numpy as jnp
from jax import lax
from jax.experimental import pallas as pl
from jax.experimental.pallas import tpu as pltpu

D_MODEL = 1024
PLE_DIM = 256
HEAD_DIM = 64
ATT_WIDTH = 512
N_HEADS = 8
SG_WIDTH = 512
N_SG_GROUPS = 8
CHUNK = 128
D_FF = 4096
EPS = 1e-6

LANES = 128
HEAD_PAIRS = N_HEADS // 2
NEG_BIG = -1e30
LOG2E = 1.4426950408889634

T_IN = 512
T_SUB = 512
T_Q = 2048
T_K = 512
STAIRS = T_Q // T_K
ATT_UNROLL = 2
assert ATT_UNROLL == 2 and STAIRS % 2 == 0
T_MLP = 512
T_MLP_SUB = 256
FF_CHUNK = 1024

VMEM_LIMIT = 56 * 1024 * 1024

F32 = jnp.float32
BF16 = jnp.bfloat16


def _rms(x, g):
    return x * lax.rsqrt(jnp.mean(x * x, axis=-1, keepdims=True) + EPS) * g


def _gelu_tanh(x):
    return x * (0.5 * (1.0 + jnp.tanh(0.7978845608028654 * (x + 0.044715 * (x * x * x)))))


def _split3(x):
    hi = x.astype(BF16).astype(F32)
    r = x - hi
    mid = r.astype(BF16).astype(F32)
    lo = (r - mid).astype(BF16).astype(F32)
    return hi, mid, lo


def _inproj_kernel(x_ref, g_ref, wqkv_ref, wf_ref, fb_ref, tril_ref, wuv_ref, lng_ref, lnb_ref,
                   sgw_ref, sgb_ref, sgog_ref,
                   qp_ref, kp_ref, vp_ref, ysg_ref,
                   carry_ref, ysg_scr):
    t = pl.program_id(1)

    @pl.when(t == 0)
    def _():
        carry_ref[...] = jnp.zeros_like(carry_ref)

    for r0 in range(0, T_IN, T_SUB):
        _inproj_rows(r0, x_ref, g_ref, wqkv_ref, wf_ref, fb_ref, tril_ref, wuv_ref, lng_ref, lnb_ref,
                     sgw_ref, sgb_ref, sgog_ref, qp_ref, kp_ref, vp_ref, ysg_ref, carry_ref, ysg_scr)


def _inproj_rows(r0, x_ref, g_ref, wqkv_ref, wf_ref, fb_ref, tril_ref, wuv_ref, lng_ref, lnb_ref,
                 sgw_ref, sgb_ref, sgog_ref, qp_ref, kp_ref, vp_ref, ysg_ref, carry_ref, ysg_scr):
    rows = slice(r0, r0 + T_SUB)
    x = x_ref[0, rows, :]
    ab = _rms(x, g_ref[...]).astype(BF16)
    lane = lax.broadcasted_iota(jnp.int32, (T_SUB, LANES), 1)

    zuv = jnp.dot(ab, wuv_ref[...], preferred_element_type=F32)
    f = jnp.dot(ab, wf_ref[...], preferred_element_type=F32) + fb_ref[...]
    zqkv = jnp.dot(ab, wqkv_ref[...], preferred_element_type=F32)

    logf = jnp.minimum(f, 0.0) - jnp.log1p(jnp.exp(-jnp.abs(f)))
    hi, mid, lo = _split3(logf)
    pieces = jnp.where(lane < 24, hi, jnp.where(lane < 48, mid, lo)).astype(BF16)
    c3 = jnp.dot(tril_ref[...], pieces, preferred_element_type=F32) + carry_ref[...]
    carry_ref[...] = c3[T_SUB - 1:T_SUB, :]

    u = _gelu_tanh(zuv[:, :SG_WIDTH])
    v = _gelu_tanh(zuv[:, SG_WIDTH:])
    vc = v - jnp.mean(v, axis=-1, keepdims=True)
    vln = vc * lax.rsqrt(jnp.mean(vc * vc, axis=-1, keepdims=True) + EPS) * lng_ref[...] + lnb_ref[...]
    vb = vln.astype(BF16)
    n_chunks = T_SUB // CHUNK
    ci = lax.broadcasted_iota(jnp.int32, (CHUNK, CHUNK), 0)
    cj = lax.broadcasted_iota(jnp.int32, (CHUNK, CHUNK), 1)
    lane_c = lax.broadcasted_iota(jnp.int32, (CHUNK, LANES), 1)
    for pb in range(SG_WIDTH // LANES):
        vcat = jnp.concatenate(
            [vb[c * CHUNK:(c + 1) * CHUNK, pb * LANES:(pb + 1) * LANES] for c in range(n_chunks)], axis=1)
        res = []
        for g in (2 * pb, 2 * pb + 1):
            w = jnp.where(cj <= ci, sgw_ref[g], 0.0).astype(BF16)
            res.append(jnp.dot(w, vcat, preferred_element_type=F32))
        for c in range(n_chunks):
            mixed = jnp.where(lane_c < HEAD_DIM,
                              res[0][:, c * LANES:(c + 1) * LANES],
                              res[1][:, c * LANES:(c + 1) * LANES])
            mixed = mixed + sgb_ref[:, pb * LANES:(pb + 1) * LANES]
            ysg_scr[r0 + c * CHUNK:r0 + (c + 1) * CHUNK, pb * LANES:(pb + 1) * LANES] = (
                u[c * CHUNK:(c + 1) * CHUNK, pb * LANES:(pb + 1) * LANES] * mixed)
    ysg_ref[0, rows, :] = _rms(ysg_scr[rows, :], sgog_ref[...]).astype(BF16)

    csum = c3 + pltpu.roll(c3, LANES - 24, axis=1) + pltpu.roll(c3, LANES - 48, axis=1)
    chi, cmid, clo = _split3(csum * LOG2E)
    cp = jnp.where(lane < 8, chi, jnp.where(lane < 16, cmid, clo))
    q_bias_lo = jnp.where(lane < 24, cp, jnp.where(lane < 48, 1.0, 0.0))
    k_bias_lo = jnp.where(lane < 24, 1.0, jnp.where(lane < 48, -pltpu.roll(cp, 24, axis=1), 0.0))
    q_bias_hi = pltpu.roll(q_bias_lo, HEAD_DIM, axis=1)
    k_bias_hi = pltpu.roll(k_bias_lo, HEAD_DIM, axis=1)

    for h in range(N_HEADS):
        blk = h // 2
        even = (h % 2 == 0)
        in_head = (lane < HEAD_DIM) if even else (lane >= HEAD_DIM)
        own_bias = ((lane & 7) == h) & ((lane & (HEAD_DIM - 1)) < 48)
        zq = zqkv[:, blk * LANES:(blk + 1) * LANES] * (HEAD_DIM ** -0.5 * LOG2E)
        zk = zqkv[:, ATT_WIDTH + blk * LANES:ATT_WIDTH + (blk + 1) * LANES]
        zv = zqkv[:, 2 * ATT_WIDTH + blk * LANES:2 * ATT_WIDTH + (blk + 1) * LANES]
        v_extra = jnp.where(lane == (HEAD_DIM if even else 0), 1.0, 0.0)
        qp_ref[0, h, rows, :] = jnp.where(in_head, zq, q_bias_hi if even else q_bias_lo).astype(BF16)
        kp_ref[0, h, rows, :] = jnp.where(
            in_head, zk, jnp.where(own_bias, k_bias_hi if even else k_bias_lo, 0.0)).astype(BF16)
        vp_ref[0, h, rows, :] = jnp.where(in_head, zv, v_extra).astype(BF16)


def _attn_kernel(q_ref, k_ref, v_ref, o_ref, acc_ref, m_ref, s_scr, mc_scr):
    n_full = pl.program_id(2) * STAIRS

    def scores(b, slot, stair=None):
        row0 = 0 if stair is None else stair * T_K
        start = pl.multiple_of(b * T_K, T_K)
        for hh in range(2):
            q = q_ref[0, hh, row0:, :]
            kb = k_ref[0, hh, pl.ds(start, T_K), :]
            s = lax.dot_general(q, kb, (((1,), (1,)), ((), ())), preferred_element_type=F32)
            parts = [(row0, T_Q, s)]
            if stair is not None:
                r = lax.broadcasted_iota(jnp.int32, (T_K, T_K), 0)
                c = lax.broadcasted_iota(jnp.int32, (T_K, T_K), 1)
                parts = [(row0, row0 + T_K, jnp.where(c <= r, s[:T_K], NEG_BIG))]
                if row0 + T_K < T_Q:
                    parts.append((row0 + T_K, T_Q, s[T_K:]))
            for lo, hi, sp in parts:
                s_scr[slot, hh, lo:hi, :] = sp
                mc_scr[slot, hh, lo:hi, :] = jnp.broadcast_to(
                    jnp.max(sp, axis=-1, keepdims=True), (hi - lo, LANES))

    def update(b, slot, row0=0, first=False):
        start = pl.multiple_of(b * T_K, T_K)
        for hh in range(2):
            vb = v_ref[0, hh, pl.ds(start, T_K), :]
            m_blk = mc_scr[slot, hh, row0:, :]
            if first:
                m_new = m_blk
            else:
                m_old = m_ref[hh, row0:, :]
                m_new = jnp.maximum(m_old, m_blk)
            m_wide = jnp.concatenate([m_new] * (T_K // LANES), axis=1)
            p = jnp.exp2(s_scr[slot, hh, row0:, :] - m_wide).astype(BF16)
            pv = jnp.dot(p, vb, preferred_element_type=F32)
            if first:
                acc_ref[hh] = pv
            else:
                acc_ref[hh, row0:, :] = jnp.exp2(m_old - m_new) * acc_ref[hh, row0:, :] + pv
            m_ref[hh, row0:, :] = m_new

    def staircase(first_blk, is_tile_start):
        for k in range(STAIRS - 1):
            scores(first_blk + k + 1, (k + 1) % 2, stair=k + 1)
            update(first_blk + k, k % 2, row0=k * T_K, first=is_tile_start and k == 0)
        update(first_blk + STAIRS - 1, (STAIRS - 1) % 2, row0=(STAIRS - 1) * T_K)

    @pl.when(n_full == 0)
    def _():
        scores(0, 0, stair=0)
        staircase(0, True)

    @pl.when(n_full > 0)
    def _():
        scores(0, 0)
        scores(1, 1)
        update(0, 0, first=True)

        def body(i, carry):
            b = 1 + ATT_UNROLL * i
            scores(b + 1, 0)
            update(b, 1)
            scores(b + 2, 1)
            update(b + 1, 0)
            return carry

        lax.fori_loop(0, (n_full - 2) // ATT_UNROLL, body, 0)
        scores(n_full, 0, stair=0)
        update(n_full - 1, 1)
        staircase(n_full, False)

    a0 = acc_ref[0]
    a1 = acc_ref[1]
    lane = lax.broadcasted_iota(jnp.int32, (T_Q, LANES), 1)
    o_ref[0] = jnp.where(lane < HEAD_DIM, a0 / a0[:, HEAD_DIM:HEAD_DIM + 1], a1 / a1[:, 0:1])


def _mlp_kernel(x_ref, ya_ref, ys_ref, p_ref, woa_ref, wos_ref, w1_ref, w2_ref, wg_ref, wpe_ref,
                gatt_ref, gpm_ref, gpf_ref, gqf_ref, bg_ref, o_ref):
    subs = [slice(r, r + T_MLP_SUB) for r in range(0, T_MLP, T_MLP_SUB)]
    y = []
    for rows in subs:
        yan = _rms(ya_ref[rows, :], gatt_ref[...]).astype(BF16)
        y.append(jnp.dot(yan, woa_ref[...], preferred_element_type=F32)
                 + jnp.dot(ys_ref[rows, :], wos_ref[...], preferred_element_type=F32))
    h1 = [x_ref[rows, :] + _rms(yy, gpm_ref[...]) for rows, yy in zip(subs, y)]
    cn = [_rms(hh, gpf_ref[...]).astype(BF16) for hh in h1]
    ff = [None] * len(subs)
    for c in range(D_FF // FF_CHUNK):
        cols = slice(c * FF_CHUNK, (c + 1) * FF_CHUNK)
        for i in range(len(subs)):
            hid = jnp.dot(cn[i], w1_ref[:, cols], preferred_element_type=F32)
            hid = jnp.square(jnp.maximum(hid, 0.0)).astype(BF16)
            part = jnp.dot(hid, w2_ref[cols, :], preferred_element_type=F32)
            ff[i] = part if ff[i] is None else ff[i] + part
    for i, rows in enumerate(subs):
        h2 = h1[i] + _rms(ff[i], gqf_ref[...])
        gate = jax.nn.sigmoid(
            jnp.dot(h2.astype(BF16), wg_ref[...], preferred_element_type=F32) + bg_ref[...])
        pe = jnp.dot(p_ref[rows, :].astype(BF16), wpe_ref[...], preferred_element_type=F32)
        o_ref[rows, :] = h2 + gate * pe


def _layer(h, p, w_in, f_bias, sg_ln_g, sg_ln_b, sg_w, sg_b, att_out_g, sg_out_g, w_out,
           pre_mix_g, post_mix_g, pre_ffn_g, post_ffn_g, w_ff1, w_ff2, ple_w, ple_gate_w, ple_gate_b):
    B, S, D = h.shape
    R = B * S
    row = lambda a: a.reshape(1, -1).astype(F32)

    wqkv = w_in[:, :3 * ATT_WIDTH].astype(BF16)
    wf8 = w_in[:, 3 * ATT_WIDTH:3 * ATT_WIDTH + N_HEADS]
    n_rep = 9
    wf = jnp.pad(jnp.tile(wf8, (1, n_rep)), ((0, 0), (0, LANES - n_rep * N_HEADS))).astype(BF16)
    fb = jnp.pad(jnp.tile(f_bias, n_rep), (0, LANES - n_rep * N_HEADS)).reshape(1, LANES).astype(F32)
    tril = jnp.tril(jnp.ones((T_SUB, T_SUB), BF16))
    wuv = w_in[:, 3 * ATT_WIDTH + N_HEADS:].astype(BF16)
    sgb = jnp.repeat(jnp.transpose(sg_b), HEAD_DIM, axis=1).astype(F32)

    const2 = lambda shape: pl.BlockSpec(shape, lambda b, t: (0,) * len(shape))
    qp, kp, vp, ysg = pl.pallas_call(
        _inproj_kernel,
        grid=(B, S // T_IN),
        in_specs=[
            pl.BlockSpec((1, T_IN, D), lambda b, t: (b, t, 0)),
            const2((1, D)),
            const2((D, 3 * ATT_WIDTH)),
            const2((D, LANES)),
            const2((1, LANES)),
            const2((T_SUB, T_SUB)),
            const2((D, 2 * SG_WIDTH)),
            const2((1, SG_WIDTH)),
            const2((1, SG_WIDTH)),
            const2((N_SG_GROUPS, CHUNK, CHUNK)),
            const2((CHUNK, SG_WIDTH)),
            const2((1, SG_WIDTH)),
        ],
        out_specs=[
            pl.BlockSpec((1, N_HEADS, T_IN, LANES), lambda b, t: (b, 0, t, 0)),
            pl.BlockSpec((1, N_HEADS, T_IN, LANES), lambda b, t: (b, 0, t, 0)),
            pl.BlockSpec((1, N_HEADS, T_IN, LANES), lambda b, t: (b, 0, t, 0)),
            pl.BlockSpec((1, T_IN, SG_WIDTH), lambda b, t: (b, t, 0)),
        ],
        out_shape=[
            jax.ShapeDtypeStruct((B, N_HEADS, S, LANES), BF16),
            jax.ShapeDtypeStruct((B, N_HEADS, S, LANES), BF16),
            jax.ShapeDtypeStruct((B, N_HEADS, S, LANES), BF16),
            jax.ShapeDtypeStruct((B, S, SG_WIDTH), BF16),
        ],
        scratch_shapes=[pltpu.VMEM((1, LANES), F32), pltpu.VMEM((T_IN, SG_WIDTH), F32)],
        compiler_params=pltpu.CompilerParams(
            dimension_semantics=("arbitrary", "arbitrary"), vmem_limit_bytes=VMEM_LIMIT),
        name="inproj",
    )(h, row(pre_mix_g), wqkv, wf, fb, tril, wuv, row(sg_ln_g), row(sg_ln_b), sg_w.astype(F32), sgb, row(sg_out_g))

    yatt = pl.pallas_call(
        _attn_kernel,
        grid=(B, HEAD_PAIRS, S // T_Q),
        in_specs=[
            pl.BlockSpec((1, 2, T_Q, LANES), lambda b, hp, i: (b, hp, i, 0)),
            pl.BlockSpec((1, 2, S, LANES), lambda b, hp, i: (b, hp, 0, 0)),
            pl.BlockSpec((1, 2, S, LANES), lambda b, hp, i: (b, hp, 0, 0)),
        ],
        out_specs=pl.BlockSpec((1, T_Q, LANES), lambda b, hp, i: (b, i, hp)),
        out_shape=jax.ShapeDtypeStruct((B, S, ATT_WIDTH), F32),
        scratch_shapes=[pltpu.VMEM((2, T_Q, LANES), F32), pltpu.VMEM((2, T_Q, LANES), F32),
                        pltpu.VMEM((2, 2, T_Q, T_K), F32), pltpu.VMEM((2, 2, T_Q, LANES), F32)],
        compiler_params=pltpu.CompilerParams(
            dimension_semantics=("arbitrary", "arbitrary", "arbitrary"), vmem_limit_bytes=VMEM_LIMIT),
        name="fox_attn",
    )(qp, kp, vp)

    rows = lambda w: pl.BlockSpec((T_MLP, w), lambda i: (i, 0))
    const = lambda shape: pl.BlockSpec(shape, lambda i: (0,) * len(shape), pipeline_mode=pl.Buffered(1))
    out = pl.pallas_call(
        _mlp_kernel,
        grid=(R // T_MLP,),
        in_specs=[
            rows(D), rows(ATT_WIDTH), rows(SG_WIDTH), rows(PLE_DIM),
            const((ATT_WIDTH, D)), const((SG_WIDTH, D)),
            const((D, D_FF)), const((D_FF, D)),
            const((D, D)), const((PLE_DIM, D)),
            const((1, ATT_WIDTH)), const((1, D)), const((1, D)), const((1, D)), const((1, D)),
        ],
        out_specs=rows(D),
        out_shape=jax.ShapeDtypeStruct((R, D), F32),
        compiler_params=pltpu.CompilerParams(
            dimension_semantics=("arbitrary",), vmem_limit_bytes=VMEM_LIMIT),
        name="mlp",
    )(h.reshape(R, D), yatt.reshape(R, ATT_WIDTH), ysg.reshape(R, SG_WIDTH), p.reshape(R, PLE_DIM),
      w_out[:ATT_WIDTH].astype(BF16), w_out[ATT_WIDTH:].astype(BF16),
      w_ff1.astype(BF16), w_ff2.astype(BF16), ple_gate_w.astype(BF16), ple_w.astype(BF16),
      row(att_out_g), row(post_mix_g), row(pre_ffn_g), row(post_ffn_g), row(ple_gate_b))
    return out.reshape(B, S, D)


def kernel(x, p, w_in, f_bias, sg_ln_g, sg_ln_b, sg_w, sg_b, att_out_g, sg_out_g, w_out,
           pre_mix_g, post_mix_g, pre_ffn_g, post_ffn_g, w_ff1, w_ff2, ple_w, ple_gate_w, ple_gate_b):
    h = x
    for i in range(p.shape[0]):
        h = _layer(h, p[i], w_in[i], f_bias[i], sg_ln_g[i], sg_ln_b[i], sg_w[i], sg_b[i],
                   att_out_g[i], sg_out_g[i], w_out[i], pre_mix_g[i], post_mix_g[i],
                   pre_ffn_g[i], post_ffn_g[i], w_ff1[i], w_ff2[i], ple_w[i], ple_gate_w[i],
                   ple_gate_b[i])
    return h
```

```python
import functools

import jax
import jax.numpy as jnp
from jax import lax
from jax.experimental import pallas as pl
from jax.experimental.pallas import tpu as pltpu

D_MODEL = 1024
PLE_DIM = 256
HEAD_DIM = 64
ATT_WIDTH = 512
N_HEADS = 8
SG_WIDTH = 512
N_SG_GROUPS = 8
CHUNK = 128
D_FF = 4096
EPS = 1e-6

LANES = 128
HEAD_PAIRS = N_HEADS // 2
NEG_BIG = -1e30
LOG2E = 1.4426950408889634

T_IN = 512
T_SUB = 256
T_Q = 2048
T_K = 512
STAIRS = T_Q // T_K
ATT_UNROLL = 2
assert ATT_UNROLL == 2 and STAIRS % 2 == 0
T_MLP = 512
T_MLP_SUB = 256
FF_CHUNK = 1024

VMEM_LIMIT = 56 * 1024 * 1024

F32 = jnp.float32
BF16 = jnp.bfloat16


def _rms(x, g):
    return x * lax.rsqrt(jnp.mean(x * x, axis=-1, keepdims=True) + EPS) * g


def _gelu_tanh(x):
    return x * (0.5 * (1.0 + jnp.tanh(0.7978845608028654 * (x + 0.044715 * (x * x * x)))))


def _split3(x):
    hi = x.astype(BF16).astype(F32)
    r = x - hi
    mid = r.astype(BF16).astype(F32)
    lo = (r - mid).astype(BF16).astype(F32)
    return hi, mid, lo


def _inproj_kernel(x_ref, g_ref, wqkv_ref, wf_ref, fb_ref, tril_ref, wuv_ref, lng_ref, lnb_ref,
                   sgw_ref, sgb_ref, sgog_ref,
                   qp_ref, kp_ref, vp_ref, ysg_ref,
                   carry_ref, ysg_scr):
    t = pl.program_id(1)

    @pl.when(t == 0)
    def _():
        carry_ref[...] = jnp.zeros_like(carry_ref)

    subs = [slice(r, r + T_SUB) for r in range(0, T_IN, T_SUB)]
    lane = lax.broadcasted_iota(jnp.int32, (T_SUB, LANES), 1)
    ab = [_rms(x_ref[0, rows, :], g_ref[...]).astype(BF16) for rows in subs]
    zuv = [jnp.dot(a, wuv_ref[...], preferred_element_type=F32) for a in ab]
    f = [jnp.dot(a, wf_ref[...], preferred_element_type=F32) + fb_ref[...] for a in ab]

    c3 = []
    carry = carry_ref[...]
    for fi in f:
        logf = jnp.minimum(fi, 0.0) - jnp.log1p(jnp.exp(-jnp.abs(fi)))
        hi, mid, lo = _split3(logf)
        pieces = jnp.where(lane < 24, hi, jnp.where(lane < 48, mid, lo)).astype(BF16)
        c3.append(jnp.dot(tril_ref[...], pieces, preferred_element_type=F32) + carry)
        carry = c3[-1][T_SUB - 1:T_SUB, :]
    carry_ref[...] = carry

    zqkv = []
    for rows, a, zuv_i in zip(subs, ab, zuv):
        zqkv.append(jnp.dot(a, wqkv_ref[...], preferred_element_type=F32))
        _spatial_gating(rows, zuv_i, lng_ref, lnb_ref, sgw_ref, sgb_ref, sgog_ref, ysg_ref, ysg_scr)
    for rows, zqkv_i, c3_i in zip(subs, zqkv, c3):
        _pack_heads(rows, lane, zqkv_i, c3_i, qp_ref, kp_ref, vp_ref)


def _spatial_gating(rows, zuv, lng_ref, lnb_ref, sgw_ref, sgb_ref, sgog_ref, ysg_ref, ysg_scr):
    r0 = rows.start
    u = _gelu_tanh(zuv[:, :SG_WIDTH])
    v = _gelu_tanh(zuv[:, SG_WIDTH:])
    vc = v - jnp.mean(v, axis=-1, keepdims=True)
    vln = vc * lax.rsqrt(jnp.mean(vc * vc, axis=-1, keepdims=True) + EPS) * lng_ref[...] + lnb_ref[...]
    vb = vln.astype(BF16)
    n_chunks = T_SUB // CHUNK
    ci = lax.broadcasted_iota(jnp.int32, (CHUNK, CHUNK), 0)
    cj = lax.broadcasted_iota(jnp.int32, (CHUNK, CHUNK), 1)
    lane_c = lax.broadcasted_iota(jnp.int32, (CHUNK, LANES), 1)
    for pb in range(SG_WIDTH // LANES):
        vcat = jnp.concatenate(
            [vb[c * CHUNK:(c + 1) * CHUNK, pb * LANES:(pb + 1) * LANES] for c in range(n_chunks)], axis=1)
        res = []
        for g in (2 * pb, 2 * pb + 1):
            w = jnp.where(cj <= ci, sgw_ref[g], 0.0).astype(BF16)
            res.append(jnp.dot(w, vcat, preferred_element_type=F32))
        for c in range(n_chunks):
            mixed = jnp.where(lane_c < HEAD_DIM,
                              res[0][:, c * LANES:(c + 1) * LANES],
                              res[1][:, c * LANES:(c + 1) * LANES])
            mixed = mixed + sgb_ref[:, pb * LANES:(pb + 1) * LANES]
            ysg_scr[r0 + c * CHUNK:r0 + (c + 1) * CHUNK, pb * LANES:(pb + 1) * LANES] = (
                u[c * CHUNK:(c + 1) * CHUNK, pb * LANES:(pb + 1) * LANES] * mixed)
    ysg_ref[0, rows, :] = _rms(ysg_scr[rows, :], sgog_ref[...]).astype(BF16)


def _pack_heads(rows, lane, zqkv, c3, qp_ref, kp_ref, vp_ref):
    csum = c3 + pltpu.roll(c3, LANES - 24, axis=1) + pltpu.roll(c3, LANES - 48, axis=1)
    chi, cmid, clo = _split3(csum * LOG2E)
    cp = jnp.where(lane < 8, chi, jnp.where(lane < 16, cmid, clo))
    q_bias_lo = jnp.where(lane < 24, cp, jnp.where(lane < 48, 1.0, 0.0))
    k_bias_lo = jnp.where(lane < 24, 1.0, jnp.where(lane < 48, -pltpu.roll(cp, 24, axis=1), 0.0))
    q_bias_hi = pltpu.roll(q_bias_lo, HEAD_DIM, axis=1)
    k_bias_hi = pltpu.roll(k_bias_lo, HEAD_DIM, axis=1)

    for h in range(N_HEADS):
        blk = h // 2
        even = (h % 2 == 0)
        in_head = (lane < HEAD_DIM) if even else (lane >= HEAD_DIM)
        own_bias = ((lane & 7) == h) & ((lane & (HEAD_DIM - 1)) < 48)
        zq = zqkv[:, blk * LANES:(blk + 1) * LANES] * (HEAD_DIM ** -0.5 * LOG2E)
        zk = zqkv[:, ATT_WIDTH + blk * LANES:ATT_WIDTH + (blk + 1) * LANES]
        zv = zqkv[:, 2 * ATT_WIDTH + blk * LANES:2 * ATT_WIDTH + (blk + 1) * LANES]
        v_extra = jnp.where(lane == (HEAD_DIM if even else 0), 1.0, 0.0)
        qp_ref[0, h, rows, :] = jnp.where(in_head, zq, q_bias_hi if even else q_bias_lo).astype(BF16)
        kp_ref[0, h, rows, :] = jnp.where(
            in_head, zk, jnp.where(own_bias, k_bias_hi if even else k_bias_lo, 0.0)).astype(BF16)
        vp_ref[0, h, rows, :] = jnp.where(in_head, zv, v_extra).astype(BF16)


def _attn_kernel(q_ref, k_ref, v_ref, o_ref, acc_ref, m_ref, s_scr, mc_scr):
    n_full = pl.program_id(2) * STAIRS

    def scores(b, slot, stair=None):
        row0 = 0 if stair is None else stair * T_K
        start = pl.multiple_of(b * T_K, T_K)
        for hh in range(2):
            q = q_ref[0, hh, row0:, :]
            kb = k_ref[0, hh, pl.ds(start, T_K), :]
            s = lax.dot_general(q, kb, (((1,), (1,)), ((), ())), preferred_element_type=F32)
            parts = [(row0, T_Q, s)]
            if stair is not None:
                r = lax.broadcasted_iota(jnp.int32, (T_K, T_K), 0)
                c = lax.broadcasted_iota(jnp.int32, (T_K, T_K), 1)
                parts = [(row0, row0 + T_K, jnp.where(c <= r, s[:T_K], NEG_BIG))]
                if row0 + T_K < T_Q:
                    parts.append((row0 + T_K, T_Q, s[T_K:]))
            for lo, hi, sp in parts:
                s_scr[slot, hh, lo:hi, :] = sp
                mc_scr[slot, hh, lo:hi, :] = jnp.broadcast_to(
                    jnp.max(sp, axis=-1, keepdims=True), (hi - lo, LANES))

    def update(b, slot, row0=0, first=False):
        start = pl.multiple_of(b * T_K, T_K)
        for hh in range(2):
            vb = v_ref[0, hh, pl.ds(start, T_K), :]
            m_blk = mc_scr[slot, hh, row0:, :]
            if first:
                m_new = m_blk
            else:
                m_old = m_ref[hh, row0:, :]
                m_new = jnp.maximum(m_old, m_blk)
            m_wide = jnp.concatenate([m_new] * (T_K // LANES), axis=1)
            p = jnp.exp2(s_scr[slot, hh, row0:, :] - m_wide).astype(BF16)
            pv = jnp.dot(p, vb, preferred_element_type=F32)
            if first:
                acc_ref[hh] = pv
            else:
                acc_ref[hh, row0:, :] = jnp.exp2(m_old - m_new) * acc_ref[hh, row0:, :] + pv
            m_ref[hh, row0:, :] = m_new

    def staircase(first_blk, is_tile_start):
        for k in range(STAIRS - 1):
            scores(first_blk + k + 1, (k + 1) % 2, stair=k + 1)
            update(first_blk + k, k % 2, row0=k * T_K, first=is_tile_start and k == 0)
        update(first_blk + STAIRS - 1, (STAIRS - 1) % 2, row0=(STAIRS - 1) * T_K)

    @pl.when(n_full == 0)
    def _():
        scores(0, 0, stair=0)
        staircase(0, True)

    @pl.when(n_full > 0)
    def _():
        scores(0, 0)
        scores(1, 1)
        update(0, 0, first=True)

        def body(i, carry):
            b = 1 + ATT_UNROLL * i
            scores(b + 1, 0)
            update(b, 1)
            scores(b + 2, 1)
            update(b + 1, 0)
            return carry

        lax.fori_loop(0, (n_full - 2) // ATT_UNROLL, body, 0)
        scores(n_full, 0, stair=0)
        update(n_full - 1, 1)
        staircase(n_full, False)

    a0 = acc_ref[0]
    a1 = acc_ref[1]
    lane = lax.broadcasted_iota(jnp.int32, (T_Q, LANES), 1)
    o_ref[0] = jnp.where(lane < HEAD_DIM, a0 / a0[:, HEAD_DIM:HEAD_DIM + 1], a1 / a1[:, 0:1])


def _mlp_kernel(x_ref, ya_ref, ys_ref, p_ref, woa_ref, wos_ref, w1_ref, w2_ref, wg_ref, wpe_ref,
                gatt_ref, gpm_ref, gpf_ref, gqf_ref, bg_ref, o_ref):
    subs = [slice(r, r + T_MLP_SUB) for r in range(0, T_MLP, T_MLP_SUB)]
    y = []
    for rows in subs:
        yan = _rms(ya_ref[rows, :], gatt_ref[...]).astype(BF16)
        y.append(jnp.dot(yan, woa_ref[...], preferred_element_type=F32)
                 + jnp.dot(ys_ref[rows, :], wos_ref[...], preferred_element_type=F32))
    h1 = [x_ref[rows, :] + _rms(yy, gpm_ref[...]) for rows, yy in zip(subs, y)]
    cn = [_rms(hh, gpf_ref[...]).astype(BF16) for hh in h1]
    ff = [None] * len(subs)
    for c in range(D_FF // FF_CHUNK):
        cols = slice(c * FF_CHUNK, (c + 1) * FF_CHUNK)
        for i in range(len(subs)):
            hid = jnp.dot(cn[i], w1_ref[:, cols], preferred_element_type=F32)
            hid = jnp.square(jnp.maximum(hid, 0.0)).astype(BF16)
            part = jnp.dot(hid, w2_ref[cols, :], preferred_element_type=F32)
            ff[i] = part if ff[i] is None else ff[i] + part
    for i, rows in enumerate(subs):
        h2 = h1[i] + _rms(ff[i], gqf_ref[...])
        gate = jax.nn.sigmoid(
            jnp.dot(h2.astype(BF16), wg_ref[...], preferred_element_type=F32) + bg_ref[...])
        pe = jnp.dot(p_ref[rows, :].astype(BF16), wpe_ref[...], preferred_element_type=F32)
        o_ref[rows, :] = h2 + gate * pe


def _layer(h, p, w_in, f_bias, sg_ln_g, sg_ln_b, sg_w, sg_b, att_out_g, sg_out_g, w_out,
           pre_mix_g, post_mix_g, pre_ffn_g, post_ffn_g, w_ff1, w_ff2, ple_w, ple_gate_w, ple_gate_b):
    B, S, D = h.shape
    R = B * S
    row = lambda a: a.reshape(1, -1).astype(F32)

    wqkv = w_in[:, :3 * ATT_WIDTH].astype(BF16)
    wf8 = w_in[:, 3 * ATT_WIDTH:3 * ATT_WIDTH + N_HEADS]
    n_rep = 9
    wf = jnp.pad(jnp.tile(wf8, (1, n_rep)), ((0, 0), (0, LANES - n_rep * N_HEADS))).astype(BF16)
    fb = jnp.pad(jnp.tile(f_bias, n_rep), (0, LANES - n_rep * N_HEADS)).reshape(1, LANES).astype(F32)
    tril = jnp.tril(jnp.ones((T_SUB, T_SUB), BF16))
    wuv = w_in[:, 3 * ATT_WIDTH + N_HEADS:].astype(BF16)
    sgb = jnp.repeat(jnp.transpose(sg_b), HEAD_DIM, axis=1).astype(F32)

    const2 = lambda shape: pl.BlockSpec(shape, lambda b, t: (0,) * len(shape))
    qp, kp, vp, ysg = pl.pallas_call(
        _inproj_kernel,
        grid=(B, S // T_IN),
        in_specs=[
            pl.BlockSpec((1, T_IN, D), lambda b, t: (b, t, 0)),
            const2((1, D)),
            const2((D, 3 * ATT_WIDTH)),
            const2((D, LANES)),
            const2((1, LANES)),
            const2((T_SUB, T_SUB)),
            const2((D, 2 * SG_WIDTH)),
            const2((1, SG_WIDTH)),
            const2((1, SG_WIDTH)),
            const2((N_SG_GROUPS, CHUNK, CHUNK)),
            const2((CHUNK, SG_WIDTH)),
            const2((1, SG_WIDTH)),
        ],
        out_specs=[
            pl.BlockSpec((1, N_HEADS, T_IN, LANES), lambda b, t: (b, 0, t, 0)),
            pl.BlockSpec((1, N_HEADS, T_IN, LANES), lambda b, t: (b, 0, t, 0)),
            pl.BlockSpec((1, N_HEADS, T_IN, LANES), lambda b, t: (b, 0, t, 0)),
            pl.BlockSpec((1, T_IN, SG_WIDTH), lambda b, t: (b, t, 0)),
        ],
        out_shape=[
            jax.ShapeDtypeStruct((B, N_HEADS, S, LANES), BF16),
            jax.ShapeDtypeStruct((B, N_HEADS, S, LANES), BF16),
            jax.ShapeDtypeStruct((B, N_HEADS, S, LANES), BF16),
            jax.ShapeDtypeStruct((B, S, SG_WIDTH), BF16),
        ],
        scratch_shapes=[pltpu.VMEM((1, LANES), F32), pltpu.VMEM((T_IN, SG_WIDTH), F32)],
        compiler_params=pltpu.CompilerParams(
            dimension_semantics=("arbitrary", "arbitrary"), vmem_limit_bytes=VMEM_LIMIT),
        name="inproj",
    )(h, row(pre_mix_g), wqkv, wf, fb, tril, wuv, row(sg_ln_g), row(sg_ln_b), sg_w.astype(F32), sgb, row(sg_out_g))

    yatt = pl.pallas_call(
        _attn_kernel,
        grid=(B, HEAD_PAIRS, S // T_Q),
        in_specs=[
            pl.BlockSpec((1, 2, T_Q, LANES), lambda b, hp, i: (b, hp, i, 0)),
            pl.BlockSpec((1, 2, S, LANES), lambda b, hp, i: (b, hp, 0, 0)),
            pl.BlockSpec((1, 2, S, LANES), lambda b, hp, i: (b, hp, 0, 0)),
        ],
        out_specs=pl.BlockSpec((1, T_Q, LANES), lambda b, hp, i: (b, i, hp)),
        out_shape=jax.ShapeDtypeStruct((B, S, ATT_WIDTH), F32),
        scratch_shapes=[pltpu.VMEM((2, T_Q, LANES), F32), pltpu.VMEM((2, T_Q, LANES), F32),
                        pltpu.VMEM((2, 2, T_Q, T_K), F32), pltpu.VMEM((2, 2, T_Q, LANES), F32)],
        compiler_params=pltpu.CompilerParams(
            dimension_semantics=("arbitrary", "arbitrary", "arbitrary"), vmem_limit_bytes=VMEM_LIMIT),
        name="fox_attn",
    )(qp, kp, vp)

    rows = lambda w: pl.BlockSpec((T_MLP, w), lambda i: (i, 0))
    const = lambda shape: pl.BlockSpec(shape, lambda i: (0,) * len(shape), pipeline_mode=pl.Buffered(1))
    out = pl.pallas_call(
        _mlp_kernel,
        grid=(R // T_MLP,),
        in_specs=[
            rows(D), rows(ATT_WIDTH), rows(SG_WIDTH), rows(PLE_DIM),
            const((ATT_WIDTH, D)), const((SG_WIDTH, D)),
            const((D, D_FF)), const((D_FF, D)),
            const((D, D)), const((PLE_DIM, D)),
            const((1, ATT_WIDTH)), const((1, D)), const((1, D)), const((1, D)), const((1, D)),
        ],
        out_specs=rows(D),
        out_shape=jax.ShapeDtypeStruct((R, D), F32),
        compiler_params=pltpu.CompilerParams(
            dimension_semantics=("arbitrary",), vmem_limit_bytes=VMEM_LIMIT),
        name="mlp",
    )(h.reshape(R, D), yatt.reshape(R, ATT_WIDTH), ysg.reshape(R, SG_WIDTH), p.reshape(R, PLE_DIM),
      w_out[:ATT_WIDTH].astype(BF16), w_out[ATT_WIDTH:].astype(BF16),
      w_ff1.astype(BF16), w_ff2.astype(BF16), ple_gate_w.astype(BF16), ple_w.astype(BF16),
      row(att_out_g), row(post_mix_g), row(pre_ffn_g), row(post_ffn_g), row(ple_gate_b))
    return out.reshape(B, S, D)


def kernel(x, p, w_in, f_bias, sg_ln_g, sg_ln_b, sg_w, sg_b, att_out_g, sg_out_g, w_out,
           pre_mix_g, post_mix_g, pre_ffn_g, post_ffn_g, w_ff1, w_ff2, ple_w, ple_gate_w, ple_gate_b):
    h = x
    for i in range(p.shape[0]):
        h = _layer(h, p[i], w_in[i], f_bias[i], sg_ln_g[i], sg_ln_b[i], sg_w[i], sg_b[i],
                   att_out_g[i], sg_out_g[i], w_out[i], pre_mix_g[i], post_mix_g[i],
                   pre_ffn_g[i], post_ffn_g[i], w_ff1[i], w_ff2[i], ple_w[i], ple_gate_w[i],
                   ple_gate_b[i])
    return h
```

```python
import functools

import jax
import jax.numpy as jnp
from jax import lax
from jax.experimental import pallas as pl
from jax.experimental.pallas import tpu as pltpu

D_MODEL = 1024
PLE_DIM = 256
HEAD_DIM = 64
ATT_WIDTH = 512
N_HEADS = 8
SG_WIDTH = 512
N_SG_GROUPS = 8
CHUNK = 128
D_FF = 4096
EPS = 1e-6

LANES = 128
HEAD_PAIRS = N_HEADS // 2
NEG_BIG = -1e30
LOG2E = 1.4426950408889634
N_PIECES = 3
PIECE_LANES = N_PIECES * N_HEADS
BIAS_LANES = 2 * PIECE_LANES

T_IN = 512
T_SUB = 256
T_Q = 2048
T_K = 512
STAIRS = T_Q // T_K
ATT_UNROLL = 2
assert ATT_UNROLL == 2 and STAIRS % 2 == 0
T_MLP = 512
T_MLP_SUB = 256
FF_CHUNK = 1024

VMEM_LIMIT = 56 * 1024 * 1024

F32 = jnp.float32
BF16 = jnp.bfloat16


def _rms(x, g):
    return x * lax.rsqrt(jnp.mean(x * x, axis=-1, keepdims=True) + EPS) * g


def _gelu_tanh(x):
    return x * (0.5 * (1.0 + jnp.tanh(0.7978845608028654 * (x + 0.044715 * (x * x * x)))))


def _split3(x):
    hi = x.astype(BF16).astype(F32)
    r = x - hi
    mid = r.astype(BF16).astype(F32)
    lo = (r - mid).astype(BF16).astype(F32)
    return hi, mid, lo


def _inproj_kernel(x_ref, g_ref, wqkv_ref, wf_ref, fb_ref, tril_ref, wuv_ref, lng_ref, lnb_ref,
                   sgw_ref, sgb_ref, sgog_ref, w1_ref, w2_ref, wg_ref, wo_ref,
                   qp_ref, kp_ref, vp_ref, ysg_ref, w1b_ref, w2b_ref, wgb_ref, wob_ref,
                   carry_ref, ysg_scr):
    t = pl.program_id(1)

    @pl.when(t == 0)
    def _():
        carry_ref[...] = jnp.zeros_like(carry_ref)

    w1b_ref[...] = w1_ref[...].astype(BF16)
    w2b_ref[...] = w2_ref[...].astype(BF16)
    wgb_ref[...] = wg_ref[...].astype(BF16)
    wob_ref[...] = wo_ref[...].astype(BF16)

    subs = [slice(r, r + T_SUB) for r in range(0, T_IN, T_SUB)]
    lane = lax.broadcasted_iota(jnp.int32, (T_SUB, LANES), 1)
    ab = [_rms(x_ref[0, rows, :], g_ref[...]).astype(BF16) for rows in subs]
    zuv = [jnp.dot(a, wuv_ref[...], preferred_element_type=F32) for a in ab]
    f = [jnp.dot(a, wf_ref[...], preferred_element_type=F32) + fb_ref[...] for a in ab]

    c3 = []
    carry = carry_ref[...]
    for fi in f:
        logf = jnp.minimum(fi, 0.0) - jnp.log1p(jnp.exp(-jnp.abs(fi)))
        hi, mid, lo = _split3(logf)
        pieces = jnp.where(lane < PIECE_LANES, hi, jnp.where(lane < 2 * PIECE_LANES, mid, lo)).astype(BF16)
        c3.append(jnp.dot(tril_ref[...], pieces, preferred_element_type=F32) + carry)
        carry = c3[-1][T_SUB - 1:T_SUB, :]
    carry_ref[...] = carry

    zqkv = []
    for rows, a, zuv_i in zip(subs, ab, zuv):
        zqkv.append(jnp.dot(a, wqkv_ref[...], preferred_element_type=F32))
        _spatial_gating(rows, zuv_i, lng_ref, lnb_ref, sgw_ref, sgb_ref, sgog_ref, ysg_ref, ysg_scr)
    for rows, zqkv_i, c3_i in zip(subs, zqkv, c3):
        _pack_heads(rows, lane, zqkv_i, c3_i, qp_ref, kp_ref, vp_ref)


def _spatial_gating(rows, zuv, lng_ref, lnb_ref, sgw_ref, sgb_ref, sgog_ref, ysg_ref, ysg_scr):
    r0 = rows.start
    u = _gelu_tanh(zuv[:, :SG_WIDTH])
    v = _gelu_tanh(zuv[:, SG_WIDTH:])
    vc = v - jnp.mean(v, axis=-1, keepdims=True)
    vln = vc * lax.rsqrt(jnp.mean(vc * vc, axis=-1, keepdims=True) + EPS) * lng_ref[...] + lnb_ref[...]
    vb = vln.astype(BF16)
    n_chunks = T_SUB // CHUNK
    ci = lax.broadcasted_iota(jnp.int32, (CHUNK, CHUNK), 0)
    cj = lax.broadcasted_iota(jnp.int32, (CHUNK, CHUNK), 1)
    lane_c = lax.broadcasted_iota(jnp.int32, (CHUNK, LANES), 1)
    for pb in range(SG_WIDTH // LANES):
        vcat = jnp.concatenate(
            [vb[c * CHUNK:(c + 1) * CHUNK, pb * LANES:(pb + 1) * LANES] for c in range(n_chunks)], axis=1)
        res = []
        for g in (2 * pb, 2 * pb + 1):
            w = jnp.where(cj <= ci, sgw_ref[g], 0.0).astype(BF16)
            res.append(jnp.dot(w, vcat, preferred_element_type=F32))
        for c in range(n_chunks):
            mixed = jnp.where(lane_c < HEAD_DIM,
                              res[0][:, c * LANES:(c + 1) * LANES],
                              res[1][:, c * LANES:(c + 1) * LANES])
            mixed = mixed + sgb_ref[:, pb * LANES:(pb + 1) * LANES]
            ysg_scr[r0 + c * CHUNK:r0 + (c + 1) * CHUNK, pb * LANES:(pb + 1) * LANES] = (
                u[c * CHUNK:(c + 1) * CHUNK, pb * LANES:(pb + 1) * LANES] * mixed)
    ysg_ref[0, rows, :] = _rms(ysg_scr[rows, :], sgog_ref[...]).astype(BF16)


def _pack_heads(rows, lane, zqkv, c3, qp_ref, kp_ref, vp_ref):
    csum = (c3 + pltpu.roll(c3, LANES - PIECE_LANES, axis=1)
            + pltpu.roll(c3, LANES - 2 * PIECE_LANES, axis=1))
    chi, cmid, clo = _split3(csum * LOG2E)
    cp = jnp.where(lane < N_HEADS, chi, jnp.where(lane < 2 * N_HEADS, cmid, clo))
    q_bias_lo = jnp.where(lane < PIECE_LANES, cp, jnp.where(lane < BIAS_LANES, 1.0, 0.0))
    k_bias_lo = jnp.where(lane < PIECE_LANES, 1.0,
                          jnp.where(lane < BIAS_LANES, -pltpu.roll(cp, PIECE_LANES, axis=1), 0.0))
    q_bias_hi = pltpu.roll(q_bias_lo, HEAD_DIM, axis=1)
    k_bias_hi = pltpu.roll(k_bias_lo, HEAD_DIM, axis=1)

    for h in range(N_HEADS):
        blk = h // 2
        even = (h % 2 == 0)
        in_head = (lane < HEAD_DIM) if even else (lane >= HEAD_DIM)
        own_bias = ((lane & (N_HEADS - 1)) == h) & ((lane & (HEAD_DIM - 1)) < BIAS_LANES)
        zq = zqkv[:, blk * LANES:(blk + 1) * LANES] * (HEAD_DIM ** -0.5 * LOG2E)
        zk = zqkv[:, ATT_WIDTH + blk * LANES:ATT_WIDTH + (blk + 1) * LANES]
        zv = zqkv[:, 2 * ATT_WIDTH + blk * LANES:2 * ATT_WIDTH + (blk + 1) * LANES]
        v_extra = jnp.where(lane == (HEAD_DIM if even else 0), 1.0, 0.0)
        qp_ref[0, h, rows, :] = jnp.where(in_head, zq, q_bias_hi if even else q_bias_lo).astype(BF16)
        kp_ref[0, h, rows, :] = jnp.where(
            in_head, zk, jnp.where(own_bias, k_bias_hi if even else k_bias_lo, 0.0)).astype(BF16)
        vp_ref[0, h, rows, :] = jnp.where(in_head, zv, v_extra).astype(BF16)


def _attn_kernel(q_ref, k_ref, v_ref, o_ref, acc_ref, m_ref, s_scr, mc_scr):
    n_full = pl.program_id(2) * STAIRS

    def scores(b, slot, stair=None):
        row0 = 0 if stair is None else stair * T_K
        start = pl.multiple_of(b * T_K, T_K)
        for hh in range(2):
            q = q_ref[0, hh, row0:, :]
            kb = k_ref[0, hh, pl.ds(start, T_K), :]
            s = lax.dot_general(q, kb, (((1,), (1,)), ((), ())), preferred_element_type=F32)
            parts = [(row0, T_Q, s)]
            if stair is not None:
                r = lax.broadcasted_iota(jnp.int32, (T_K, T_K), 0)
                c = lax.broadcasted_iota(jnp.int32, (T_K, T_K), 1)
                parts = [(row0, row0 + T_K, jnp.where(c <= r, s[:T_K], NEG_BIG))]
                if row0 + T_K < T_Q:
                    parts.append((row0 + T_K, T_Q, s[T_K:]))
            for lo, hi, sp in parts:
                s_scr[slot, hh, lo:hi, :] = sp
                mc_scr[slot, hh, lo:hi, :] = jnp.broadcast_to(
                    jnp.max(sp, axis=-1, keepdims=True), (hi - lo, LANES))

    def update(b, slot, row0=0, first=False):
        start = pl.multiple_of(b * T_K, T_K)
        for hh in range(2):
            vb = v_ref[0, hh, pl.ds(start, T_K), :]
            m_blk = mc_scr[slot, hh, row0:, :]
            if first:
                m_new = m_blk
            else:
                m_old = m_ref[hh, row0:, :]
                m_new = jnp.maximum(m_old, m_blk)
            m_wide = jnp.concatenate([m_new] * (T_K // LANES), axis=1)
            p = jnp.exp2(s_scr[slot, hh, row0:, :] - m_wide).astype(BF16)
            pv = jnp.dot(p, vb, preferred_element_type=F32)
            if first:
                acc_ref[hh] = pv
            else:
                acc_ref[hh, row0:, :] = jnp.exp2(m_old - m_new) * acc_ref[hh, row0:, :] + pv
            m_ref[hh, row0:, :] = m_new

    def staircase(first_blk, is_tile_start):
        for k in range(STAIRS - 1):
            scores(first_blk + k + 1, (k + 1) % 2, stair=k + 1)
            update(first_blk + k, k % 2, row0=k * T_K, first=is_tile_start and k == 0)
        update(first_blk + STAIRS - 1, (STAIRS - 1) % 2, row0=(STAIRS - 1) * T_K)

    @pl.when(n_full == 0)
    def _():
        scores(0, 0, stair=0)
        staircase(0, True)

    @pl.when(n_full > 0)
    def _():
        scores(0, 0)
        scores(1, 1)
        update(0, 0, first=True)

        def body(i, carry):
            b = 1 + ATT_UNROLL * i
            scores(b + 1, 0)
            update(b, 1)
            scores(b + 2, 1)
            update(b + 1, 0)
            return carry

        lax.fori_loop(0, (n_full - 2) // ATT_UNROLL, body, 0)
        scores(n_full, 0, stair=0)
        update(n_full - 1, 1)
        staircase(n_full, False)

    a0 = acc_ref[0]
    a1 = acc_ref[1]
    lane = lax.broadcasted_iota(jnp.int32, (T_Q, LANES), 1)
    o_ref[0] = jnp.where(lane < HEAD_DIM, a0 / a0[:, HEAD_DIM:HEAD_DIM + 1], a1 / a1[:, 0:1])


def _mlp_kernel(x_ref, ya_ref, ys_ref, p_ref, wo_ref, w1_ref, w2_ref, wg_ref, wpe_ref,
                gatt_ref, gpm_ref, gpf_ref, gqf_ref, bg_ref, o_ref):
    subs = [slice(r, r + T_MLP_SUB) for r in range(0, T_MLP, T_MLP_SUB)]
    y = []
    for rows in subs:
        yan = _rms(ya_ref[rows, :], gatt_ref[...]).astype(BF16)
        y.append(jnp.dot(yan, wo_ref[:ATT_WIDTH, :], preferred_element_type=F32)
                 + jnp.dot(ys_ref[rows, :], wo_ref[ATT_WIDTH:, :], preferred_element_type=F32))
    h1 = [x_ref[rows, :] + _rms(yy, gpm_ref[...]) for rows, yy in zip(subs, y)]
    cn = [_rms(hh, gpf_ref[...]).astype(BF16) for hh in h1]
    ff = [None] * len(subs)
    for c in range(D_FF // FF_CHUNK):
        cols = slice(c * FF_CHUNK, (c + 1) * FF_CHUNK)
        for i in range(len(subs)):
            hid = jnp.dot(cn[i], w1_ref[:, cols], preferred_element_type=F32)
            hid = jnp.square(jnp.maximum(hid, 0.0)).astype(BF16)
            part = jnp.dot(hid, w2_ref[cols, :], preferred_element_type=F32)
            ff[i] = part if ff[i] is None else ff[i] + part
    for i, rows in enumerate(subs):
        h2 = h1[i] + _rms(ff[i], gqf_ref[...])
        gate = jax.nn.sigmoid(
            jnp.dot(h2.astype(BF16), wg_ref[...], preferred_element_type=F32) + bg_ref[...])
        pe = jnp.dot(p_ref[rows, :].astype(BF16), wpe_ref[...], preferred_element_type=F32)
        o_ref[rows, :] = h2 + gate * pe


def _layer(h, p, w_in, f_bias, sg_ln_g, sg_ln_b, sg_w, sg_b, att_out_g, sg_out_g, w_out,
           pre_mix_g, post_mix_g, pre_ffn_g, post_ffn_g, w_ff1, w_ff2, ple_w, ple_gate_w, ple_gate_b):
    B, S, D = h.shape
    R = B * S
    row = lambda a: a.reshape(1, -1).astype(F32)

    wqkv = w_in[:, :3 * ATT_WIDTH].astype(BF16)
    wf8 = w_in[:, 3 * ATT_WIDTH:3 * ATT_WIDTH + N_HEADS]
    n_rep = N_PIECES * N_PIECES
    wf = jnp.pad(jnp.tile(wf8, (1, n_rep)), ((0, 0), (0, LANES - n_rep * N_HEADS))).astype(BF16)
    fb = jnp.pad(jnp.tile(f_bias, n_rep), (0, LANES - n_rep * N_HEADS)).reshape(1, LANES).astype(F32)
    tril = jnp.tril(jnp.ones((T_SUB, T_SUB), BF16))
    wuv = w_in[:, 3 * ATT_WIDTH + N_HEADS:].astype(BF16)
    sgb = jnp.repeat(jnp.transpose(sg_b), HEAD_DIM, axis=1).astype(F32)

    const2 = lambda shape: pl.BlockSpec(shape, lambda b, t: (0,) * len(shape))
    n_t = S // T_IN
    slab = lambda w: pl.BlockSpec((w.shape[0] // (B * n_t), w.shape[1]), lambda b, t: (b * n_t + t, 0))
    cast_ws = (w_ff1, w_ff2, ple_gate_w, w_out)
    qp, kp, vp, ysg, w1b, w2b, wgb, wob = pl.pallas_call(
        _inproj_kernel,
        grid=(B, S // T_IN),
        in_specs=[
            pl.BlockSpec((1, T_IN, D), lambda b, t: (b, t, 0)),
            const2((1, D)),
            const2((D, 3 * ATT_WIDTH)),
            const2((D, LANES)),
            const2((1, LANES)),
            const2((T_SUB, T_SUB)),
            const2((D, 2 * SG_WIDTH)),
            const2((1, SG_WIDTH)),
            const2((1, SG_WIDTH)),
            const2((N_SG_GROUPS, CHUNK, CHUNK)),
            const2((CHUNK, SG_WIDTH)),
            const2((1, SG_WIDTH)),
        ] + [slab(w) for w in cast_ws],
        out_specs=[
            pl.BlockSpec((1, N_HEADS, T_IN, LANES), lambda b, t: (b, 0, t, 0)),
            pl.BlockSpec((1, N_HEADS, T_IN, LANES), lambda b, t: (b, 0, t, 0)),
            pl.BlockSpec((1, N_HEADS, T_IN, LANES), lambda b, t: (b, 0, t, 0)),
            pl.BlockSpec((1, T_IN, SG_WIDTH), lambda b, t: (b, t, 0)),
        ] + [slab(w) for w in cast_ws],
        out_shape=[
            jax.ShapeDtypeStruct((B, N_HEADS, S, LANES), BF16),
            jax.ShapeDtypeStruct((B, N_HEADS, S, LANES), BF16),
            jax.ShapeDtypeStruct((B, N_HEADS, S, LANES), BF16),
            jax.ShapeDtypeStruct((B, S, SG_WIDTH), BF16),
        ] + [jax.ShapeDtypeStruct(w.shape, BF16) for w in cast_ws],
        scratch_shapes=[pltpu.VMEM((1, LANES), F32), pltpu.VMEM((T_IN, SG_WIDTH), F32)],
        compiler_params=pltpu.CompilerParams(
            dimension_semantics=("arbitrary", "arbitrary"), vmem_limit_bytes=VMEM_LIMIT),
        name="inproj",
    )(h, row(pre_mix_g), wqkv, wf, fb, tril, wuv, row(sg_ln_g), row(sg_ln_b), sg_w.astype(F32), sgb,
      row(sg_out_g), *cast_ws)

    yatt = pl.pallas_call(
        _attn_kernel,
        grid=(B, HEAD_PAIRS, S // T_Q),
        in_specs=[
            pl.BlockSpec((1, 2, T_Q, LANES), lambda b, hp, i: (b, hp, i, 0)),
            pl.BlockSpec((1, 2, S, LANES), lambda b, hp, i: (b, hp, 0, 0)),
            pl.BlockSpec((1, 2, S, LANES), lambda b, hp, i: (b, hp, 0, 0)),
        ],
        out_specs=pl.BlockSpec((1, T_Q, LANES), lambda b, hp, i: (b, i, hp)),
        out_shape=jax.ShapeDtypeStruct((B, S, ATT_WIDTH), F32),
        scratch_shapes=[pltpu.VMEM((2, T_Q, LANES), F32), pltpu.VMEM((2, T_Q, LANES), F32),
                        pltpu.VMEM((2, 2, T_Q, T_K), F32), pltpu.VMEM((2, 2, T_Q, LANES), F32)],
        compiler_params=pltpu.CompilerParams(
            dimension_semantics=("arbitrary", "arbitrary", "arbitrary"), vmem_limit_bytes=VMEM_LIMIT),
        name="fox_attn",
    )(qp, kp, vp)

    rows = lambda w: pl.BlockSpec((T_MLP, w), lambda i: (i, 0))
    const = lambda shape: pl.BlockSpec(shape, lambda i: (0,) * len(shape), pipeline_mode=pl.Buffered(1))
    out = pl.pallas_call(
        _mlp_kernel,
        grid=(R // T_MLP,),
        in_specs=[
            rows(D), rows(ATT_WIDTH), rows(SG_WIDTH), rows(PLE_DIM),
            const((ATT_WIDTH + SG_WIDTH, D)),
            const((D, D_FF)), const((D_FF, D)),
            const((D, D)), const((PLE_DIM, D)),
            const((1, ATT_WIDTH)), const((1, D)), const((1, D)), const((1, D)), const((1, D)),
        ],
        out_specs=rows(D),
        out_shape=jax.ShapeDtypeStruct((R, D), F32),
        compiler_params=pltpu.CompilerParams(
            dimension_semantics=("arbitrary",), vmem_limit_bytes=VMEM_LIMIT),
        name="mlp",
    )(h.reshape(R, D), yatt.reshape(R, ATT_WIDTH), ysg.reshape(R, SG_WIDTH), p.reshape(R, PLE_DIM),
      wob, w1b, w2b, wgb, ple_w.astype(BF16),
      row(att_out_g), row(post_mix_g), row(pre_ffn_g), row(post_ffn_g), row(ple_gate_b))
    return out.reshape(B, S, D)


def kernel(x, p, w_in, f_bias, sg_ln_g, sg_ln_b, sg_w, sg_b, att_out_g, sg_out_g, w_out,
           pre_mix_g, post_mix_g, pre_ffn_g, post_ffn_g, w_ff1, w_ff2, ple_w, ple_gate_w, ple_gate_b):
    h = x
    for i in range(p.shape[0]):
        h = _layer(h, p[i], w_in[i], f_bias[i], sg_ln_g[i], sg_ln_b[i], sg_w[i], sg_b[i],
                   att_out_g[i], sg_out_g[i], w_out[i], pre_mix_g[i], post_mix_g[i],
                   pre_ffn_g[i], post_ffn_g[i], w_ff1[i], w_ff2[i], ple_w[i], ple_gate_w[i],
                   ple_gate_b[i])
    return h
```

```python
import jax
import jax.numpy as jnp
from jax import lax
from jax.experimental import pallas as pl
from jax.experimental.pallas import tpu as pltpu

D_MODEL = 1024
PLE_DIM = 256
HEAD_DIM = 64
ATT_WIDTH = 512
N_HEADS = 8
SG_WIDTH = 512
N_SG_GROUPS = 8
CHUNK = 128
D_FF = 4096
EPS = 1e-6

LANES = 128
HEAD_PAIRS = N_HEADS // 2
NEG_BIG = -1e30
LOG2E = 1.4426950408889634
N_PIECES = 3
PIECE_LANES = N_PIECES * N_HEADS
BIAS_LANES = 2 * PIECE_LANES

T_IN = 512
T_SUB = 256
T_Q = 2048
T_K = 512
STAIRS = T_Q // T_K
ATT_UNROLL = 2
assert ATT_UNROLL == 2 and STAIRS % 2 == 0
T_MLP = 512
T_MLP_SUB = 256
FF_CHUNK = 1024

VMEM_LIMIT = 56 * 1024 * 1024

F32 = jnp.float32
BF16 = jnp.bfloat16


def _rms(x, g):
    return x * lax.rsqrt(jnp.mean(x * x, axis=-1, keepdims=True) + EPS) * g


def _gelu_tanh(x):
    return x * (0.5 * (1.0 + jnp.tanh(0.7978845608028654 * (x + 0.044715 * (x * x * x)))))


def _split3(x):
    hi = x.astype(BF16).astype(F32)
    r = x - hi
    mid = r.astype(BF16).astype(F32)
    lo = (r - mid).astype(BF16).astype(F32)
    return hi, mid, lo


def _inproj_kernel(x_ref, g_ref, wqkv32_ref, wf_ref, fb_ref, tril_ref, wuv32_ref, lng_ref, lnb_ref,
                   sgw_ref, sgb_ref, sgog_ref, w1_ref, w2_ref, wg_ref, wo_ref,
                   qp_ref, kp_ref, vp_ref, ysg_ref, w1b_ref, w2b_ref, wgb_ref, wob_ref,
                   carry_ref, ysg_scr, wqkv_ref, wuv_ref):
    t = pl.program_id(1)

    @pl.when((pl.program_id(0) == 0) & (t == 0))
    def _():
        wqkv_ref[...] = wqkv32_ref[...].astype(BF16)
        wuv_ref[...] = wuv32_ref[...].astype(BF16)

    @pl.when(t == 0)
    def _():
        carry_ref[...] = jnp.zeros_like(carry_ref)

    w1b_ref[...] = w1_ref[...].astype(BF16)
    w2b_ref[...] = w2_ref[...].astype(BF16)
    wgb_ref[...] = wg_ref[...].astype(BF16)
    wob_ref[...] = wo_ref[...].astype(BF16)

    subs = [slice(r, r + T_SUB) for r in range(0, T_IN, T_SUB)]
    lane = lax.broadcasted_iota(jnp.int32, (T_SUB, LANES), 1)
    ab = [_rms(x_ref[0, rows, :], g_ref[...]).astype(BF16) for rows in subs]
    zuv = [jnp.dot(a, wuv_ref[...], preferred_element_type=F32) for a in ab]
    f = [jnp.dot(a, wf_ref[...], preferred_element_type=F32) + fb_ref[...] for a in ab]

    c3 = []
    carry = carry_ref[...]
    for fi in f:
        logf = jnp.minimum(fi, 0.0) - jnp.log1p(jnp.exp(-jnp.abs(fi)))
        hi, mid, lo = _split3(logf)
        pieces = jnp.where(lane < PIECE_LANES, hi, jnp.where(lane < 2 * PIECE_LANES, mid, lo)).astype(BF16)
        c3.append(jnp.dot(tril_ref[...], pieces, preferred_element_type=F32) + carry)
        carry = c3[-1][T_SUB - 1:T_SUB, :]
    carry_ref[...] = carry

    zqkv = []
    for rows, a, zuv_i in zip(subs, ab, zuv):
        zqkv.append(jnp.dot(a, wqkv_ref[...], preferred_element_type=F32))
        _spatial_gating(rows, zuv_i, lng_ref, lnb_ref, sgw_ref, sgb_ref, sgog_ref, ysg_ref, ysg_scr)
    for rows, zqkv_i, c3_i in zip(subs, zqkv, c3):
        _pack_heads(rows, lane, zqkv_i, c3_i, qp_ref, kp_ref, vp_ref)


def _spatial_gating(rows, zuv, lng_ref, lnb_ref, sgw_ref, sgb_ref, sgog_ref, ysg_ref, ysg_scr):
    r0 = rows.start
    u = _gelu_tanh(zuv[:, :SG_WIDTH])
    v = _gelu_tanh(zuv[:, SG_WIDTH:])
    vc = v - jnp.mean(v, axis=-1, keepdims=True)
    vln = vc * lax.rsqrt(jnp.mean(vc * vc, axis=-1, keepdims=True) + EPS) * lng_ref[...] + lnb_ref[...]
    vb = vln.astype(BF16)
    n_chunks = T_SUB // CHUNK
    ci = lax.broadcasted_iota(jnp.int32, (CHUNK, CHUNK), 0)
    cj = lax.broadcasted_iota(jnp.int32, (CHUNK, CHUNK), 1)
    lane_c = lax.broadcasted_iota(jnp.int32, (CHUNK, LANES), 1)
    for pb in range(SG_WIDTH // LANES):
        vcat = jnp.concatenate(
            [vb[c * CHUNK:(c + 1) * CHUNK, pb * LANES:(pb + 1) * LANES] for c in range(n_chunks)], axis=1)
        res = []
        for g in (2 * pb, 2 * pb + 1):
            w = jnp.where(cj <= ci, sgw_ref[g], 0.0).astype(BF16)
            res.append(jnp.dot(w, vcat, preferred_element_type=F32))
        for c in range(n_chunks):
            mixed = jnp.where(lane_c < HEAD_DIM,
                              res[0][:, c * LANES:(c + 1) * LANES],
                              res[1][:, c * LANES:(c + 1) * LANES])
            mixed = mixed + sgb_ref[:, pb * LANES:(pb + 1) * LANES]
            ysg_scr[r0 + c * CHUNK:r0 + (c + 1) * CHUNK, pb * LANES:(pb + 1) * LANES] = (
                u[c * CHUNK:(c + 1) * CHUNK, pb * LANES:(pb + 1) * LANES] * mixed)
    ysg_ref[0, rows, :] = _rms(ysg_scr[rows, :], sgog_ref[...]).astype(BF16)


def _pack_heads(rows, lane, zqkv, c3, qp_ref, kp_ref, vp_ref):
    csum = (c3 + pltpu.roll(c3, LANES - PIECE_LANES, axis=1)
            + pltpu.roll(c3, LANES - 2 * PIECE_LANES, axis=1))
    chi, cmid, clo = _split3(csum * LOG2E)
    cp = jnp.where(lane < N_HEADS, chi, jnp.where(lane < 2 * N_HEADS, cmid, clo))
    q_bias_lo = jnp.where(lane < PIECE_LANES, cp, jnp.where(lane < BIAS_LANES, 1.0, 0.0))
    k_bias_lo = jnp.where(lane < PIECE_LANES, 1.0,
                          jnp.where(lane < BIAS_LANES, -pltpu.roll(cp, PIECE_LANES, axis=1), 0.0))
    q_bias_hi = pltpu.roll(q_bias_lo, HEAD_DIM, axis=1)
    k_bias_hi = pltpu.roll(k_bias_lo, HEAD_DIM, axis=1)

    for h in range(N_HEADS):
        blk = h // 2
        even = (h % 2 == 0)
        in_head = (lane < HEAD_DIM) if even else (lane >= HEAD_DIM)
        own_bias = ((lane & (N_HEADS - 1)) == h) & ((lane & (HEAD_DIM - 1)) < BIAS_LANES)
        zq = zqkv[:, blk * LANES:(blk + 1) * LANES] * (HEAD_DIM ** -0.5 * LOG2E)
        zk = zqkv[:, ATT_WIDTH + blk * LANES:ATT_WIDTH + (blk + 1) * LANES]
        zv = zqkv[:, 2 * ATT_WIDTH + blk * LANES:2 * ATT_WIDTH + (blk + 1) * LANES]
        v_extra = jnp.where(lane == (HEAD_DIM if even else 0), 1.0, 0.0)
        qp_ref[0, h, rows, :] = jnp.where(in_head, zq, q_bias_hi if even else q_bias_lo).astype(BF16)
        kp_ref[0, h, rows, :] = jnp.where(
            in_head, zk, jnp.where(own_bias, k_bias_hi if even else k_bias_lo, 0.0)).astype(BF16)
        vp_ref[0, h, rows, :] = jnp.where(in_head, zv, v_extra).astype(BF16)


def _attn_kernel(q_ref, k_ref, v_ref, o_ref, acc_ref, m_ref, s_scr, mc_scr):
    n_full = pl.program_id(2) * STAIRS

    def scores(b, slot, stair=None):
        row0 = 0 if stair is None else stair * T_K
        start = pl.multiple_of(b * T_K, T_K)
        for hh in range(2):
            q = q_ref[0, hh, row0:, :]
            kb = k_ref[0, hh, pl.ds(start, T_K), :]
            s = lax.dot_general(q, kb, (((1,), (1,)), ((), ())), preferred_element_type=F32)
            parts = [(row0, T_Q, s)]
            if stair is not None:
                r = lax.broadcasted_iota(jnp.int32, (T_K, T_K), 0)
                c = lax.broadcasted_iota(jnp.int32, (T_K, T_K), 1)
                parts = [(row0, row0 + T_K, jnp.where(c <= r, s[:T_K], NEG_BIG))]
                if row0 + T_K < T_Q:
                    parts.append((row0 + T_K, T_Q, s[T_K:]))
            for lo, hi, sp in parts:
                s_scr[slot, hh, lo:hi, :] = sp
                mc_scr[slot, hh, lo:hi, :] = jnp.broadcast_to(
                    jnp.max(sp, axis=-1, keepdims=True), (hi - lo, LANES))

    def update(b, slot, row0=0, first=False):
        start = pl.multiple_of(b * T_K, T_K)
        for hh in range(2):
            vb = v_ref[0, hh, pl.ds(start, T_K), :]
            m_blk = mc_scr[slot, hh, row0:, :]
            if first:
                m_new = m_blk
            else:
                m_old = m_ref[hh, row0:, :]
                m_new = jnp.maximum(m_old, m_blk)
            m_wide = jnp.concatenate([m_new] * (T_K // LANES), axis=1)
            p = jnp.exp2(s_scr[slot, hh, row0:, :] - m_wide).astype(BF16)
            pv = jnp.dot(p, vb, preferred_element_type=F32)
            if first:
                acc_ref[hh] = pv
            else:
                acc_ref[hh, row0:, :] = jnp.exp2(m_old - m_new) * acc_ref[hh, row0:, :] + pv
            m_ref[hh, row0:, :] = m_new

    def staircase(first_blk, is_tile_start):
        for k in range(STAIRS - 1):
            scores(first_blk + k + 1, (k + 1) % 2, stair=k + 1)
            update(first_blk + k, k % 2, row0=k * T_K, first=is_tile_start and k == 0)
        update(first_blk + STAIRS - 1, (STAIRS - 1) % 2, row0=(STAIRS - 1) * T_K)

    @pl.when(n_full == 0)
    def _():
        scores(0, 0, stair=0)
        staircase(0, True)

    @pl.when(n_full > 0)
    def _():
        scores(0, 0)
        scores(1, 1)
        update(0, 0, first=True)

        def body(i, carry):
            b = 1 + ATT_UNROLL * i
            scores(b + 1, 0)
            update(b, 1)
            scores(b + 2, 1)
            update(b + 1, 0)
            return carry

        lax.fori_loop(0, (n_full - 2) // ATT_UNROLL, body, 0)
        scores(n_full, 0, stair=0)
        update(n_full - 1, 1)
        staircase(n_full, False)

    a0 = acc_ref[0]
    a1 = acc_ref[1]
    lane = lax.broadcasted_iota(jnp.int32, (T_Q, LANES), 1)
    o_ref[0] = jnp.where(lane < HEAD_DIM, a0 / a0[:, HEAD_DIM:HEAD_DIM + 1], a1 / a1[:, 0:1])


def _mlp_kernel(x_ref, ya_ref, ys_ref, p_ref, wo_ref, w1_ref, w2_ref, wg_ref, wpe_ref,
                gatt_ref, gpm_ref, gpf_ref, gqf_ref, bg_ref, o_ref):
    subs = [slice(r, r + T_MLP_SUB) for r in range(0, T_MLP, T_MLP_SUB)]
    y = []
    for rows in subs:
        yan = _rms(ya_ref[rows, :], gatt_ref[...]).astype(BF16)
        y.append(jnp.dot(yan, wo_ref[:ATT_WIDTH, :], preferred_element_type=F32)
                 + jnp.dot(ys_ref[rows, :], wo_ref[ATT_WIDTH:, :], preferred_element_type=F32))
    h1 = [x_ref[rows, :] + _rms(yy, gpm_ref[...]) for rows, yy in zip(subs, y)]
    cn = [_rms(hh, gpf_ref[...]).astype(BF16) for hh in h1]
    ff = [None] * len(subs)
    for c in range(D_FF // FF_CHUNK):
        cols = slice(c * FF_CHUNK, (c + 1) * FF_CHUNK)
        for i in range(len(subs)):
            hid = jnp.dot(cn[i], w1_ref[:, cols], preferred_element_type=F32)
            hid = jnp.square(jnp.maximum(hid, 0.0)).astype(BF16)
            part = jnp.dot(hid, w2_ref[cols, :], preferred_element_type=F32)
            ff[i] = part if ff[i] is None else ff[i] + part
    for i, rows in enumerate(subs):
        h2 = h1[i] + _rms(ff[i], gqf_ref[...])
        gate = jax.nn.sigmoid(
            jnp.dot(h2.astype(BF16), wg_ref[...], preferred_element_type=F32) + bg_ref[...])
        pe = jnp.dot(p_ref[rows, :].astype(BF16), wpe_ref[...], preferred_element_type=F32)
        o_ref[rows, :] = h2 + gate * pe


def _layer(h, p, w_in, f_bias, sg_ln_g, sg_ln_b, sg_w, sg_b, att_out_g, sg_out_g, w_out,
           pre_mix_g, post_mix_g, pre_ffn_g, post_ffn_g, w_ff1, w_ff2, ple_w, ple_gate_w, ple_gate_b):
    B, S, D = h.shape
    R = B * S
    assert D == D_MODEL and w_in.shape == (D, 3 * ATT_WIDTH + N_HEADS + 2 * SG_WIDTH)
    assert S % T_Q == 0 and S % T_IN == 0 and R % T_MLP == 0 and D % (B * (S // T_IN)) == 0
    assert T_IN % T_SUB == 0 and T_SUB % CHUNK == 0 and T_MLP % T_MLP_SUB == 0 and D_FF % FF_CHUNK == 0
    row = lambda a: a.reshape(1, -1).astype(F32)

    wf8 = w_in[:, 3 * ATT_WIDTH:3 * ATT_WIDTH + N_HEADS]
    n_rep = N_PIECES * N_PIECES
    wf = jnp.pad(jnp.tile(wf8, (1, n_rep)), ((0, 0), (0, LANES - n_rep * N_HEADS))).astype(BF16)
    fb = jnp.pad(jnp.tile(f_bias, n_rep), (0, LANES - n_rep * N_HEADS)).reshape(1, LANES).astype(F32)
    tril = jnp.tril(jnp.ones((T_SUB, T_SUB), BF16))
    wuv = w_in[:, 3 * ATT_WIDTH + N_HEADS:]
    sgb = jnp.repeat(jnp.transpose(sg_b), HEAD_DIM, axis=1).astype(F32)

    const2 = lambda shape: pl.BlockSpec(shape, lambda b, t: (0,) * len(shape))
    once2 = lambda shape: pl.BlockSpec(shape, lambda b, t: (0,) * len(shape), pipeline_mode=pl.Buffered(1))
    n_t = S // T_IN
    slab = lambda w: pl.BlockSpec((w.shape[0] // (B * n_t), w.shape[1]), lambda b, t: (b * n_t + t, 0))
    cast_ws = (w_ff1, w_ff2, ple_gate_w, w_out)
    qp, kp, vp, ysg, w1b, w2b, wgb, wob = pl.pallas_call(
        _inproj_kernel,
        grid=(B, S // T_IN),
        in_specs=[
            pl.BlockSpec((1, T_IN, D), lambda b, t: (b, t, 0)),
            const2((1, D)),
            once2((D, 3 * ATT_WIDTH)),
            const2((D, LANES)),
            const2((1, LANES)),
            const2((T_SUB, T_SUB)),
            once2((D, 2 * SG_WIDTH)),
            const2((1, SG_WIDTH)),
            const2((1, SG_WIDTH)),
            const2((N_SG_GROUPS, CHUNK, CHUNK)),
            const2((CHUNK, SG_WIDTH)),
            const2((1, SG_WIDTH)),
        ] + [slab(w) for w in cast_ws],
        out_specs=[
            pl.BlockSpec((1, N_HEADS, T_IN, LANES), lambda b, t: (b, 0, t, 0)),
            pl.BlockSpec((1, N_HEADS, T_IN, LANES), lambda b, t: (b, 0, t, 0)),
            pl.BlockSpec((1, N_HEADS, T_IN, LANES), lambda b, t: (b, 0, t, 0)),
            pl.BlockSpec((1, T_IN, SG_WIDTH), lambda b, t: (b, t, 0)),
        ] + [slab(w) for w in cast_ws],
        out_shape=[
            jax.ShapeDtypeStruct((B, N_HEADS, S, LANES), BF16),
            jax.ShapeDtypeStruct((B, N_HEADS, S, LANES), BF16),
            jax.ShapeDtypeStruct((B, N_HEADS, S, LANES), BF16),
            jax.ShapeDtypeStruct((B, S, SG_WIDTH), BF16),
        ] + [jax.ShapeDtypeStruct(w.shape, BF16) for w in cast_ws],
        scratch_shapes=[pltpu.VMEM((1, LANES), F32), pltpu.VMEM((T_IN, SG_WIDTH), F32),
                        pltpu.VMEM((D, 3 * ATT_WIDTH), BF16), pltpu.VMEM((D, 2 * SG_WIDTH), BF16)],
        compiler_params=pltpu.CompilerParams(
            dimension_semantics=("arbitrary", "arbitrary"), vmem_limit_bytes=VMEM_LIMIT),
        name="inproj",
    )(h, row(pre_mix_g), w_in, wf, fb, tril, wuv, row(sg_ln_g), row(sg_ln_b), sg_w.astype(F32), sgb,
      row(sg_out_g), *cast_ws)

    yatt = pl.pallas_call(
        _attn_kernel,
        grid=(B, HEAD_PAIRS, S // T_Q),
        in_specs=[
            pl.BlockSpec((1, 2, T_Q, LANES), lambda b, hp, i: (b, hp, i, 0)),
            pl.BlockSpec((1, 2, S, LANES), lambda b, hp, i: (b, hp, 0, 0)),
            pl.BlockSpec((1, 2, S, LANES), lambda b, hp, i: (b, hp, 0, 0)),
        ],
        out_specs=pl.BlockSpec((1, T_Q, LANES), lambda b, hp, i: (b, i, hp)),
        out_shape=jax.ShapeDtypeStruct((B, S, ATT_WIDTH), F32),
        scratch_shapes=[pltpu.VMEM((2, T_Q, LANES), F32), pltpu.VMEM((2, T_Q, LANES), F32),
                        pltpu.VMEM((2, 2, T_Q, T_K), F32), pltpu.VMEM((2, 2, T_Q, LANES), F32)],
        compiler_params=pltpu.CompilerParams(
            dimension_semantics=("arbitrary", "arbitrary", "arbitrary"), vmem_limit_bytes=VMEM_LIMIT),
        name="fox_attn",
    )(qp, kp, vp)

    rows = lambda w: pl.BlockSpec((T_MLP, w), lambda i: (i, 0))
    const = lambda shape: pl.BlockSpec(shape, lambda i: (0,) * len(shape), pipeline_mode=pl.Buffered(1))
    out = pl.pallas_call(
        _mlp_kernel,
        grid=(R // T_MLP,),
        in_specs=[
            rows(D), rows(ATT_WIDTH), rows(SG_WIDTH), rows(PLE_DIM),
            const((ATT_WIDTH + SG_WIDTH, D)),
            const((D, D_FF)), const((D_FF, D)),
            const((D, D)), const((PLE_DIM, D)),
            const((1, ATT_WIDTH)), const((1, D)), const((1, D)), const((1, D)), const((1, D)),
        ],
        out_specs=rows(D),
        out_shape=jax.ShapeDtypeStruct((R, D), F32),
        compiler_params=pltpu.CompilerParams(
            dimension_semantics=("arbitrary",), vmem_limit_bytes=VMEM_LIMIT),
        name="mlp",
    )(h.reshape(R, D), yatt.reshape(R, ATT_WIDTH), ysg.reshape(R, SG_WIDTH), p.reshape(R, PLE_DIM),
      wob, w1b, w2b, wgb, ple_w.astype(BF16),
      row(att_out_g), row(post_mix_g), row(pre_ffn_g), row(post_ffn_g), row(ple_gate_b))
    return out.reshape(B, S, D)


def kernel(x, p, w_in, f_bias, sg_ln_g, sg_ln_b, sg_w, sg_b, att_out_g, sg_out_g, w_out,
           pre_mix_g, post_mix_g, pre_ffn_g, post_ffn_g, w_ff1, w_ff2, ple_w, ple_gate_w, ple_gate_b):
    h = x
    for i in range(p.shape[0]):
        h = _layer(h, p[i], w_in[i], f_bias[i], sg_ln_g[i], sg_ln_b[i], sg_w[i], sg_b[i],
                   att_out_g[i], sg_out_g[i], w_out[i], pre_mix_g[i], post_mix_g[i],
                   pre_ffn_g[i], post_ffn_g[i], w_ff1[i], w_ff2[i], ple_w[i], ple_gate_w[i],
                   ple_gate_b[i])
    return h
```

```python
import jax
import jax.numpy as jnp
from jax import lax
from jax.experimental import pallas as pl
from jax.experimental.pallas import tpu as pltpu

D_MODEL = 1024
PLE_DIM = 256
HEAD_DIM = 64
ATT_WIDTH = 512
N_HEADS = 8
SG_WIDTH = 512
N_SG_GROUPS = 8
CHUNK = 128
D_FF = 4096
EPS = 1e-6

LANES = 128
HEAD_PAIRS = N_HEADS // 2
NEG_BIG = -1e30
LOG2E = 1.4426950408889634
N_PIECES = 3
PIECE_LANES = N_PIECES * N_HEADS
BIAS_LANES = 2 * PIECE_LANES

T_IN = 512
T_SUB = 256
T_Q = 2048
T_K = 512
STAIRS = T_Q // T_K
ATT_UNROLL = 2
assert ATT_UNROLL == 2 and STAIRS % 2 == 0
T_MLP = 512
T_MLP_SUB = 256
FF_CHUNK = 1024

VMEM_LIMIT = 56 * 1024 * 1024

F32 = jnp.float32
BF16 = jnp.bfloat16


def _rms(x, g):
    return x * lax.rsqrt(jnp.mean(x * x, axis=-1, keepdims=True) + EPS) * g


def _gelu_tanh(x):
    return x * (0.5 * (1.0 + jnp.tanh(0.7978845608028654 * (x + 0.044715 * (x * x * x)))))


def _split3(x):
    hi = x.astype(BF16).astype(F32)
    r = x - hi
    mid = r.astype(BF16).astype(F32)
    lo = (r - mid).astype(BF16).astype(F32)
    return hi, mid, lo


def _inproj_kernel(x_ref, g_ref, wqkv_ref, wf_ref, fb_ref, tril_ref, wuv_ref, lng_ref, lnb_ref,
                   sgw_ref, sgb_ref, sgog_ref, w1_ref, w2_ref, wg_ref, wo_ref,
                   qp_ref, kp_ref, vp_ref, ysg_ref, w1b_ref, w2b_ref, wgb_ref, wob_ref,
                   carry_ref, ysg_scr):
    t = pl.program_id(1)

    @pl.when(t == 0)
    def _():
        carry_ref[...] = jnp.zeros_like(carry_ref)

    w1b_ref[...] = w1_ref[...].astype(BF16)
    w2b_ref[...] = w2_ref[...].astype(BF16)
    wgb_ref[...] = wg_ref[...].astype(BF16)
    wob_ref[...] = wo_ref[...].astype(BF16)

    subs = [slice(r, r + T_SUB) for r in range(0, T_IN, T_SUB)]
    lane = lax.broadcasted_iota(jnp.int32, (T_SUB, LANES), 1)
    ab = [_rms(x_ref[0, rows, :], g_ref[...]).astype(BF16) for rows in subs]
    zuv = [jnp.dot(a, wuv_ref[...], preferred_element_type=F32) for a in ab]
    f = [jnp.dot(a, wf_ref[...], preferred_element_type=F32) + fb_ref[...] for a in ab]

    c3 = []
    carry = carry_ref[...]
    for fi in f:
        logf = jnp.minimum(fi, 0.0) - jnp.log1p(jnp.exp(-jnp.abs(fi)))
        hi, mid, lo = _split3(logf)
        pieces = jnp.where(lane < PIECE_LANES, hi, jnp.where(lane < 2 * PIECE_LANES, mid, lo)).astype(BF16)
        c3.append(jnp.dot(tril_ref[...], pieces, preferred_element_type=F32) + carry)
        carry = c3[-1][T_SUB - 1:T_SUB, :]
    carry_ref[...] = carry

    zqkv = []
    for rows, a, zuv_i in zip(subs, ab, zuv):
        zqkv.append(jnp.dot(a, wqkv_ref[...], preferred_element_type=F32))
        _spatial_gating(rows, zuv_i, lng_ref, lnb_ref, sgw_ref, sgb_ref, sgog_ref, ysg_ref, ysg_scr)
    for rows, zqkv_i, c3_i in zip(subs, zqkv, c3):
        _pack_heads(rows, lane, zqkv_i, c3_i, qp_ref, kp_ref, vp_ref)


def _spatial_gating(rows, zuv, lng_ref, lnb_ref, sgw_ref, sgb_ref, sgog_ref, ysg_ref, ysg_scr):
    r0 = rows.start
    u = _gelu_tanh(zuv[:, :SG_WIDTH])
    v = _gelu_tanh(zuv[:, SG_WIDTH:])
    vc = v - jnp.mean(v, axis=-1, keepdims=True)
    vln = vc * lax.rsqrt(jnp.mean(vc * vc, axis=-1, keepdims=True) + EPS) * lng_ref[...] + lnb_ref[...]
    vb = vln.astype(BF16)
    n_chunks = T_SUB // CHUNK
    ci = lax.broadcasted_iota(jnp.int32, (CHUNK, CHUNK), 0)
    cj = lax.broadcasted_iota(jnp.int32, (CHUNK, CHUNK), 1)
    lane_c = lax.broadcasted_iota(jnp.int32, (CHUNK, LANES), 1)
    for pb in range(SG_WIDTH // LANES):
        vcat = jnp.concatenate(
            [vb[c * CHUNK:(c + 1) * CHUNK, pb * LANES:(pb + 1) * LANES] for c in range(n_chunks)], axis=1)
        res = []
        for g in (2 * pb, 2 * pb + 1):
            w = jnp.where(cj <= ci, sgw_ref[g], 0.0).astype(BF16)
            res.append(jnp.dot(w, vcat, preferred_element_type=F32))
        for c in range(n_chunks):
            mixed = jnp.where(lane_c < HEAD_DIM,
                              res[0][:, c * LANES:(c + 1) * LANES],
                              res[1][:, c * LANES:(c + 1) * LANES])
            mixed = mixed + sgb_ref[:, pb * LANES:(pb + 1) * LANES]
            ysg_scr[r0 + c * CHUNK:r0 + (c + 1) * CHUNK, pb * LANES:(pb + 1) * LANES] = (
                u[c * CHUNK:(c + 1) * CHUNK, pb * LANES:(pb + 1) * LANES] * mixed)
    ysg_ref[0, rows, :] = _rms(ysg_scr[rows, :], sgog_ref[...]).astype(BF16)


def _pack_heads(rows, lane, zqkv, c3, qp_ref, kp_ref, vp_ref):
    csum = (c3 + pltpu.roll(c3, LANES - PIECE_LANES, axis=1)
            + pltpu.roll(c3, LANES - 2 * PIECE_LANES, axis=1))
    chi, cmid, clo = _split3(csum * LOG2E)
    cp = jnp.where(lane < N_HEADS, chi, jnp.where(lane < 2 * N_HEADS, cmid, clo))
    q_bias_lo = jnp.where(lane < PIECE_LANES, cp, jnp.where(lane < BIAS_LANES, 1.0, 0.0))
    k_bias_lo = jnp.where(lane < PIECE_LANES, 1.0,
                          jnp.where(lane < BIAS_LANES, -pltpu.roll(cp, PIECE_LANES, axis=1), 0.0))
    q_bias_hi = pltpu.roll(q_bias_lo, HEAD_DIM, axis=1)
    k_bias_hi = pltpu.roll(k_bias_lo, HEAD_DIM, axis=1)

    for h in range(N_HEADS):
        blk = h // 2
        even = (h % 2 == 0)
        in_head = (lane < HEAD_DIM) if even else (lane >= HEAD_DIM)
        own_bias = ((lane & (N_HEADS - 1)) == h) & ((lane & (HEAD_DIM - 1)) < BIAS_LANES)
        zq = zqkv[:, blk * LANES:(blk + 1) * LANES] * (HEAD_DIM ** -0.5 * LOG2E)
        zk = zqkv[:, ATT_WIDTH + blk * LANES:ATT_WIDTH + (blk + 1) * LANES]
        zv = zqkv[:, 2 * ATT_WIDTH + blk * LANES:2 * ATT_WIDTH + (blk + 1) * LANES]
        qp_ref[0, h, rows, :] = jnp.where(in_head, zq, q_bias_hi if even else q_bias_lo).astype(BF16)
        kp_ref[0, h, rows, :] = jnp.where(
            in_head, zk, jnp.where(own_bias, k_bias_hi if even else k_bias_lo, 0.0)).astype(BF16)
        vp_ref[0, h, rows, :] = jnp.where(in_head, zv, 1.0).astype(BF16)


def _attn_kernel(q_ref, k_ref, v_ref, o_ref, acc_ref, m_ref, s_scr, mc_scr):
    n_full = pl.program_id(2) * STAIRS

    def scores(b, slot, stair=None):
        row0 = 0 if stair is None else stair * T_K
        start = pl.multiple_of(b * T_K, T_K)
        for hh in range(2):
            q = q_ref[0, hh, row0:, :]
            kb = k_ref[0, hh, pl.ds(start, T_K), :]
            s = lax.dot_general(q, kb, (((1,), (1,)), ((), ())), preferred_element_type=F32)
            parts = [(row0, T_Q, s)]
            if stair is not None:
                r = lax.broadcasted_iota(jnp.int32, (T_K, T_K), 0)
                c = lax.broadcasted_iota(jnp.int32, (T_K, T_K), 1)
                parts = [(row0, row0 + T_K, jnp.where(c <= r, s[:T_K], NEG_BIG))]
                if row0 + T_K < T_Q:
                    parts.append((row0 + T_K, T_Q, s[T_K:]))
            for lo, hi, sp in parts:
                s_scr[slot, hh, lo:hi, :] = sp
                mc_scr[slot, hh, lo:hi, :] = jnp.broadcast_to(
                    jnp.max(sp, axis=-1, keepdims=True), (hi - lo, LANES))

    def update(b, slot, row0=0, first=False):
        start = pl.multiple_of(b * T_K, T_K)
        for hh in range(2):
            vb = v_ref[0, hh, pl.ds(start, T_K), :]
            m_blk = mc_scr[slot, hh, row0:, :]
            if first:
                m_new = m_blk
            else:
                m_old = m_ref[hh, row0:, :]
                m_new = jnp.maximum(m_old, m_blk)
            m_wide = jnp.concatenate([m_new] * (T_K // LANES), axis=1)
            p = jnp.exp2(s_scr[slot, hh, row0:, :] - m_wide).astype(BF16)
            pv = jnp.dot(p, vb, preferred_element_type=F32)
            if first:
                acc_ref[hh] = pv
            else:
                acc_ref[hh, row0:, :] = jnp.exp2(m_old - m_new) * acc_ref[hh, row0:, :] + pv
            m_ref[hh, row0:, :] = m_new

    def staircase(first_blk, is_tile_start):
        for k in range(STAIRS - 1):
            scores(first_blk + k + 1, (k + 1) % 2, stair=k + 1)
            update(first_blk + k, k % 2, row0=k * T_K, first=is_tile_start and k == 0)
        update(first_blk + STAIRS - 1, (STAIRS - 1) % 2, row0=(STAIRS - 1) * T_K)

    @pl.when(n_full == 0)
    def _():
        scores(0, 0, stair=0)
        staircase(0, True)

    @pl.when(n_full > 0)
    def _():
        scores(0, 0)
        scores(1, 1)
        update(0, 0, first=True)

        def body(i, carry):
            b = 1 + ATT_UNROLL * i
            scores(b + 1, 0)
            update(b, 1)
            scores(b + 2, 1)
            update(b + 1, 0)
            return carry

        lax.fori_loop(0, (n_full - 2) // ATT_UNROLL, body, 0)
        scores(n_full, 0, stair=0)
        update(n_full - 1, 1)
        staircase(n_full, False)

    a0 = acc_ref[0]
    a1 = acc_ref[1]
    low = lax.broadcasted_iota(jnp.int32, (T_Q, LANES), 1) < HEAD_DIM
    numer = jnp.where(low, a0, a1)
    denom = pltpu.roll(jnp.where(low, a1, a0), HEAD_DIM, axis=1)
    o_ref[0] = numer / denom


def _mlp_kernel(x_ref, ya_ref, ys_ref, p_ref, wo_ref, w1_ref, w2_ref, wg_ref, wpe_ref,
                gatt_ref, gpm_ref, gpf_ref, gqf_ref, bg_ref, o_ref):
    subs = [slice(r, r + T_MLP_SUB) for r in range(0, T_MLP, T_MLP_SUB)]
    y = []
    for rows in subs:
        yan = _rms(ya_ref[rows, :], gatt_ref[...]).astype(BF16)
        y.append(jnp.dot(yan, wo_ref[:ATT_WIDTH, :], preferred_element_type=F32)
                 + jnp.dot(ys_ref[rows, :], wo_ref[ATT_WIDTH:, :], preferred_element_type=F32))
    h1 = [x_ref[rows, :] + _rms(yy, gpm_ref[...]) for rows, yy in zip(subs, y)]
    cn = [_rms(hh, gpf_ref[...]).astype(BF16) for hh in h1]
    ff = [None] * len(subs)
    for c in range(D_FF // FF_CHUNK):
        cols = slice(c * FF_CHUNK, (c + 1) * FF_CHUNK)
        for i in range(len(subs)):
            hid = jnp.dot(cn[i], w1_ref[:, cols], preferred_element_type=F32)
            hid = jnp.square(jnp.maximum(hid, 0.0)).astype(BF16)
            part = jnp.dot(hid, w2_ref[cols, :], preferred_element_type=F32)
            ff[i] = part if ff[i] is None else ff[i] + part
    for i, rows in enumerate(subs):
        h2 = h1[i] + _rms(ff[i], gqf_ref[...])
        gate = jax.nn.sigmoid(
            jnp.dot(h2.astype(BF16), wg_ref[...], preferred_element_type=F32) + bg_ref[...])
        pe = jnp.dot(p_ref[rows, :].astype(BF16), wpe_ref[...], preferred_element_type=F32)
        o_ref[rows, :] = h2 + gate * pe


def _layer(h, p, w_in, f_bias, sg_ln_g, sg_ln_b, sg_w, sg_b, att_out_g, sg_out_g, w_out,
           pre_mix_g, post_mix_g, pre_ffn_g, post_ffn_g, w_ff1, w_ff2, ple_w, ple_gate_w, ple_gate_b):
    B, S, D = h.shape
    R = B * S
    assert D == D_MODEL and w_in.shape == (D, 3 * ATT_WIDTH + N_HEADS + 2 * SG_WIDTH)
    assert S % T_Q == 0 and S % T_IN == 0 and R % T_MLP == 0 and D % (B * (S // T_IN)) == 0
    assert T_IN % T_SUB == 0 and T_SUB % CHUNK == 0 and T_MLP % T_MLP_SUB == 0 and D_FF % FF_CHUNK == 0
    row = lambda a: a.reshape(1, -1).astype(F32)

    wqkv = w_in[:, :3 * ATT_WIDTH].astype(BF16)
    wf8 = w_in[:, 3 * ATT_WIDTH:3 * ATT_WIDTH + N_HEADS]
    n_rep = N_PIECES * N_PIECES
    wf = jnp.pad(jnp.tile(wf8, (1, n_rep)), ((0, 0), (0, LANES - n_rep * N_HEADS))).astype(BF16)
    fb = jnp.pad(jnp.tile(f_bias, n_rep), (0, LANES - n_rep * N_HEADS)).reshape(1, LANES).astype(F32)
    tril = jnp.tril(jnp.ones((T_SUB, T_SUB), BF16))
    wuv = w_in[:, 3 * ATT_WIDTH + N_HEADS:].astype(BF16)
    sgb = jnp.repeat(jnp.transpose(sg_b), HEAD_DIM, axis=1).astype(F32)

    const2 = lambda shape: pl.BlockSpec(shape, lambda b, t: (0,) * len(shape))
    n_t = S // T_IN
    slab = lambda w: pl.BlockSpec((w.shape[0] // (B * n_t), w.shape[1]), lambda b, t: (b * n_t + t, 0))
    cast_ws = (w_ff1, w_ff2, ple_gate_w, w_out)
    qp, kp, vp, ysg, w1b, w2b, wgb, wob = pl.pallas_call(
        _inproj_kernel,
        grid=(B, S // T_IN),
        in_specs=[
            pl.BlockSpec((1, T_IN, D), lambda b, t: (b, t, 0)),
            const2((1, D)),
            const2((D, 3 * ATT_WIDTH)),
            const2((D, LANES)),
            const2((1, LANES)),
            const2((T_SUB, T_SUB)),
            const2((D, 2 * SG_WIDTH)),
            const2((1, SG_WIDTH)),
            const2((1, SG_WIDTH)),
            const2((N_SG_GROUPS, CHUNK, CHUNK)),
            const2((CHUNK, SG_WIDTH)),
            const2((1, SG_WIDTH)),
        ] + [slab(w) for w in cast_ws],
        out_specs=[
            pl.BlockSpec((1, N_HEADS, T_IN, LANES), lambda b, t: (b, 0, t, 0)),
            pl.BlockSpec((1, N_HEADS, T_IN, LANES), lambda b, t: (b, 0, t, 0)),
            pl.BlockSpec((1, N_HEADS, T_IN, LANES), lambda b, t: (b, 0, t, 0)),
            pl.BlockSpec((1, T_IN, SG_WIDTH), lambda b, t: (b, t, 0)),
        ] + [slab(w) for w in cast_ws],
        out_shape=[
            jax.ShapeDtypeStruct((B, N_HEADS, S, LANES), BF16),
            jax.ShapeDtypeStruct((B, N_HEADS, S, LANES), BF16),
            jax.ShapeDtypeStruct((B, N_HEADS, S, LANES), BF16),
            jax.ShapeDtypeStruct((B, S, SG_WIDTH), BF16),
        ] + [jax.ShapeDtypeStruct(w.shape, BF16) for w in cast_ws],
        scratch_shapes=[pltpu.VMEM((1, LANES), F32), pltpu.VMEM((T_IN, SG_WIDTH), F32)],
        compiler_params=pltpu.CompilerParams(
            dimension_semantics=("arbitrary", "arbitrary"), vmem_limit_bytes=VMEM_LIMIT),
        name="inproj",
    )(h, row(pre_mix_g), wqkv, wf, fb, tril, wuv, row(sg_ln_g), row(sg_ln_b), sg_w.astype(F32), sgb,
      row(sg_out_g), *cast_ws)

    yatt = pl.pallas_call(
        _attn_kernel,
        grid=(B, HEAD_PAIRS, S // T_Q),
        in_specs=[
            pl.BlockSpec((1, 2, T_Q, LANES), lambda b, hp, i: (b, hp, i, 0)),
            pl.BlockSpec((1, 2, S, LANES), lambda b, hp, i: (b, hp, 0, 0)),
            pl.BlockSpec((1, 2, S, LANES), lambda b, hp, i: (b, hp, 0, 0)),
        ],
        out_specs=pl.BlockSpec((1, T_Q, LANES), lambda b, hp, i: (b, i, hp)),
        out_shape=jax.ShapeDtypeStruct((B, S, ATT_WIDTH), F32),
        scratch_shapes=[pltpu.VMEM((2, T_Q, LANES), F32), pltpu.VMEM((2, T_Q, LANES), F32),
                        pltpu.VMEM((2, 2, T_Q, T_K), F32), pltpu.VMEM((2, 2, T_Q, LANES), F32)],
        compiler_params=pltpu.CompilerParams(
            dimension_semantics=("arbitrary", "arbitrary", "arbitrary"), vmem_limit_bytes=VMEM_LIMIT),
        name="fox_attn",
    )(qp, kp, vp)

    rows = lambda w: pl.BlockSpec((T_MLP, w), lambda i: (i, 0))
    const = lambda shape: pl.BlockSpec(shape, lambda i: (0,) * len(shape), pipeline_mode=pl.Buffered(1))
    out = pl.pallas_call(
        _mlp_kernel,
        grid=(R // T_MLP,),
        in_specs=[
            rows(D), rows(ATT_WIDTH), rows(SG_WIDTH), rows(PLE_DIM),
            const((ATT_WIDTH + SG_WIDTH, D)),
            const((D, D_FF)), const((D_FF, D)),
            const((D, D)), const((PLE_DIM, D)),
            const((1, ATT_WIDTH)), const((1, D)), const((1, D)), const((1, D)), const((1, D)),
        ],
        out_specs=rows(D),
        out_shape=jax.ShapeDtypeStruct((R, D), F32),
        compiler_params=pltpu.CompilerParams(
            dimension_semantics=("arbitrary",), vmem_limit_bytes=VMEM_LIMIT),
        name="mlp",
    )(h.reshape(R, D), yatt.reshape(R, ATT_WIDTH), ysg.reshape(R, SG_WIDTH), p.reshape(R, PLE_DIM),
      wob, w1b, w2b, wgb, ple_w.astype(BF16),
      row(att_out_g), row(post_mix_g), row(pre_ffn_g), row(post_ffn_g), row(ple_gate_b))
    return out.reshape(B, S, D)


def kernel(x, p, w_in, f_bias, sg_ln_g, sg_ln_b, sg_w, sg_b, att_out_g, sg_out_g, w_out,
           pre_mix_g, post_mix_g, pre_ffn_g, post_ffn_g, w_ff1, w_ff2, ple_w, ple_gate_w, ple_gate_b):
    h = x
    for i in range(p.shape[0]):
        h = _layer(h, p[i], w_in[i], f_bias[i], sg_ln_g[i], sg_ln_b[i], sg_w[i], sg_b[i],
                   att_out_g[i], sg_out_g[i], w_out[i], pre_mix_g[i], post_mix_g[i],
                   pre_ffn_g[i], post_ffn_g[i], w_ff1[i], w_ff2[i], ple_w[i], ple_gate_w[i],
                   ple_gate_b[i])
    return h
```

```python
import jax
import jax.numpy as jnp
from jax import lax
from jax.experimental import pallas as pl
from jax.experimental.pallas import tpu as pltpu

D_MODEL = 1024
PLE_DIM = 256
HEAD_DIM = 64
ATT_WIDTH = 512
N_HEADS = 8
SG_WIDTH = 512
N_SG_GROUPS = 8
CHUNK = 128
D_FF = 4096
EPS = 1e-6

LANES = 128
HEAD_PAIRS = N_HEADS // 2
NEG_BIG = -1e30
LOG2E = 1.4426950408889634
N_PIECES = 3
PIECE_LANES = N_PIECES * N_HEADS
BIAS_LANES = 2 * PIECE_LANES

T_IN = 1024
T_SUB = 256
T_Q = 2048
T_K = 512
STAIRS = T_Q // T_K
ATT_UNROLL = 2
assert ATT_UNROLL == 2 and STAIRS % 2 == 0
T_MLP = 512
T_MLP_SUB = 256
FF_CHUNK = 1024

VMEM_LIMIT = 56 * 1024 * 1024

F32 = jnp.float32
BF16 = jnp.bfloat16


def _rms(x, g):
    return x * lax.rsqrt(jnp.mean(x * x, axis=-1, keepdims=True) + EPS) * g


def _gelu_tanh(x):
    return x * (0.5 * (1.0 + jnp.tanh(0.7978845608028654 * (x + 0.044715 * (x * x * x)))))


def _split3(x):
    hi = x.astype(BF16).astype(F32)
    r = x - hi
    mid = r.astype(BF16).astype(F32)
    lo = (r - mid).astype(BF16).astype(F32)
    return hi, mid, lo


def _inproj_kernel(x_ref, g_ref, wqkv_ref, wf_ref, fb_ref, tril_ref, wuv_ref, lng_ref, lnb_ref,
                   sgw_ref, sgb_ref, sgog_ref, w1_ref, w2_ref, wg_ref, wo_ref,
                   qp_ref, kp_ref, vp_ref, ysg_ref, w1b_ref, w2b_ref, wgb_ref, wob_ref,
                   carry_ref, ysg_scr):
    t = pl.program_id(1)

    @pl.when(t == 0)
    def _():
        carry_ref[...] = jnp.zeros_like(carry_ref)

    w1b_ref[...] = w1_ref[...].astype(BF16)
    w2b_ref[...] = w2_ref[...].astype(BF16)
    wgb_ref[...] = wg_ref[...].astype(BF16)
    wob_ref[...] = wo_ref[...].astype(BF16)

    subs = [slice(r, r + T_SUB) for r in range(0, T_IN, T_SUB)]
    lane = lax.broadcasted_iota(jnp.int32, (T_SUB, LANES), 1)
    ab = [_rms(x_ref[0, rows, :], g_ref[...]).astype(BF16) for rows in subs]
    zuv = [jnp.dot(a, wuv_ref[...], preferred_element_type=F32) for a in ab]
    f = [jnp.dot(a, wf_ref[...], preferred_element_type=F32) + fb_ref[...] for a in ab]

    c3 = []
    carry = carry_ref[...]
    for fi in f:
        logf = jnp.minimum(fi, 0.0) - jnp.log1p(jnp.exp(-jnp.abs(fi)))
        hi, mid, lo = _split3(logf)
        pieces = jnp.where(lane < PIECE_LANES, hi, jnp.where(lane < 2 * PIECE_LANES, mid, lo)).astype(BF16)
        c3.append(jnp.dot(tril_ref[...], pieces, preferred_element_type=F32) + carry)
        carry = c3[-1][T_SUB - 1:T_SUB, :]
    carry_ref[...] = carry

    zqkv = []
    for rows, a, zuv_i in zip(subs, ab, zuv):
        zqkv.append(jnp.dot(a, wqkv_ref[...], preferred_element_type=F32))
        _spatial_gating(rows, zuv_i, lng_ref, lnb_ref, sgw_ref, sgb_ref, sgog_ref, ysg_ref, ysg_scr)
    for rows, zqkv_i, c3_i in zip(subs, zqkv, c3):
        _pack_heads(rows, lane, zqkv_i, c3_i, qp_ref, kp_ref, vp_ref)


def _spatial_gating(rows, zuv, lng_ref, lnb_ref, sgw_ref, sgb_ref, sgog_ref, ysg_ref, ysg_scr):
    r0 = rows.start
    u = _gelu_tanh(zuv[:, :SG_WIDTH])
    v = _gelu_tanh(zuv[:, SG_WIDTH:])
    vc = v - jnp.mean(v, axis=-1, keepdims=True)
    vln = vc * lax.rsqrt(jnp.mean(vc * vc, axis=-1, keepdims=True) + EPS) * lng_ref[...] + lnb_ref[...]
    vb = vln.astype(BF16)
    n_chunks = T_SUB // CHUNK
    ci = lax.broadcasted_iota(jnp.int32, (CHUNK, CHUNK), 0)
    cj = lax.broadcasted_iota(jnp.int32, (CHUNK, CHUNK), 1)
    lane_c = lax.broadcasted_iota(jnp.int32, (CHUNK, LANES), 1)
    for pb in range(SG_WIDTH // LANES):
        vcat = jnp.concatenate(
            [vb[c * CHUNK:(c + 1) * CHUNK, pb * LANES:(pb + 1) * LANES] for c in range(n_chunks)], axis=1)
        res = []
        for g in (2 * pb, 2 * pb + 1):
            w = jnp.where(cj <= ci, sgw_ref[g], 0.0).astype(BF16)
            res.append(jnp.dot(w, vcat, preferred_element_type=F32))
        for c in range(n_chunks):
            mixed = jnp.where(lane_c < HEAD_DIM,
                              res[0][:, c * LANES:(c + 1) * LANES],
                              res[1][:, c * LANES:(c + 1) * LANES])
            mixed = mixed + sgb_ref[:, pb * LANES:(pb + 1) * LANES]
            ysg_scr[r0 + c * CHUNK:r0 + (c + 1) * CHUNK, pb * LANES:(pb + 1) * LANES] = (
                u[c * CHUNK:(c + 1) * CHUNK, pb * LANES:(pb + 1) * LANES] * mixed)
    ysg_ref[0, rows, :] = _rms(ysg_scr[rows, :], sgog_ref[...]).astype(BF16)


def _pack_heads(rows, lane, zqkv, c3, qp_ref, kp_ref, vp_ref):
    csum = (c3 + pltpu.roll(c3, LANES - PIECE_LANES, axis=1)
            + pltpu.roll(c3, LANES - 2 * PIECE_LANES, axis=1))
    chi, cmid, clo = _split3(csum * LOG2E)
    cp = jnp.where(lane < N_HEADS, chi, jnp.where(lane < 2 * N_HEADS, cmid, clo))
    q_bias_lo = jnp.where(lane < PIECE_LANES, cp, jnp.where(lane < BIAS_LANES, 1.0, 0.0))
    k_bias_lo = jnp.where(lane < PIECE_LANES, 1.0,
                          jnp.where(lane < BIAS_LANES, -pltpu.roll(cp, PIECE_LANES, axis=1), 0.0))
    q_bias_hi = pltpu.roll(q_bias_lo, HEAD_DIM, axis=1)
    k_bias_hi = pltpu.roll(k_bias_lo, HEAD_DIM, axis=1)

    for h in range(N_HEADS):
        blk = h // 2
        even = (h % 2 == 0)
        in_head = (lane < HEAD_DIM) if even else (lane >= HEAD_DIM)
        own_bias = ((lane & (N_HEADS - 1)) == h) & ((lane & (HEAD_DIM - 1)) < BIAS_LANES)
        zq = zqkv[:, blk * LANES:(blk + 1) * LANES] * (HEAD_DIM ** -0.5 * LOG2E)
        zk = zqkv[:, ATT_WIDTH + blk * LANES:ATT_WIDTH + (blk + 1) * LANES]
        zv = zqkv[:, 2 * ATT_WIDTH + blk * LANES:2 * ATT_WIDTH + (blk + 1) * LANES]
        qp_ref[0, h, rows, :] = jnp.where(in_head, zq, q_bias_hi if even else q_bias_lo).astype(BF16)
        kp_ref[0, h, rows, :] = jnp.where(
            in_head, zk, jnp.where(own_bias, k_bias_hi if even else k_bias_lo, 0.0)).astype(BF16)
        vp_ref[0, h, rows, :] = jnp.where(in_head, zv, 1.0).astype(BF16)


def _attn_kernel(q_ref, k_ref, v_ref, o_ref, acc_ref, m_ref, s_scr, mc_scr):
    n_full = pl.program_id(2) * STAIRS

    def scores(b, slot, stair=None):
        row0 = 0 if stair is None else stair * T_K
        start = pl.multiple_of(b * T_K, T_K)
        for hh in range(2):
            q = q_ref[0, hh, row0:, :]
            kb = k_ref[0, hh, pl.ds(start, T_K), :]
            s = lax.dot_general(q, kb, (((1,), (1,)), ((), ())), preferred_element_type=F32)
            parts = [(row0, T_Q, s)]
            if stair is not None:
                r = lax.broadcasted_iota(jnp.int32, (T_K, T_K), 0)
                c = lax.broadcasted_iota(jnp.int32, (T_K, T_K), 1)
                parts = [(row0, row0 + T_K, jnp.where(c <= r, s[:T_K], NEG_BIG))]
                if row0 + T_K < T_Q:
                    parts.append((row0 + T_K, T_Q, s[T_K:]))
            for lo, hi, sp in parts:
                s_scr[slot, hh, lo:hi, :] = sp
                mc_scr[slot, hh, lo:hi, :] = jnp.broadcast_to(
                    jnp.max(sp, axis=-1, keepdims=True), (hi - lo, LANES))

    def update(b, slot, row0=0, first=False):
        start = pl.multiple_of(b * T_K, T_K)
        for hh in range(2):
            vb = v_ref[0, hh, pl.ds(start, T_K), :]
            m_blk = mc_scr[slot, hh, row0:, :]
            if first:
                m_new = m_blk
            else:
                m_old = m_ref[hh, row0:, :]
                m_new = jnp.maximum(m_old, m_blk)
            m_wide = jnp.concatenate([m_new] * (T_K // LANES), axis=1)
            p = jnp.exp2(s_scr[slot, hh, row0:, :] - m_wide).astype(BF16)
            pv = jnp.dot(p, vb, preferred_element_type=F32)
            if first:
                acc_ref[hh] = pv
            else:
                acc_ref[hh, row0:, :] = jnp.exp2(m_old - m_new) * acc_ref[hh, row0:, :] + pv
            m_ref[hh, row0:, :] = m_new

    def staircase(first_blk, is_tile_start):
        for k in range(STAIRS - 1):
            scores(first_blk + k + 1, (k + 1) % 2, stair=k + 1)
            update(first_blk + k, k % 2, row0=k * T_K, first=is_tile_start and k == 0)
        update(first_blk + STAIRS - 1, (STAIRS - 1) % 2, row0=(STAIRS - 1) * T_K)

    @pl.when(n_full == 0)
    def _():
        scores(0, 0, stair=0)
        staircase(0, True)

    @pl.when(n_full > 0)
    def _():
        scores(0, 0)
        scores(1, 1)
        update(0, 0, first=True)

        def body(i, carry):
            b = 1 + ATT_UNROLL * i
            scores(b + 1, 0)
            update(b, 1)
            scores(b + 2, 1)
            update(b + 1, 0)
            return carry

        lax.fori_loop(0, (n_full - 2) // ATT_UNROLL, body, 0)
        scores(n_full, 0, stair=0)
        update(n_full - 1, 1)
        staircase(n_full, False)

    a0 = acc_ref[0]
    a1 = acc_ref[1]
    low = lax.broadcasted_iota(jnp.int32, (T_Q, LANES), 1) < HEAD_DIM
    numer = jnp.where(low, a0, a1)
    denom = pltpu.roll(jnp.where(low, a1, a0), HEAD_DIM, axis=1)
    o_ref[0] = numer / denom


def _mlp_kernel(x_ref, ya_ref, ys_ref, p_ref, wo_ref, w1_ref, w2_ref, wg_ref, wpe_ref,
                gatt_ref, gpm_ref, gpf_ref, gqf_ref, bg_ref, o_ref):
    subs = [slice(r, r + T_MLP_SUB) for r in range(0, T_MLP, T_MLP_SUB)]
    y = []
    for rows in subs:
        yan = _rms(ya_ref[rows, :], gatt_ref[...]).astype(BF16)
        y.append(jnp.dot(yan, wo_ref[:ATT_WIDTH, :], preferred_element_type=F32)
                 + jnp.dot(ys_ref[rows, :], wo_ref[ATT_WIDTH:, :], preferred_element_type=F32))
    h1 = [x_ref[rows, :] + _rms(yy, gpm_ref[...]) for rows, yy in zip(subs, y)]
    cn = [_rms(hh, gpf_ref[...]).astype(BF16) for hh in h1]
    ff = [None] * len(subs)
    for c in range(D_FF // FF_CHUNK):
        cols = slice(c * FF_CHUNK, (c + 1) * FF_CHUNK)
        for i in range(len(subs)):
            hid = jnp.dot(cn[i], w1_ref[:, cols], preferred_element_type=F32)
            hid = jnp.square(jnp.maximum(hid, 0.0)).astype(BF16)
            part = jnp.dot(hid, w2_ref[cols, :], preferred_element_type=F32)
            ff[i] = part if ff[i] is None else ff[i] + part
    for i, rows in enumerate(subs):
        h2 = h1[i] + _rms(ff[i], gqf_ref[...])
        gate = jax.nn.sigmoid(
            jnp.dot(h2.astype(BF16), wg_ref[...], preferred_element_type=F32) + bg_ref[...])
        pe = jnp.dot(p_ref[rows, :].astype(BF16), wpe_ref[...], preferred_element_type=F32)
        o_ref[rows, :] = h2 + gate * pe


def _layer(h, p, w_in, f_bias, sg_ln_g, sg_ln_b, sg_w, sg_b, att_out_g, sg_out_g, w_out,
           pre_mix_g, post_mix_g, pre_ffn_g, post_ffn_g, w_ff1, w_ff2, ple_w, ple_gate_w, ple_gate_b):
    B, S, D = h.shape
    R = B * S
    assert D == D_MODEL and w_in.shape == (D, 3 * ATT_WIDTH + N_HEADS + 2 * SG_WIDTH)
    assert S % T_Q == 0 and S % T_IN == 0 and R % T_MLP == 0 and D % (B * (S // T_IN)) == 0
    assert T_IN % T_SUB == 0 and T_SUB % CHUNK == 0 and T_MLP % T_MLP_SUB == 0 and D_FF % FF_CHUNK == 0
    row = lambda a: a.reshape(1, -1).astype(F32)

    wqkv = w_in[:, :3 * ATT_WIDTH].astype(BF16)
    wf8 = w_in[:, 3 * ATT_WIDTH:3 * ATT_WIDTH + N_HEADS]
    n_rep = N_PIECES * N_PIECES
    wf = jnp.pad(jnp.tile(wf8, (1, n_rep)), ((0, 0), (0, LANES - n_rep * N_HEADS))).astype(BF16)
    fb = jnp.pad(jnp.tile(f_bias, n_rep), (0, LANES - n_rep * N_HEADS)).reshape(1, LANES).astype(F32)
    tril = jnp.tril(jnp.ones((T_SUB, T_SUB), BF16))
    wuv = w_in[:, 3 * ATT_WIDTH + N_HEADS:].astype(BF16)
    sgb = jnp.repeat(jnp.transpose(sg_b), HEAD_DIM, axis=1).astype(F32)

    const2 = lambda shape: pl.BlockSpec(shape, lambda b, t: (0,) * len(shape))
    n_t = S // T_IN
    slab = lambda w: pl.BlockSpec((w.shape[0] // (B * n_t), w.shape[1]), lambda b, t: (b * n_t + t, 0))
    cast_ws = (w_ff1, w_ff2, ple_gate_w, w_out)
    qp, kp, vp, ysg, w1b, w2b, wgb, wob = pl.pallas_call(
        _inproj_kernel,
        grid=(B, S // T_IN),
        in_specs=[
            pl.BlockSpec((1, T_IN, D), lambda b, t: (b, t, 0)),
            const2((1, D)),
            const2((D, 3 * ATT_WIDTH)),
            const2((D, LANES)),
            const2((1, LANES)),
            const2((T_SUB, T_SUB)),
            const2((D, 2 * SG_WIDTH)),
            const2((1, SG_WIDTH)),
            const2((1, SG_WIDTH)),
            const2((N_SG_GROUPS, CHUNK, CHUNK)),
            const2((CHUNK, SG_WIDTH)),
            const2((1, SG_WIDTH)),
        ] + [slab(w) for w in cast_ws],
        out_specs=[
            pl.BlockSpec((1, N_HEADS, T_IN, LANES), lambda b, t: (b, 0, t, 0)),
            pl.BlockSpec((1, N_HEADS, T_IN, LANES), lambda b, t: (b, 0, t, 0)),
            pl.BlockSpec((1, N_HEADS, T_IN, LANES), lambda b, t: (b, 0, t, 0)),
            pl.BlockSpec((1, T_IN, SG_WIDTH), lambda b, t: (b, t, 0)),
        ] + [slab(w) for w in cast_ws],
        out_shape=[
            jax.ShapeDtypeStruct((B, N_HEADS, S, LANES), BF16),
            jax.ShapeDtypeStruct((B, N_HEADS, S, LANES), BF16),
            jax.ShapeDtypeStruct((B, N_HEADS, S, LANES), BF16),
            jax.ShapeDtypeStruct((B, S, SG_WIDTH), BF16),
        ] + [jax.ShapeDtypeStruct(w.shape, BF16) for w in cast_ws],
        scratch_shapes=[pltpu.VMEM((1, LANES), F32), pltpu.VMEM((T_IN, SG_WIDTH), F32)],
        compiler_params=pltpu.CompilerParams(
            dimension_semantics=("arbitrary", "arbitrary"), vmem_limit_bytes=VMEM_LIMIT),
        name="inproj",
    )(h, row(pre_mix_g), wqkv, wf, fb, tril, wuv, row(sg_ln_g), row(sg_ln_b), sg_w.astype(F32), sgb,
      row(sg_out_g), *cast_ws)

    yatt = pl.pallas_call(
        _attn_kernel,
        grid=(B, HEAD_PAIRS, S // T_Q),
        in_specs=[
            pl.BlockSpec((1, 2, T_Q, LANES), lambda b, hp, i: (b, hp, i, 0)),
            pl.BlockSpec((1, 2, S, LANES), lambda b, hp, i: (b, hp, 0, 0)),
            pl.BlockSpec((1, 2, S, LANES), lambda b, hp, i: (b, hp, 0, 0)),
        ],
        out_specs=pl.BlockSpec((1, T_Q, LANES), lambda b, hp, i: (b, i, hp)),
        out_shape=jax.ShapeDtypeStruct((B, S, ATT_WIDTH), F32),
        scratch_shapes=[pltpu.VMEM((2, T_Q, LANES), F32), pltpu.VMEM((2, T_Q, LANES), F32),
                        pltpu.VMEM((2, 2, T_Q, T_K), F32), pltpu.VMEM((2, 2, T_Q, LANES), F32)],
        compiler_params=pltpu.CompilerParams(
            dimension_semantics=("arbitrary", "arbitrary", "arbitrary"), vmem_limit_bytes=VMEM_LIMIT),
        name="fox_attn",
    )(qp, kp, vp)

    rows = lambda w: pl.BlockSpec((T_MLP, w), lambda i: (i, 0))
    const = lambda shape: pl.BlockSpec(shape, lambda i: (0,) * len(shape), pipeline_mode=pl.Buffered(1))
    out = pl.pallas_call(
        _mlp_kernel,
        grid=(R // T_MLP,),
        in_specs=[
            rows(D), rows(ATT_WIDTH), rows(SG_WIDTH), rows(PLE_DIM),
            const((ATT_WIDTH + SG_WIDTH, D)),
            const((D, D_FF)), const((D_FF, D)),
            const((D, D)), const((PLE_DIM, D)),
            const((1, ATT_WIDTH)), const((1, D)), const((1, D)), const((1, D)), const((1, D)),
        ],
        out_specs=rows(D),
        out_shape=jax.ShapeDtypeStruct((R, D), F32),
        compiler_params=pltpu.CompilerParams(
            dimension_semantics=("arbitrary",), vmem_limit_bytes=VMEM_LIMIT),
        name="mlp",
    )(h.reshape(R, D), yatt.reshape(R, ATT_WIDTH), ysg.reshape(R, SG_WIDTH), p.reshape(R, PLE_DIM),
      wob, w1b, w2b, wgb, ple_w.astype(BF16),
      row(att_out_g), row(post_mix_g), row(pre_ffn_g), row(post_ffn_g), row(ple_gate_b))
    return out.reshape(B, S, D)


def kernel(x, p, w_in, f_bias, sg_ln_g, sg_ln_b, sg_w, sg_b, att_out_g, sg_out_g, w_out,
           pre_mix_g, post_mix_g, pre_ffn_g, post_ffn_g, w_ff1, w_ff2, ple_w, ple_gate_w, ple_gate_b):
    h = x
    for i in range(p.shape[0]):
        h = _layer(h, p[i], w_in[i], f_bias[i], sg_ln_g[i], sg_ln_b[i], sg_w[i], sg_b[i],
                   att_out_g[i], sg_out_g[i], w_out[i], pre_mix_g[i], post_mix_g[i],
                   pre_ffn_g[i], post_ffn_g[i], w_ff1[i], w_ff2[i], ple_w[i], ple_gate_w[i],
                   ple_gate_b[i])
    return h
```

```python
import jax
import jax.numpy as jnp
from jax import lax
from jax.experimental import pallas as pl
from jax.experimental.pallas import tpu as pltpu

D_MODEL = 1024
PLE_DIM = 256
HEAD_DIM = 64
ATT_WIDTH = 512
N_HEADS = 8
SG_WIDTH = 512
N_SG_GROUPS = 8
CHUNK = 128
D_FF = 4096
EPS = 1e-6

LANES = 128
SUB = 8
HEAD_PAIRS = N_HEADS // 2
NEG_BIG = -1e30
LOG2E = 1.4426950408889634
N_PIECES = 3
PIECE_LANES = N_PIECES * N_HEADS
BIAS_LANES = 2 * PIECE_LANES

T_IN = 1024
T_SUB = 256
T_Q = 2048
T_K = 512
STAIRS = T_Q // T_K
ATT_UNROLL = 2
assert ATT_UNROLL == 2 and STAIRS % 2 == 0
T_MLP = 512
T_MLP_SUB = 256
FF_CHUNK = 1024

VMEM_LIMIT = 56 * 1024 * 1024

F32 = jnp.float32
BF16 = jnp.bfloat16


def _rms(x, g):
    return x * lax.rsqrt(jnp.mean(x * x, axis=-1, keepdims=True) + EPS) * g


def _gelu_tanh(x):
    return x * (0.5 * (1.0 + jnp.tanh(0.7978845608028654 * (x + 0.044715 * (x * x * x)))))


def _split3(x):
    hi = x.astype(BF16).astype(F32)
    r = x - hi
    mid = r.astype(BF16).astype(F32)
    lo = (r - mid).astype(BF16).astype(F32)
    return hi, mid, lo


def _inproj_kernel(x_ref, g_ref, wqkv_ref, wf_ref, fb_ref, tril_ref, wuv_ref, lng_ref, lnb_ref,
                   sgw_ref, sgb_ref, sgog_ref, w1_ref, w2_ref, wg_ref, wo_ref,
                   qp_ref, kp_ref, vp_ref, ysg_ref, w1b_ref, w2b_ref, wgb_ref, wob_ref,
                   carry_ref, ysg_scr):
    t = pl.program_id(1)

    @pl.when(t == 0)
    def _():
        carry_ref[...] = jnp.zeros_like(carry_ref)

    w1b_ref[...] = w1_ref[...].astype(BF16)
    w2b_ref[...] = w2_ref[...].astype(BF16)
    wgb_ref[...] = wg_ref[...].astype(BF16)
    wob_ref[...] = wo_ref[...].astype(BF16)

    subs = [slice(r, r + T_SUB) for r in range(0, T_IN, T_SUB)]
    lane = lax.broadcasted_iota(jnp.int32, (T_SUB, LANES), 1)
    ab = [_rms(x_ref[0, rows, :], g_ref[...]).astype(BF16) for rows in subs]
    zuv = [jnp.dot(a, wuv_ref[...], preferred_element_type=F32) for a in ab]
    f = [jnp.dot(a, wf_ref[...], preferred_element_type=F32) + fb_ref[...] for a in ab]

    c3 = []
    carry = carry_ref[...]
    for fi in f:
        logf = jnp.minimum(fi, 0.0) - jnp.log1p(jnp.exp(-jnp.abs(fi)))
        hi, mid, lo = _split3(logf)
        pieces = jnp.where(lane < PIECE_LANES, hi, jnp.where(lane < 2 * PIECE_LANES, mid, lo)).astype(BF16)
        c3.append(jnp.dot(tril_ref[...], pieces, preferred_element_type=F32) + carry)
        carry = c3[-1][T_SUB - 1:T_SUB, :]
    carry_ref[...] = carry

    zqkv = []
    for rows, a, zuv_i in zip(subs, ab, zuv):
        zqkv.append(jnp.dot(a, wqkv_ref[...], preferred_element_type=F32))
        _spatial_gating(rows, zuv_i, lng_ref, lnb_ref, sgw_ref, sgb_ref, sgog_ref, ysg_ref, ysg_scr)
    for rows, zqkv_i, c3_i in zip(subs, zqkv, c3):
        _pack_heads(rows, lane, zqkv_i, c3_i, qp_ref, kp_ref, vp_ref)


def _spatial_gating(rows, zuv, lng_ref, lnb_ref, sgw_ref, sgb_ref, sgog_ref, ysg_ref, ysg_scr):
    r0 = rows.start
    u = _gelu_tanh(zuv[:, :SG_WIDTH])
    v = _gelu_tanh(zuv[:, SG_WIDTH:])
    vc = v - jnp.mean(v, axis=-1, keepdims=True)
    vln = vc * lax.rsqrt(jnp.mean(vc * vc, axis=-1, keepdims=True) + EPS) * lng_ref[...] + lnb_ref[...]
    vb = vln.astype(BF16)
    n_chunks = T_SUB // CHUNK
    ci = lax.broadcasted_iota(jnp.int32, (CHUNK, CHUNK), 0)
    cj = lax.broadcasted_iota(jnp.int32, (CHUNK, CHUNK), 1)
    lane_c = lax.broadcasted_iota(jnp.int32, (CHUNK, LANES), 1)
    for pb in range(SG_WIDTH // LANES):
        vcat = jnp.concatenate(
            [vb[c * CHUNK:(c + 1) * CHUNK, pb * LANES:(pb + 1) * LANES] for c in range(n_chunks)], axis=1)
        res = []
        for g in (2 * pb, 2 * pb + 1):
            w = jnp.where(cj <= ci, sgw_ref[g], 0.0).astype(BF16)
            res.append(jnp.dot(w, vcat, preferred_element_type=F32))
        for c in range(n_chunks):
            mixed = jnp.where(lane_c < HEAD_DIM,
                              res[0][:, c * LANES:(c + 1) * LANES],
                              res[1][:, c * LANES:(c + 1) * LANES])
            mixed = mixed + sgb_ref[:, pb * LANES:(pb + 1) * LANES]
            ysg_scr[r0 + c * CHUNK:r0 + (c + 1) * CHUNK, pb * LANES:(pb + 1) * LANES] = (
                u[c * CHUNK:(c + 1) * CHUNK, pb * LANES:(pb + 1) * LANES] * mixed)
    ysg_ref[0, rows, :] = _rms(ysg_scr[rows, :], sgog_ref[...]).astype(BF16)


def _pack_heads(rows, lane, zqkv, c3, qp_ref, kp_ref, vp_ref):
    csum = (c3 + pltpu.roll(c3, LANES - PIECE_LANES, axis=1)
            + pltpu.roll(c3, LANES - 2 * PIECE_LANES, axis=1))
    chi, cmid, clo = _split3(csum * LOG2E)
    cp = jnp.where(lane < N_HEADS, chi, jnp.where(lane < 2 * N_HEADS, cmid, clo))
    q_bias_lo = jnp.where(lane < PIECE_LANES, cp, jnp.where(lane < BIAS_LANES, 1.0, 0.0))
    k_bias_lo = jnp.where(lane < PIECE_LANES, 1.0,
                          jnp.where(lane < BIAS_LANES, -pltpu.roll(cp, PIECE_LANES, axis=1), 0.0))
    q_bias_hi = pltpu.roll(q_bias_lo, HEAD_DIM, axis=1)
    k_bias_hi = pltpu.roll(k_bias_lo, HEAD_DIM, axis=1)

    for h in range(N_HEADS):
        blk = h // 2
        even = (h % 2 == 0)
        in_head = (lane < HEAD_DIM) if even else (lane >= HEAD_DIM)
        own_bias = ((lane & (N_HEADS - 1)) == h) & ((lane & (HEAD_DIM - 1)) < BIAS_LANES)
        zq = zqkv[:, blk * LANES:(blk + 1) * LANES] * (HEAD_DIM ** -0.5 * LOG2E)
        zk = zqkv[:, ATT_WIDTH + blk * LANES:ATT_WIDTH + (blk + 1) * LANES]
        zv = zqkv[:, 2 * ATT_WIDTH + blk * LANES:2 * ATT_WIDTH + (blk + 1) * LANES]
        qp_ref[0, h, rows, :] = jnp.where(in_head, zq, q_bias_hi if even else q_bias_lo).astype(BF16)
        kp_ref[0, h, rows, :] = jnp.where(
            in_head, zk, jnp.where(own_bias, k_bias_hi if even else k_bias_lo, 0.0)).astype(BF16)
        vp_ref[0, h, rows, :] = jnp.where(in_head, zv, 1.0).astype(BF16)


def _attn_kernel(q_ref, k_ref, v_ref, o_ref, acc_ref, m_ref, s_scr, mc_scr, qt_scr):
    n_full = pl.program_id(2) * STAIRS
    for hh in range(2):
        qt_scr[hh] = q_ref[0, hh].astype(F32).T.astype(BF16)

    def scores(b, slot, stair=None):
        col0 = 0 if stair is None else stair * T_K
        start = pl.multiple_of(b * T_K, T_K)
        for hh in range(2):
            kb = k_ref[0, hh, pl.ds(start, T_K), :]
            s = jnp.dot(kb, qt_scr[hh, :, col0:], preferred_element_type=F32)
            parts = [(col0, T_Q, s)]
            if stair is not None:
                key = lax.broadcasted_iota(jnp.int32, (T_K, T_K), 0)
                qry = lax.broadcasted_iota(jnp.int32, (T_K, T_K), 1)
                parts = [(col0, col0 + T_K, jnp.where(key <= qry, s[:, :T_K], NEG_BIG))]
                if col0 + T_K < T_Q:
                    parts.append((col0 + T_K, T_Q, s[:, T_K:]))
            for lo, hi, sp in parts:
                s_scr[slot, hh, :, lo:hi] = sp
                mc_scr[slot, hh, :, lo:hi] = jnp.broadcast_to(
                    jnp.max(sp, axis=0, keepdims=True), (SUB, hi - lo))

    def update(b, slot, col0=0, first=False):
        start = pl.multiple_of(b * T_K, T_K)
        for hh in range(2):
            vt = v_ref[0, hh, pl.ds(start, T_K), :].astype(F32).T.astype(BF16)
            m_blk = mc_scr[slot, hh, :, col0:]
            if first:
                m_new = m_blk
            else:
                m_old = m_ref[hh, :, col0:]
                m_new = jnp.maximum(m_old, m_blk)
            p = jnp.exp2(s_scr[slot, hh, :, col0:] - m_new[0:1, :]).astype(BF16)
            pv = jnp.dot(vt, p, preferred_element_type=F32)
            if first:
                acc_ref[hh] = pv
            else:
                alpha = jnp.exp2(m_old - m_new)
                acc_ref[hh, :, col0:] = alpha[0:1, :] * acc_ref[hh, :, col0:] + pv
            m_ref[hh, :, col0:] = m_new

    def staircase(first_blk, is_tile_start):
        for k in range(STAIRS - 1):
            scores(first_blk + k + 1, (k + 1) % 2, stair=k + 1)
            update(first_blk + k, k % 2, col0=k * T_K, first=is_tile_start and k == 0)
        update(first_blk + STAIRS - 1, (STAIRS - 1) % 2, col0=(STAIRS - 1) * T_K)

    @pl.when(n_full == 0)
    def _():
        scores(0, 0, stair=0)
        staircase(0, True)

    @pl.when(n_full > 0)
    def _():
        scores(0, 0)
        scores(1, 1)
        update(0, 0, first=True)

        def body(i, carry):
            b = 1 + ATT_UNROLL * i
            scores(b + 1, 0)
            update(b, 1)
            scores(b + 2, 1)
            update(b + 1, 0)
            return carry

        lax.fori_loop(0, (n_full - 2) // ATT_UNROLL, body, 0)
        scores(n_full, 0, stair=0)
        update(n_full - 1, 1)
        staircase(n_full, False)

    a0 = acc_ref[0]
    a1 = acc_ref[1]
    numer = jnp.concatenate([a0[:HEAD_DIM], a1[HEAD_DIM:]], axis=0)
    denom = jnp.concatenate([a0[HEAD_DIM:], a1[:HEAD_DIM]], axis=0)
    o_ref[0] = (numer / denom).T


def _mlp_kernel(x_ref, ya_ref, ys_ref, p_ref, wo_ref, w1_ref, w2_ref, wg_ref, wpe_ref,
                gatt_ref, gpm_ref, gpf_ref, gqf_ref, bg_ref, o_ref):
    subs = [slice(r, r + T_MLP_SUB) for r in range(0, T_MLP, T_MLP_SUB)]
    y = []
    for rows in subs:
        yan = _rms(ya_ref[rows, :], gatt_ref[...]).astype(BF16)
        y.append(jnp.dot(yan, wo_ref[:ATT_WIDTH, :], preferred_element_type=F32)
                 + jnp.dot(ys_ref[rows, :], wo_ref[ATT_WIDTH:, :], preferred_element_type=F32))
    h1 = [x_ref[rows, :] + _rms(yy, gpm_ref[...]) for rows, yy in zip(subs, y)]
    cn = [_rms(hh, gpf_ref[...]).astype(BF16) for hh in h1]
    ff = [None] * len(subs)
    for c in range(D_FF // FF_CHUNK):
        cols = slice(c * FF_CHUNK, (c + 1) * FF_CHUNK)
        for i in range(len(subs)):
            hid = jnp.dot(cn[i], w1_ref[:, cols], preferred_element_type=F32)
            hid = jnp.square(jnp.maximum(hid, 0.0)).astype(BF16)
            part = jnp.dot(hid, w2_ref[cols, :], preferred_element_type=F32)
            ff[i] = part if ff[i] is None else ff[i] + part
    for i, rows in enumerate(subs):
        h2 = h1[i] + _rms(ff[i], gqf_ref[...])
        gate = jax.nn.sigmoid(
            jnp.dot(h2.astype(BF16), wg_ref[...], preferred_element_type=F32) + bg_ref[...])
        pe = jnp.dot(p_ref[rows, :].astype(BF16), wpe_ref[...], preferred_element_type=F32)
        o_ref[rows, :] = h2 + gate * pe


def _layer(h, p, w_in, f_bias, sg_ln_g, sg_ln_b, sg_w, sg_b, att_out_g, sg_out_g, w_out,
           pre_mix_g, post_mix_g, pre_ffn_g, post_ffn_g, w_ff1, w_ff2, ple_w, ple_gate_w, ple_gate_b):
    B, S, D = h.shape
    R = B * S
    assert D == D_MODEL and w_in.shape == (D, 3 * ATT_WIDTH + N_HEADS + 2 * SG_WIDTH)
    assert S % T_Q == 0 and S % T_IN == 0 and R % T_MLP == 0 and D % (B * (S // T_IN)) == 0
    assert T_IN % T_SUB == 0 and T_SUB % CHUNK == 0 and T_MLP % T_MLP_SUB == 0 and D_FF % FF_CHUNK == 0
    row = lambda a: a.reshape(1, -1).astype(F32)

    wqkv = w_in[:, :3 * ATT_WIDTH].astype(BF16)
    wf8 = w_in[:, 3 * ATT_WIDTH:3 * ATT_WIDTH + N_HEADS]
    n_rep = N_PIECES * N_PIECES
    wf = jnp.pad(jnp.tile(wf8, (1, n_rep)), ((0, 0), (0, LANES - n_rep * N_HEADS))).astype(BF16)
    fb = jnp.pad(jnp.tile(f_bias, n_rep), (0, LANES - n_rep * N_HEADS)).reshape(1, LANES).astype(F32)
    tril = jnp.tril(jnp.ones((T_SUB, T_SUB), BF16))
    wuv = w_in[:, 3 * ATT_WIDTH + N_HEADS:].astype(BF16)
    sgb = jnp.repeat(jnp.transpose(sg_b), HEAD_DIM, axis=1).astype(F32)

    const2 = lambda shape: pl.BlockSpec(shape, lambda b, t: (0,) * len(shape))
    n_t = S // T_IN
    slab = lambda w: pl.BlockSpec((w.shape[0] // (B * n_t), w.shape[1]), lambda b, t: (b * n_t + t, 0))
    cast_ws = (w_ff1, w_ff2, ple_gate_w, w_out)
    qp, kp, vp, ysg, w1b, w2b, wgb, wob = pl.pallas_call(
        _inproj_kernel,
        grid=(B, S // T_IN),
        in_specs=[
            pl.BlockSpec((1, T_IN, D), lambda b, t: (b, t, 0)),
            const2((1, D)),
            const2((D, 3 * ATT_WIDTH)),
            const2((D, LANES)),
            const2((1, LANES)),
            const2((T_SUB, T_SUB)),
            const2((D, 2 * SG_WIDTH)),
            const2((1, SG_WIDTH)),
            const2((1, SG_WIDTH)),
            const2((N_SG_GROUPS, CHUNK, CHUNK)),
            const2((CHUNK, SG_WIDTH)),
            const2((1, SG_WIDTH)),
        ] + [slab(w) for w in cast_ws],
        out_specs=[
            pl.BlockSpec((1, N_HEADS, T_IN, LANES), lambda b, t: (b, 0, t, 0)),
            pl.BlockSpec((1, N_HEADS, T_IN, LANES), lambda b, t: (b, 0, t, 0)),
            pl.BlockSpec((1, N_HEADS, T_IN, LANES), lambda b, t: (b, 0, t, 0)),
            pl.BlockSpec((1, T_IN, SG_WIDTH), lambda b, t: (b, t, 0)),
        ] + [slab(w) for w in cast_ws],
        out_shape=[
            jax.ShapeDtypeStruct((B, N_HEADS, S, LANES), BF16),
            jax.ShapeDtypeStruct((B, N_HEADS, S, LANES), BF16),
            jax.ShapeDtypeStruct((B, N_HEADS, S, LANES), BF16),
            jax.ShapeDtypeStruct((B, S, SG_WIDTH), BF16),
        ] + [jax.ShapeDtypeStruct(w.shape, BF16) for w in cast_ws],
        scratch_shapes=[pltpu.VMEM((1, LANES), F32), pltpu.VMEM((T_IN, SG_WIDTH), F32)],
        compiler_params=pltpu.CompilerParams(
            dimension_semantics=("arbitrary", "arbitrary"), vmem_limit_bytes=VMEM_LIMIT),
        name="inproj",
    )(h, row(pre_mix_g), wqkv, wf, fb, tril, wuv, row(sg_ln_g), row(sg_ln_b), sg_w.astype(F32), sgb,
      row(sg_out_g), *cast_ws)

    yatt = pl.pallas_call(
        _attn_kernel,
        grid=(B, HEAD_PAIRS, S // T_Q),
        in_specs=[
            pl.BlockSpec((1, 2, T_Q, LANES), lambda b, hp, i: (b, hp, i, 0)),
            pl.BlockSpec((1, 2, S, LANES), lambda b, hp, i: (b, hp, 0, 0)),
            pl.BlockSpec((1, 2, S, LANES), lambda b, hp, i: (b, hp, 0, 0)),
        ],
        out_specs=pl.BlockSpec((1, T_Q, LANES), lambda b, hp, i: (b, i, hp)),
        out_shape=jax.ShapeDtypeStruct((B, S, ATT_WIDTH), F32),
        scratch_shapes=[pltpu.VMEM((2, LANES, T_Q), F32), pltpu.VMEM((2, SUB, T_Q), F32),
                        pltpu.VMEM((2, 2, T_K, T_Q), F32), pltpu.VMEM((2, 2, SUB, T_Q), F32),
                        pltpu.VMEM((2, LANES, T_Q), BF16)],
        compiler_params=pltpu.CompilerParams(
            dimension_semantics=("arbitrary", "arbitrary", "arbitrary"), vmem_limit_bytes=VMEM_LIMIT),
        name="fox_attn",
    )(qp, kp, vp)

    rows = lambda w: pl.BlockSpec((T_MLP, w), lambda i: (i, 0))
    const = lambda shape: pl.BlockSpec(shape, lambda i: (0,) * len(shape), pipeline_mode=pl.Buffered(1))
    out = pl.pallas_call(
        _mlp_kernel,
        grid=(R // T_MLP,),
        in_specs=[
            rows(D), rows(ATT_WIDTH), rows(SG_WIDTH), rows(PLE_DIM),
            const((ATT_WIDTH + SG_WIDTH, D)),
            const((D, D_FF)), const((D_FF, D)),
            const((D, D)), const((PLE_DIM, D)),
            const((1, ATT_WIDTH)), const((1, D)), const((1, D)), const((1, D)), const((1, D)),
        ],
        out_specs=rows(D),
        out_shape=jax.ShapeDtypeStruct((R, D), F32),
        compiler_params=pltpu.CompilerParams(
            dimension_semantics=("arbitrary",), vmem_limit_bytes=VMEM_LIMIT),
        name="mlp",
    )(h.reshape(R, D), yatt.reshape(R, ATT_WIDTH), ysg.reshape(R, SG_WIDTH), p.reshape(R, PLE_DIM),
      wob, w1b, w2b, wgb, ple_w.astype(BF16),
      row(att_out_g), row(post_mix_g), row(pre_ffn_g), row(post_ffn_g), row(ple_gate_b))
    return out.reshape(B, S, D)


def kernel(x, p, w_in, f_bias, sg_ln_g, sg_ln_b, sg_w, sg_b, att_out_g, sg_out_g, w_out,
           pre_mix_g, post_mix_g, pre_ffn_g, post_ffn_g, w_ff1, w_ff2, ple_w, ple_gate_w, ple_gate_b):
    h = x
    for i in range(p.shape[0]):
        h = _layer(h, p[i], w_in[i], f_bias[i], sg_ln_g[i], sg_ln_b[i], sg_w[i], sg_b[i],
                   att_out_g[i], sg_out_g[i], w_out[i], pre_mix_g[i], post_mix_g[i],
                   pre_ffn_g[i], post_ffn_g[i], w_ff1[i], w_ff2[i], ple_w[i], ple_gate_w[i],
                   ple_gate_b[i])
    return h
```

```python
import jax
import jax.numpy as jnp
from jax import lax
from jax.experimental import pallas as pl
from jax.experimental.pallas import tpu as pltpu

D_MODEL = 1024
PLE_DIM = 256
HEAD_DIM = 64
ATT_WIDTH = 512
N_HEADS = 8
SG_WIDTH = 512
N_SG_GROUPS = 8
CHUNK = 128
D_FF = 4096
EPS = 1e-6

LANES = 128
SUB = 8
HEAD_PAIRS = N_HEADS // 2
NEG_BIG = -1e30
LOG2E = 1.4426950408889634
N_PIECES = 3
PIECE_LANES = N_PIECES * N_HEADS
BIAS_LANES = 2 * PIECE_LANES

T_IN = 1024
T_SUB = 256
T_Q = 2048
T_K = 512
STAIRS = T_Q // T_K
ATT_UNROLL = 2
assert ATT_UNROLL == 2 and STAIRS % 2 == 0
T_MLP = 512
T_MLP_SUB = 256
FF_CHUNK = 1024

VMEM_LIMIT = 56 * 1024 * 1024

F32 = jnp.float32
BF16 = jnp.bfloat16


def _rms(x, g):
    return x * lax.rsqrt(jnp.mean(x * x, axis=-1, keepdims=True) + EPS) * g


def _gelu_tanh(x):
    return x * (0.5 * (1.0 + jnp.tanh(0.7978845608028654 * (x + 0.044715 * (x * x * x)))))


def _split3(x):
    hi = x.astype(BF16).astype(F32)
    r = x - hi
    mid = r.astype(BF16).astype(F32)
    lo = (r - mid).astype(BF16).astype(F32)
    return hi, mid, lo


def _inproj_kernel(x_ref, g_ref, wqkv_ref, wf_ref, fb_ref, tril_ref, wuv_ref, lng_ref, lnb_ref,
                   sgw_ref, sgb_ref, sgog_ref, w1_ref, w2_ref, wg_ref, wo_ref,
                   qp_ref, kp_ref, vp_ref, ysg_ref, w1b_ref, w2b_ref, wgb_ref, wob_ref,
                   carry_ref, ysg_scr):
    t = pl.program_id(1)

    @pl.when(t == 0)
    def _():
        carry_ref[...] = jnp.zeros_like(carry_ref)

    w1b_ref[...] = w1_ref[...].astype(BF16)
    w2b_ref[...] = w2_ref[...].astype(BF16)
    wgb_ref[...] = wg_ref[...].astype(BF16)
    wob_ref[...] = wo_ref[...].astype(BF16)

    subs = [slice(r, r + T_SUB) for r in range(0, T_IN, T_SUB)]
    lane = lax.broadcasted_iota(jnp.int32, (T_SUB, LANES), 1)
    ab = [_rms(x_ref[0, rows, :], g_ref[...]).astype(BF16) for rows in subs]
    zuv = [jnp.dot(a, wuv_ref[...], preferred_element_type=F32) for a in ab]
    f = [jnp.dot(a, wf_ref[...], preferred_element_type=F32) + fb_ref[...] for a in ab]

    c3 = []
    carry = carry_ref[...]
    for fi in f:
        logf = jnp.minimum(fi, 0.0) - jnp.log1p(jnp.exp(-jnp.abs(fi)))
        hi, mid, lo = _split3(logf)
        pieces = jnp.where(lane < PIECE_LANES, hi, jnp.where(lane < 2 * PIECE_LANES, mid, lo)).astype(BF16)
        c3.append(jnp.dot(tril_ref[...], pieces, preferred_element_type=F32) + carry)
        carry = c3[-1][T_SUB - 1:T_SUB, :]
    carry_ref[...] = carry

    zqkv = []
    for rows, a, zuv_i in zip(subs, ab, zuv):
        zqkv.append(jnp.dot(a, wqkv_ref[...], preferred_element_type=F32))
        _spatial_gating(rows, zuv_i, lng_ref, lnb_ref, sgw_ref, sgb_ref, sgog_ref, ysg_ref, ysg_scr)
    for rows, zqkv_i, c3_i in zip(subs, zqkv, c3):
        _pack_heads(rows, lane, zqkv_i, c3_i, qp_ref, kp_ref, vp_ref)


def _spatial_gating(rows, zuv, lng_ref, lnb_ref, sgw_ref, sgb_ref, sgog_ref, ysg_ref, ysg_scr):
    r0 = rows.start
    u = _gelu_tanh(zuv[:, :SG_WIDTH])
    v = _gelu_tanh(zuv[:, SG_WIDTH:])
    vc = v - jnp.mean(v, axis=-1, keepdims=True)
    vln = vc * lax.rsqrt(jnp.mean(vc * vc, axis=-1, keepdims=True) + EPS) * lng_ref[...] + lnb_ref[...]
    vb = vln.astype(BF16)
    n_chunks = T_SUB // CHUNK
    ci = lax.broadcasted_iota(jnp.int32, (CHUNK, CHUNK), 0)
    cj = lax.broadcasted_iota(jnp.int32, (CHUNK, CHUNK), 1)
    lane_c = lax.broadcasted_iota(jnp.int32, (CHUNK, LANES), 1)
    for pb in range(SG_WIDTH // LANES):
        vcat = jnp.concatenate(
            [vb[c * CHUNK:(c + 1) * CHUNK, pb * LANES:(pb + 1) * LANES] for c in range(n_chunks)], axis=1)
        res = []
        for g in (2 * pb, 2 * pb + 1):
            w = jnp.where(cj <= ci, sgw_ref[g], 0.0).astype(BF16)
            res.append(jnp.dot(w, vcat, preferred_element_type=F32))
        for c in range(n_chunks):
            mixed = jnp.where(lane_c < HEAD_DIM,
                              res[0][:, c * LANES:(c + 1) * LANES],
                              res[1][:, c * LANES:(c + 1) * LANES])
            mixed = mixed + sgb_ref[:, pb * LANES:(pb + 1) * LANES]
            ysg_scr[r0 + c * CHUNK:r0 + (c + 1) * CHUNK, pb * LANES:(pb + 1) * LANES] = (
                u[c * CHUNK:(c + 1) * CHUNK, pb * LANES:(pb + 1) * LANES] * mixed)
    ysg_ref[0, rows, :] = _rms(ysg_scr[rows, :], sgog_ref[...]).astype(BF16)


def _pack_heads(rows, lane, zqkv, c3, qp_ref, kp_ref, vp_ref):
    csum = (c3 + pltpu.roll(c3, LANES - PIECE_LANES, axis=1)
            + pltpu.roll(c3, LANES - 2 * PIECE_LANES, axis=1))
    chi, cmid, clo = _split3(csum * LOG2E)
    cp = jnp.where(lane < N_HEADS, chi, jnp.where(lane < 2 * N_HEADS, cmid, clo))
    q_bias_lo = jnp.where(lane < PIECE_LANES, cp, jnp.where(lane < BIAS_LANES, 1.0, 0.0))
    k_bias_lo = jnp.where(lane < PIECE_LANES, 1.0,
                          jnp.where(lane < BIAS_LANES, -pltpu.roll(cp, PIECE_LANES, axis=1), 0.0))
    q_bias_hi = pltpu.roll(q_bias_lo, HEAD_DIM, axis=1)
    k_bias_hi = pltpu.roll(k_bias_lo, HEAD_DIM, axis=1)

    for h in range(N_HEADS):
        blk = h // 2
        even = (h % 2 == 0)
        in_head = (lane < HEAD_DIM) if even else (lane >= HEAD_DIM)
        own_bias = ((lane & (N_HEADS - 1)) == h) & ((lane & (HEAD_DIM - 1)) < BIAS_LANES)
        zq = zqkv[:, blk * LANES:(blk + 1) * LANES] * (HEAD_DIM ** -0.5 * LOG2E)
        zk = zqkv[:, ATT_WIDTH + blk * LANES:ATT_WIDTH + (blk + 1) * LANES]
        zv = zqkv[:, 2 * ATT_WIDTH + blk * LANES:2 * ATT_WIDTH + (blk + 1) * LANES]
        qp_ref[0, h, rows, :] = jnp.where(in_head, zq, q_bias_hi if even else q_bias_lo).astype(BF16)
        kp_ref[0, h, rows, :] = jnp.where(
            in_head, zk, jnp.where(own_bias, k_bias_hi if even else k_bias_lo, 0.0)).astype(BF16)
        vp_ref[0, h, rows, :] = jnp.where(in_head, zv, 1.0).astype(BF16)


def _attn_kernel(q_ref, k_ref, v_ref, o_ref, acc_ref, m_ref, s_scr, mc_scr, qt_scr):
    n_full = pl.program_id(2) * STAIRS
    for hh in range(2):
        qt_scr[hh] = q_ref[0, hh].astype(F32).T.astype(BF16)

    chunks = [slice(c * T_K, (c + 1) * T_K) for c in range(STAIRS)]

    def scores_chunk(b, slot, c, masked):
        start = pl.multiple_of(b * T_K, T_K)
        for hh in range(2):
            kb = k_ref[0, hh, pl.ds(start, T_K), :]
            s = jnp.dot(kb, qt_scr[hh, :, chunks[c]], preferred_element_type=F32)
            if masked:
                key = lax.broadcasted_iota(jnp.int32, (T_K, T_K), 0)
                qry = lax.broadcasted_iota(jnp.int32, (T_K, T_K), 1)
                s = jnp.where(key <= qry, s, NEG_BIG)
            s_scr[slot, hh, :, chunks[c]] = s
            mc_scr[slot, hh, :, chunks[c]] = jnp.broadcast_to(jnp.max(s, axis=0, keepdims=True), (SUB, T_K))

    def update_chunk(slot, c, first, vts):
        for hh in range(2):
            m_blk = mc_scr[slot, hh, :, chunks[c]]
            if first:
                m_new = m_blk
            else:
                m_old = m_ref[hh, :, chunks[c]]
                m_new = jnp.maximum(m_old, m_blk)
            p = jnp.exp2(s_scr[slot, hh, :, chunks[c]] - m_new[0:1, :]).astype(BF16)
            pv = jnp.dot(vts[hh], p, preferred_element_type=F32)
            if first:
                acc_ref[hh, :, chunks[c]] = pv
            else:
                alpha = jnp.exp2(m_old - m_new)
                acc_ref[hh, :, chunks[c]] = alpha[0:1, :] * acc_ref[hh, :, chunks[c]] + pv
            m_ref[hh, :, chunks[c]] = m_new

    def stage(nxt=None, cur=None):
        vts = None
        if cur is not None:
            start = pl.multiple_of(cur[0] * T_K, T_K)
            vts = [v_ref[0, hh, pl.ds(start, T_K), :].astype(F32).T.astype(BF16) for hh in range(2)]
        for c in range(STAIRS):
            if nxt is not None and c >= (nxt[2] or 0):
                scores_chunk(nxt[0], nxt[1], c, masked=(nxt[2] == c))
            if cur is not None and c >= cur[2]:
                update_chunk(cur[1], c, cur[3], vts)

    def staircase(first_blk, is_tile_start):
        for k in range(STAIRS - 1):
            stage(nxt=(first_blk + k + 1, (k + 1) % 2, k + 1),
                  cur=(first_blk + k, k % 2, k, is_tile_start and k == 0))
        stage(cur=(first_blk + STAIRS - 1, (STAIRS - 1) % 2, STAIRS - 1, False))

    @pl.when(n_full == 0)
    def _():
        stage(nxt=(0, 0, 0))
        staircase(0, True)

    @pl.when(n_full > 0)
    def _():
        stage(nxt=(0, 0, None))
        stage(nxt=(1, 1, None), cur=(0, 0, 0, True))

        def body(i, carry):
            b = 1 + ATT_UNROLL * i
            stage(nxt=(b + 1, 0, None), cur=(b, 1, 0, False))
            stage(nxt=(b + 2, 1, None), cur=(b + 1, 0, 0, False))
            return carry

        lax.fori_loop(0, (n_full - 2) // ATT_UNROLL, body, 0)
        stage(nxt=(n_full, 0, 0), cur=(n_full - 1, 1, 0, False))
        staircase(n_full, False)

    a0 = acc_ref[0]
    a1 = acc_ref[1]
    numer = jnp.concatenate([a0[:HEAD_DIM], a1[HEAD_DIM:]], axis=0)
    denom = jnp.concatenate([a0[HEAD_DIM:], a1[:HEAD_DIM]], axis=0)
    o_ref[0] = (numer / denom).T


def _mlp_kernel(x_ref, ya_ref, ys_ref, p_ref, wo_ref, w1_ref, w2_ref, wg_ref, wpe_ref,
                gatt_ref, gpm_ref, gpf_ref, gqf_ref, bg_ref, o_ref):
    subs = [slice(r, r + T_MLP_SUB) for r in range(0, T_MLP, T_MLP_SUB)]
    y = []
    for rows in subs:
        yan = _rms(ya_ref[rows, :], gatt_ref[...]).astype(BF16)
        y.append(jnp.dot(yan, wo_ref[:ATT_WIDTH, :], preferred_element_type=F32)
                 + jnp.dot(ys_ref[rows, :], wo_ref[ATT_WIDTH:, :], preferred_element_type=F32))
    h1 = [x_ref[rows, :] + _rms(yy, gpm_ref[...]) for rows, yy in zip(subs, y)]
    cn = [_rms(hh, gpf_ref[...]).astype(BF16) for hh in h1]
    ff = [None] * len(subs)
    for c in range(D_FF // FF_CHUNK):
        cols = slice(c * FF_CHUNK, (c + 1) * FF_CHUNK)
        for i in range(len(subs)):
            hid = jnp.dot(cn[i], w1_ref[:, cols], preferred_element_type=F32)
            hid = jnp.square(jnp.maximum(hid, 0.0)).astype(BF16)
            part = jnp.dot(hid, w2_ref[cols, :], preferred_element_type=F32)
            ff[i] = part if ff[i] is None else ff[i] + part
    for i, rows in enumerate(subs):
        h2 = h1[i] + _rms(ff[i], gqf_ref[...])
        gate = jax.nn.sigmoid(
            jnp.dot(h2.astype(BF16), wg_ref[...], preferred_element_type=F32) + bg_ref[...])
        pe = jnp.dot(p_ref[rows, :].astype(BF16), wpe_ref[...], preferred_element_type=F32)
        o_ref[rows, :] = h2 + gate * pe


def _layer(h, p, w_in, f_bias, sg_ln_g, sg_ln_b, sg_w, sg_b, att_out_g, sg_out_g, w_out,
           pre_mix_g, post_mix_g, pre_ffn_g, post_ffn_g, w_ff1, w_ff2, ple_w, ple_gate_w, ple_gate_b):
    B, S, D = h.shape
    R = B * S
    assert D == D_MODEL and w_in.shape == (D, 3 * ATT_WIDTH + N_HEADS + 2 * SG_WIDTH)
    assert S % T_Q == 0 and S % T_IN == 0 and R % T_MLP == 0 and D % (B * (S // T_IN)) == 0
    assert T_IN % T_SUB == 0 and T_SUB % CHUNK == 0 and T_MLP % T_MLP_SUB == 0 and D_FF % FF_CHUNK == 0
    row = lambda a: a.reshape(1, -1).astype(F32)

    wqkv = w_in[:, :3 * ATT_WIDTH].astype(BF16)
    wf8 = w_in[:, 3 * ATT_WIDTH:3 * ATT_WIDTH + N_HEADS]
    n_rep = N_PIECES * N_PIECES
    wf = jnp.pad(jnp.tile(wf8, (1, n_rep)), ((0, 0), (0, LANES - n_rep * N_HEADS))).astype(BF16)
    fb = jnp.pad(jnp.tile(f_bias, n_rep), (0, LANES - n_rep * N_HEADS)).reshape(1, LANES).astype(F32)
    tril = jnp.tril(jnp.ones((T_SUB, T_SUB), BF16))
    wuv = w_in[:, 3 * ATT_WIDTH + N_HEADS:].astype(BF16)
    sgb = jnp.repeat(jnp.transpose(sg_b), HEAD_DIM, axis=1).astype(F32)

    const2 = lambda shape: pl.BlockSpec(shape, lambda b, t: (0,) * len(shape))
    n_t = S // T_IN
    slab = lambda w: pl.BlockSpec((w.shape[0] // (B * n_t), w.shape[1]), lambda b, t: (b * n_t + t, 0))
    cast_ws = (w_ff1, w_ff2, ple_gate_w, w_out)
    qp, kp, vp, ysg, w1b, w2b, wgb, wob = pl.pallas_call(
        _inproj_kernel,
        grid=(B, S // T_IN),
        in_specs=[
            pl.BlockSpec((1, T_IN, D), lambda b, t: (b, t, 0)),
            const2((1, D)),
            const2((D, 3 * ATT_WIDTH)),
            const2((D, LANES)),
            const2((1, LANES)),
            const2((T_SUB, T_SUB)),
            const2((D, 2 * SG_WIDTH)),
            const2((1, SG_WIDTH)),
            const2((1, SG_WIDTH)),
            const2((N_SG_GROUPS, CHUNK, CHUNK)),
            const2((CHUNK, SG_WIDTH)),
            const2((1, SG_WIDTH)),
        ] + [slab(w) for w in cast_ws],
        out_specs=[
            pl.BlockSpec((1, N_HEADS, T_IN, LANES), lambda b, t: (b, 0, t, 0)),
            pl.BlockSpec((1, N_HEADS, T_IN, LANES), lambda b, t: (b, 0, t, 0)),
            pl.BlockSpec((1, N_HEADS, T_IN, LANES), lambda b, t: (b, 0, t, 0)),
            pl.BlockSpec((1, T_IN, SG_WIDTH), lambda b, t: (b, t, 0)),
        ] + [slab(w) for w in cast_ws],
        out_shape=[
            jax.ShapeDtypeStruct((B, N_HEADS, S, LANES), BF16),
            jax.ShapeDtypeStruct((B, N_HEADS, S, LANES), BF16),
            jax.ShapeDtypeStruct((B, N_HEADS, S, LANES), BF16),
            jax.ShapeDtypeStruct((B, S, SG_WIDTH), BF16),
        ] + [jax.ShapeDtypeStruct(w.shape, BF16) for w in cast_ws],
        scratch_shapes=[pltpu.VMEM((1, LANES), F32), pltpu.VMEM((T_IN, SG_WIDTH), F32)],
        compiler_params=pltpu.CompilerParams(
            dimension_semantics=("arbitrary", "arbitrary"), vmem_limit_bytes=VMEM_LIMIT),
        name="inproj",
    )(h, row(pre_mix_g), wqkv, wf, fb, tril, wuv, row(sg_ln_g), row(sg_ln_b), sg_w.astype(F32), sgb,
      row(sg_out_g), *cast_ws)

    yatt = pl.pallas_call(
        _attn_kernel,
        grid=(B, HEAD_PAIRS, S // T_Q),
        in_specs=[
            pl.BlockSpec((1, 2, T_Q, LANES), lambda b, hp, i: (b, hp, i, 0)),
            pl.BlockSpec((1, 2, S, LANES), lambda b, hp, i: (b, hp, 0, 0)),
            pl.BlockSpec((1, 2, S, LANES), lambda b, hp, i: (b, hp, 0, 0)),
        ],
        out_specs=pl.BlockSpec((1, T_Q, LANES), lambda b, hp, i: (b, i, hp)),
        out_shape=jax.ShapeDtypeStruct((B, S, ATT_WIDTH), F32),
        scratch_shapes=[pltpu.VMEM((2, LANES, T_Q), F32), pltpu.VMEM((2, SUB, T_Q), F32),
                        pltpu.VMEM((2, 2, T_K, T_Q), F32), pltpu.VMEM((2, 2, SUB, T_Q), F32),
                        pltpu.VMEM((2, LANES, T_Q), BF16)],
        compiler_params=pltpu.CompilerParams(
            dimension_semantics=("arbitrary", "arbitrary", "arbitrary"), vmem_limit_bytes=VMEM_LIMIT),
        name="fox_attn",
    )(qp, kp, vp)

    rows = lambda w: pl.BlockSpec((T_MLP, w), lambda i: (i, 0))
    const = lambda shape: pl.BlockSpec(shape, lambda i: (0,) * len(shape), pipeline_mode=pl.Buffered(1))
    out = pl.pallas_call(
        _mlp_kernel,
        grid=(R // T_MLP,),
        in_specs=[
            rows(D), rows(ATT_WIDTH), rows(SG_WIDTH), rows(PLE_DIM),
            const((ATT_WIDTH + SG_WIDTH, D)),
            const((D, D_FF)), const((D_FF, D)),
            const((D, D)), const((PLE_DIM, D)),
            const((1, ATT_WIDTH)), const((1, D)), const((1, D)), const((1, D)), const((1, D)),
        ],
        out_specs=rows(D),
        out_shape=jax.ShapeDtypeStruct((R, D), F32),
        compiler_params=pltpu.CompilerParams(
            dimension_semantics=("arbitrary",), vmem_limit_bytes=VMEM_LIMIT),
        name="mlp",
    )(h.reshape(R, D), yatt.reshape(R, ATT_WIDTH), ysg.reshape(R, SG_WIDTH), p.reshape(R, PLE_DIM),
      wob, w1b, w2b, wgb, ple_w.astype(BF16),
      row(att_out_g), row(post_mix_g), row(pre_ffn_g), row(post_ffn_g), row(ple_gate_b))
    return out.reshape(B, S, D)


def kernel(x, p, w_in, f_bias, sg_ln_g, sg_ln_b, sg_w, sg_b, att_out_g, sg_out_g, w_out,
           pre_mix_g, post_mix_g, pre_ffn_g, post_ffn_g, w_ff1, w_ff2, ple_w, ple_gate_w, ple_gate_b):
    h = x
    for i in range(p.shape[0]):
        h = _layer(h, p[i], w_in[i], f_bias[i], sg_ln_g[i], sg_ln_b[i], sg_w[i], sg_b[i],
                   att_out_g[i], sg_out_g[i], w_out[i], pre_mix_g[i], post_mix_g[i],
                   pre_ffn_g[i], post_ffn_g[i], w_ff1[i], w_ff2[i], ple_w[i], ple_gate_w[i],
                   ple_gate_b[i])
    return h
```

```python
import jax
import jax.numpy as jnp
from jax import lax
from jax.experimental import pallas as pl
from jax.experimental.pallas import tpu as pltpu

D_MODEL = 1024
PLE_DIM = 256
HEAD_DIM = 64
ATT_WIDTH = 512
N_HEADS = 8
SG_WIDTH = 512
N_SG_GROUPS = 8
CHUNK = 128
D_FF = 4096
EPS = 1e-6

LANES = 128
SUB = 8
HEAD_PAIRS = N_HEADS // 2
NEG_BIG = -1e30
LOG2E = 1.4426950408889634
N_PIECES = 3
PIECE_LANES = N_PIECES * N_HEADS
BIAS_LANES = 2 * PIECE_LANES

T_IN = 1024
T_SUB = 256
T_Q = 2048
T_K = 512
STAIRS = T_Q // T_K
ATT_UNROLL = 4
ATT_TAIL = (STAIRS - 2) % ATT_UNROLL
assert ATT_UNROLL % 2 == 0 and STAIRS % ATT_UNROLL == 0 and ATT_TAIL % 2 == 0
T_MLP = 512
T_MLP_SUB = 256
FF_CHUNK = 1024

VMEM_LIMIT = 56 * 1024 * 1024

F32 = jnp.float32
BF16 = jnp.bfloat16


def _rms(x, g):
    return x * lax.rsqrt(jnp.mean(x * x, axis=-1, keepdims=True) + EPS) * g


def _gelu_tanh(x):
    return x * (0.5 * (1.0 + jnp.tanh(0.7978845608028654 * (x + 0.044715 * (x * x * x)))))


def _split3(x):
    hi = x.astype(BF16).astype(F32)
    r = x - hi
    mid = r.astype(BF16).astype(F32)
    lo = (r - mid).astype(BF16).astype(F32)
    return hi, mid, lo


def _inproj_kernel(x_ref, g_ref, wqkv_ref, wf_ref, fb_ref, tril_ref, wuv_ref, lng_ref, lnb_ref,
                   sgw_ref, sgb_ref, sgog_ref, w1_ref, w2_ref, wg_ref, wo_ref,
                   qp_ref, kp_ref, vp_ref, ysg_ref, w1b_ref, w2b_ref, wgb_ref, wob_ref,
                   carry_ref, ysg_scr):
    t = pl.program_id(1)

    @pl.when(t == 0)
    def _():
        carry_ref[...] = jnp.zeros_like(carry_ref)

    w1b_ref[...] = w1_ref[...].astype(BF16)
    w2b_ref[...] = w2_ref[...].astype(BF16)
    wgb_ref[...] = wg_ref[...].astype(BF16)
    wob_ref[...] = wo_ref[...].astype(BF16)

    subs = [slice(r, r + T_SUB) for r in range(0, T_IN, T_SUB)]
    lane = lax.broadcasted_iota(jnp.int32, (T_SUB, LANES), 1)
    ab = [_rms(x_ref[0, rows, :], g_ref[...]).astype(BF16) for rows in subs]
    zuv = [jnp.dot(a, wuv_ref[...], preferred_element_type=F32) for a in ab]
    f = [jnp.dot(a, wf_ref[...], preferred_element_type=F32) + fb_ref[...] for a in ab]

    c3 = []
    carry = carry_ref[...]
    for fi in f:
        logf = jnp.minimum(fi, 0.0) - jnp.log1p(jnp.exp(-jnp.abs(fi)))
        hi, mid, lo = _split3(logf)
        pieces = jnp.where(lane < PIECE_LANES, hi, jnp.where(lane < 2 * PIECE_LANES, mid, lo)).astype(BF16)
        c3.append(jnp.dot(tril_ref[...], pieces, preferred_element_type=F32) + carry)
        carry = c3[-1][T_SUB - 1:T_SUB, :]
    carry_ref[...] = carry

    zqkv = []
    for rows, a, zuv_i in zip(subs, ab, zuv):
        zqkv.append(jnp.dot(a, wqkv_ref[...], preferred_element_type=F32))
        _spatial_gating(rows, zuv_i, lng_ref, lnb_ref, sgw_ref, sgb_ref, sgog_ref, ysg_ref, ysg_scr)
    for rows, zqkv_i, c3_i in zip(subs, zqkv, c3):
        _pack_heads(rows, lane, zqkv_i, c3_i, qp_ref, kp_ref, vp_ref)


def _spatial_gating(rows, zuv, lng_ref, lnb_ref, sgw_ref, sgb_ref, sgog_ref, ysg_ref, ysg_scr):
    r0 = rows.start
    u = _gelu_tanh(zuv[:, :SG_WIDTH])
    v = _gelu_tanh(zuv[:, SG_WIDTH:])
    vc = v - jnp.mean(v, axis=-1, keepdims=True)
    vln = vc * lax.rsqrt(jnp.mean(vc * vc, axis=-1, keepdims=True) + EPS) * lng_ref[...] + lnb_ref[...]
    vb = vln.astype(BF16)
    n_chunks = T_SUB // CHUNK
    ci = lax.broadcasted_iota(jnp.int32, (CHUNK, CHUNK), 0)
    cj = lax.broadcasted_iota(jnp.int32, (CHUNK, CHUNK), 1)
    lane_c = lax.broadcasted_iota(jnp.int32, (CHUNK, LANES), 1)
    for pb in range(SG_WIDTH // LANES):
        vcat = jnp.concatenate(
            [vb[c * CHUNK:(c + 1) * CHUNK, pb * LANES:(pb + 1) * LANES] for c in range(n_chunks)], axis=1)
        res = []
        for g in (2 * pb, 2 * pb + 1):
            w = jnp.where(cj <= ci, sgw_ref[g], 0.0).astype(BF16)
            res.append(jnp.dot(w, vcat, preferred_element_type=F32))
        for c in range(n_chunks):
            mixed = jnp.where(lane_c < HEAD_DIM,
                              res[0][:, c * LANES:(c + 1) * LANES],
                              res[1][:, c * LANES:(c + 1) * LANES])
            mixed = mixed + sgb_ref[:, pb * LANES:(pb + 1) * LANES]
            ysg_scr[r0 + c * CHUNK:r0 + (c + 1) * CHUNK, pb * LANES:(pb + 1) * LANES] = (
                u[c * CHUNK:(c + 1) * CHUNK, pb * LANES:(pb + 1) * LANES] * mixed)
    ysg_ref[0, rows, :] = _rms(ysg_scr[rows, :], sgog_ref[...]).astype(BF16)


def _pack_heads(rows, lane, zqkv, c3, qp_ref, kp_ref, vp_ref):
    csum = (c3 + pltpu.roll(c3, LANES - PIECE_LANES, axis=1)
            + pltpu.roll(c3, LANES - 2 * PIECE_LANES, axis=1))
    chi, cmid, clo = _split3(csum * LOG2E)
    cp = jnp.where(lane < N_HEADS, chi, jnp.where(lane < 2 * N_HEADS, cmid, clo))
    q_bias_lo = jnp.where(lane < PIECE_LANES, cp, jnp.where(lane < BIAS_LANES, 1.0, 0.0))
    k_bias_lo = jnp.where(lane < PIECE_LANES, 1.0,
                          jnp.where(lane < BIAS_LANES, -pltpu.roll(cp, PIECE_LANES, axis=1), 0.0))
    q_bias_hi = pltpu.roll(q_bias_lo, HEAD_DIM, axis=1)
    k_bias_hi = pltpu.roll(k_bias_lo, HEAD_DIM, axis=1)

    for h in range(N_HEADS):
        blk = h // 2
        even = (h % 2 == 0)
        in_head = (lane < HEAD_DIM) if even else (lane >= HEAD_DIM)
        own_bias = ((lane & (N_HEADS - 1)) == h) & ((lane & (HEAD_DIM - 1)) < BIAS_LANES)
        zq = zqkv[:, blk * LANES:(blk + 1) * LANES] * (HEAD_DIM ** -0.5 * LOG2E)
        zk = zqkv[:, ATT_WIDTH + blk * LANES:ATT_WIDTH + (blk + 1) * LANES]
        zv = zqkv[:, 2 * ATT_WIDTH + blk * LANES:2 * ATT_WIDTH + (blk + 1) * LANES]
        qp_ref[0, h, rows, :] = jnp.where(in_head, zq, q_bias_hi if even else q_bias_lo).astype(BF16)
        kp_ref[0, h, rows, :] = jnp.where(
            in_head, zk, jnp.where(own_bias, k_bias_hi if even else k_bias_lo, 0.0)).astype(BF16)
        vp_ref[0, h, rows, :] = jnp.where(in_head, zv, 1.0).astype(BF16)


def _attn_kernel(q_ref, k_ref, v_ref, o_ref, acc_ref, m_ref, s_scr, mc_scr, qt_scr):
    n_full = pl.program_id(2) * STAIRS
    for hh in range(2):
        qt_scr[hh] = q_ref[0, hh].astype(F32).T.astype(BF16)

    chunks = [slice(c * T_K, (c + 1) * T_K) for c in range(STAIRS)]

    def scores_chunk(b, slot, c, masked):
        start = pl.multiple_of(b * T_K, T_K)
        for hh in range(2):
            kb = k_ref[0, hh, pl.ds(start, T_K), :]
            s = jnp.dot(kb, qt_scr[hh, :, chunks[c]], preferred_element_type=F32)
            if masked:
                key = lax.broadcasted_iota(jnp.int32, (T_K, T_K), 0)
                qry = lax.broadcasted_iota(jnp.int32, (T_K, T_K), 1)
                s = jnp.where(key <= qry, s, NEG_BIG)
            s_scr[slot, hh, :, chunks[c]] = s
            mc_scr[slot, hh, :, chunks[c]] = jnp.broadcast_to(jnp.max(s, axis=0, keepdims=True), (SUB, T_K))

    def update_chunk(slot, c, first, vts):
        for hh in range(2):
            m_blk = mc_scr[slot, hh, :, chunks[c]]
            if first:
                m_new = m_blk
            else:
                m_old = m_ref[hh, :, chunks[c]]
                m_new = jnp.maximum(m_old, m_blk)
            p = jnp.exp2(s_scr[slot, hh, :, chunks[c]] - m_new[0:1, :]).astype(BF16)
            pv = jnp.dot(vts[hh], p, preferred_element_type=F32)
            if first:
                acc_ref[hh, :, chunks[c]] = pv
            else:
                alpha = jnp.exp2(m_old - m_new)
                acc_ref[hh, :, chunks[c]] = alpha[0:1, :] * acc_ref[hh, :, chunks[c]] + pv
            m_ref[hh, :, chunks[c]] = m_new

    def stage(nxt=None, cur=None):
        vts = None
        if cur is not None:
            start = pl.multiple_of(cur[0] * T_K, T_K)
            vts = [v_ref[0, hh, pl.ds(start, T_K), :].astype(F32).T.astype(BF16) for hh in range(2)]
        for c in range(STAIRS):
            if nxt is not None and c >= (nxt[2] or 0):
                scores_chunk(nxt[0], nxt[1], c, masked=(nxt[2] == c))
            if cur is not None and c >= cur[2]:
                update_chunk(cur[1], c, cur[3], vts)

    def staircase(first_blk, is_tile_start):
        for k in range(STAIRS - 1):
            stage(nxt=(first_blk + k + 1, (k + 1) % 2, k + 1),
                  cur=(first_blk + k, k % 2, k, is_tile_start and k == 0))
        stage(cur=(first_blk + STAIRS - 1, (STAIRS - 1) % 2, STAIRS - 1, False))

    @pl.when(n_full == 0)
    def _():
        stage(nxt=(0, 0, 0))
        staircase(0, True)

    @pl.when(n_full > 0)
    def _():
        stage(nxt=(0, 0, None))
        stage(nxt=(1, 1, None), cur=(0, 0, 0, True))

        def body(i, carry):
            b = 1 + ATT_UNROLL * i
            for u in range(ATT_UNROLL):
                stage(nxt=(b + u + 1, u % 2, None), cur=(b + u, (u + 1) % 2, 0, False))
            return carry

        lax.fori_loop(0, (n_full - 2) // ATT_UNROLL, body, 0)
        b = n_full - 1 - ATT_TAIL
        for u in range(ATT_TAIL):
            stage(nxt=(b + u + 1, u % 2, None), cur=(b + u, (u + 1) % 2, 0, False))
        stage(nxt=(n_full, 0, 0), cur=(n_full - 1, 1, 0, False))
        staircase(n_full, False)

    a0 = acc_ref[0]
    a1 = acc_ref[1]
    numer = jnp.concatenate([a0[:HEAD_DIM], a1[HEAD_DIM:]], axis=0)
    denom = jnp.concatenate([a0[HEAD_DIM:], a1[:HEAD_DIM]], axis=0)
    o_ref[0] = (numer / denom).T


def _mlp_kernel(x_ref, ya_ref, ys_ref, p_ref, wo_ref, w1_ref, w2_ref, wg_ref, wpe_ref,
                gatt_ref, gpm_ref, gpf_ref, gqf_ref, bg_ref, o_ref):
    subs = [slice(r, r + T_MLP_SUB) for r in range(0, T_MLP, T_MLP_SUB)]
    y = []
    for rows in subs:
        yan = _rms(ya_ref[rows, :], gatt_ref[...]).astype(BF16)
        y.append(jnp.dot(yan, wo_ref[:ATT_WIDTH, :], preferred_element_type=F32)
                 + jnp.dot(ys_ref[rows, :], wo_ref[ATT_WIDTH:, :], preferred_element_type=F32))
    h1 = [x_ref[rows, :] + _rms(yy, gpm_ref[...]) for rows, yy in zip(subs, y)]
    cn = [_rms(hh, gpf_ref[...]).astype(BF16) for hh in h1]
    ff = [None] * len(subs)
    for c in range(D_FF // FF_CHUNK):
        cols = slice(c * FF_CHUNK, (c + 1) * FF_CHUNK)
        for i in range(len(subs)):
            hid = jnp.dot(cn[i], w1_ref[:, cols], preferred_element_type=F32)
            hid = jnp.square(jnp.maximum(hid, 0.0)).astype(BF16)
            part = jnp.dot(hid, w2_ref[cols, :], preferred_element_type=F32)
            ff[i] = part if ff[i] is None else ff[i] + part
    for i, rows in enumerate(subs):
        h2 = h1[i] + _rms(ff[i], gqf_ref[...])
        gate = jax.nn.sigmoid(
            jnp.dot(h2.astype(BF16), wg_ref[...], preferred_element_type=F32) + bg_ref[...])
        pe = jnp.dot(p_ref[rows, :].astype(BF16), wpe_ref[...], preferred_element_type=F32)
        o_ref[rows, :] = h2 + gate * pe


def _layer(h, p, w_in, f_bias, sg_ln_g, sg_ln_b, sg_w, sg_b, att_out_g, sg_out_g, w_out,
           pre_mix_g, post_mix_g, pre_ffn_g, post_ffn_g, w_ff1, w_ff2, ple_w, ple_gate_w, ple_gate_b):
    B, S, D = h.shape
    R = B * S
    assert D == D_MODEL and w_in.shape == (D, 3 * ATT_WIDTH + N_HEADS + 2 * SG_WIDTH)
    assert S % T_Q == 0 and S % T_IN == 0 and R % T_MLP == 0 and D % (B * (S // T_IN)) == 0
    assert T_IN % T_SUB == 0 and T_SUB % CHUNK == 0 and T_MLP % T_MLP_SUB == 0 and D_FF % FF_CHUNK == 0
    row = lambda a: a.reshape(1, -1).astype(F32)

    wqkv = w_in[:, :3 * ATT_WIDTH].astype(BF16)
    wf8 = w_in[:, 3 * ATT_WIDTH:3 * ATT_WIDTH + N_HEADS]
    n_rep = N_PIECES * N_PIECES
    wf = jnp.pad(jnp.tile(wf8, (1, n_rep)), ((0, 0), (0, LANES - n_rep * N_HEADS))).astype(BF16)
    fb = jnp.pad(jnp.tile(f_bias, n_rep), (0, LANES - n_rep * N_HEADS)).reshape(1, LANES).astype(F32)
    tril = jnp.tril(jnp.ones((T_SUB, T_SUB), BF16))
    wuv = w_in[:, 3 * ATT_WIDTH + N_HEADS:].astype(BF16)
    sgb = jnp.repeat(jnp.transpose(sg_b), HEAD_DIM, axis=1).astype(F32)

    const2 = lambda shape: pl.BlockSpec(shape, lambda b, t: (0,) * len(shape))
    n_t = S // T_IN
    slab = lambda w: pl.BlockSpec((w.shape[0] // (B * n_t), w.shape[1]), lambda b, t: (b * n_t + t, 0))
    cast_ws = (w_ff1, w_ff2, ple_gate_w, w_out)
    qp, kp, vp, ysg, w1b, w2b, wgb, wob = pl.pallas_call(
        _inproj_kernel,
        grid=(B, S // T_IN),
        in_specs=[
            pl.BlockSpec((1, T_IN, D), lambda b, t: (b, t, 0)),
            const2((1, D)),
            const2((D, 3 * ATT_WIDTH)),
            const2((D, LANES)),
            const2((1, LANES)),
            const2((T_SUB, T_SUB)),
            const2((D, 2 * SG_WIDTH)),
            const2((1, SG_WIDTH)),
            const2((1, SG_WIDTH)),
            const2((N_SG_GROUPS, CHUNK, CHUNK)),
            const2((CHUNK, SG_WIDTH)),
            const2((1, SG_WIDTH)),
        ] + [slab(w) for w in cast_ws],
        out_specs=[
            pl.BlockSpec((1, N_HEADS, T_IN, LANES), lambda b, t: (b, 0, t, 0)),
            pl.BlockSpec((1, N_HEADS, T_IN, LANES), lambda b, t: (b, 0, t, 0)),
            pl.BlockSpec((1, N_HEADS, T_IN, LANES), lambda b, t: (b, 0, t, 0)),
            pl.BlockSpec((1, T_IN, SG_WIDTH), lambda b, t: (b, t, 0)),
        ] + [slab(w) for w in cast_ws],
        out_shape=[
            jax.ShapeDtypeStruct((B, N_HEADS, S, LANES), BF16),
            jax.ShapeDtypeStruct((B, N_HEADS, S, LANES), BF16),
            jax.ShapeDtypeStruct((B, N_HEADS, S, LANES), BF16),
            jax.ShapeDtypeStruct((B, S, SG_WIDTH), BF16),
        ] + [jax.ShapeDtypeStruct(w.shape, BF16) for w in cast_ws],
        scratch_shapes=[pltpu.VMEM((1, LANES), F32), pltpu.VMEM((T_IN, SG_WIDTH), F32)],
        compiler_params=pltpu.CompilerParams(
            dimension_semantics=("arbitrary", "arbitrary"), vmem_limit_bytes=VMEM_LIMIT),
        name="inproj",
    )(h, row(pre_mix_g), wqkv, wf, fb, tril, wuv, row(sg_ln_g), row(sg_ln_b), sg_w.astype(F32), sgb,
      row(sg_out_g), *cast_ws)

    yatt = pl.pallas_call(
        _attn_kernel,
        grid=(B, HEAD_PAIRS, S // T_Q),
        in_specs=[
            pl.BlockSpec((1, 2, T_Q, LANES), lambda b, hp, i: (b, hp, i, 0)),
            pl.BlockSpec((1, 2, S, LANES), lambda b, hp, i: (b, hp, 0, 0)),
            pl.BlockSpec((1, 2, S, LANES), lambda b, hp, i: (b, hp, 0, 0)),
        ],
        out_specs=pl.BlockSpec((1, T_Q, LANES), lambda b, hp, i: (b, i, hp)),
        out_shape=jax.ShapeDtypeStruct((B, S, ATT_WIDTH), F32),
        scratch_shapes=[pltpu.VMEM((2, LANES, T_Q), F32), pltpu.VMEM((2, SUB, T_Q), F32),
                        pltpu.VMEM((2, 2, T_K, T_Q), F32), pltpu.VMEM((2, 2, SUB, T_Q), F32),
                        pltpu.VMEM((2, LANES, T_Q), BF16)],
        compiler_params=pltpu.CompilerParams(
            dimension_semantics=("arbitrary", "arbitrary", "arbitrary"), vmem_limit_bytes=VMEM_LIMIT),
        name="fox_attn",
    )(qp, kp, vp)

    rows = lambda w: pl.BlockSpec((T_MLP, w), lambda i: (i, 0))
    const = lambda shape: pl.BlockSpec(shape, lambda i: (0,) * len(shape), pipeline_mode=pl.Buffered(1))
    out = pl.pallas_call(
        _mlp_kernel,
        grid=(R // T_MLP,),
        in_specs=[
            rows(D), rows(ATT_WIDTH), rows(SG_WIDTH), rows(PLE_DIM),
            const((ATT_WIDTH + SG_WIDTH, D)),
            const((D, D_FF)), const((D_FF, D)),
            const((D, D)), const((PLE_DIM, D)),
            const((1, ATT_WIDTH)), const((1, D)), const((1, D)), const((1, D)), const((1, D)),
        ],
        out_specs=rows(D),
        out_shape=jax.ShapeDtypeStruct((R, D), F32),
        compiler_params=pltpu.CompilerParams(
            dimension_semantics=("arbitrary",), vmem_limit_bytes=VMEM_LIMIT),
        name="mlp",
    )(h.reshape(R, D), yatt.reshape(R, ATT_WIDTH), ysg.reshape(R, SG_WIDTH), p.reshape(R, PLE_DIM),
      wob, w1b, w2b, wgb, ple_w.astype(BF16),
      row(att_out_g), row(post_mix_g), row(pre_ffn_g), row(post_ffn_g), row(ple_gate_b))
    return out.reshape(B, S, D)


def kernel(x, p, w_in, f_bias, sg_ln_g, sg_ln_b, sg_w, sg_b, att_out_g, sg_out_g, w_out,
           pre_mix_g, post_mix_g, pre_ffn_g, post_ffn_g, w_ff1, w_ff2, ple_w, ple_gate_w, ple_gate_b):
    h = x
    for i in range(p.shape[0]):
        h = _layer(h, p[i], w_in[i], f_bias[i], sg_ln_g[i], sg_ln_b[i], sg_w[i], sg_b[i],
                   att_out_g[i], sg_out_g[i], w_out[i], pre_mix_g[i], post_mix_g[i],
                   pre_ffn_g[i], post_ffn_g[i], w_ff1[i], w_ff2[i], ple_w[i], ple_gate_w[i],
                   ple_gate_b[i])
    return h
```

```python
import jax
import jax.numpy as jnp
from jax import lax
from jax.experimental import pallas as pl
from jax.experimental.pallas import tpu as pltpu

D_MODEL = 1024
PLE_DIM = 256
HEAD_DIM = 64
ATT_WIDTH = 512
N_HEADS = 8
SG_WIDTH = 512
N_SG_GROUPS = 8
CHUNK = 128
D_FF = 4096
EPS = 1e-6

LANES = 128
SUB = 8
HEAD_PAIRS = N_HEADS // 2
NEG_BIG = -1e30
LOG2E = 1.4426950408889634
N_PIECES = 3
PIECE_LANES = N_PIECES * N_HEADS
BIAS_LANES = 2 * PIECE_LANES

T_IN = 1024
T_SUB = 256
T_Q = 2048
T_K = 512
STAIRS = T_Q // T_K
ATT_UNROLL = 4
ATT_TAIL = (STAIRS - 2) % ATT_UNROLL
assert ATT_UNROLL % 2 == 0 and STAIRS % ATT_UNROLL == 0 and ATT_TAIL % 2 == 0
T_MLP = 512
T_MLP_SUB = 256
FF_CHUNK = 1024

VMEM_LIMIT = 56 * 1024 * 1024

F32 = jnp.float32
BF16 = jnp.bfloat16


def _rms(x, g):
    return x * lax.rsqrt(jnp.mean(x * x, axis=-1, keepdims=True) + EPS) * g


def _gelu_tanh(x):
    return x * (0.5 * (1.0 + jnp.tanh(0.7978845608028654 * (x + 0.044715 * (x * x * x)))))


def _split3(x):
    hi = x.astype(BF16).astype(F32)
    r = x - hi
    mid = r.astype(BF16).astype(F32)
    lo = (r - mid).astype(BF16).astype(F32)
    return hi, mid, lo


def _inproj_kernel(x_ref, g_ref, wqkv_ref, wf_ref, fb_ref, tril_ref, wuv_ref, lng_ref, lnb_ref,
                   sgw_ref, sgb_ref, sgog_ref, w1_ref, w2_ref, wg_ref, wo_ref,
                   qp_ref, kp_ref, vp_ref, ysg_ref, w1b_ref, w2b_ref, wgb_ref, wob_ref,
                   carry_ref, ysg_scr):
    t = pl.program_id(1)

    @pl.when(t == 0)
    def _():
        carry_ref[...] = jnp.zeros_like(carry_ref)

    w1b_ref[...] = w1_ref[...].astype(BF16)
    w2b_ref[...] = w2_ref[...].astype(BF16)
    wgb_ref[...] = wg_ref[...].astype(BF16)
    wob_ref[...] = wo_ref[...].astype(BF16)

    subs = [slice(r, r + T_SUB) for r in range(0, T_IN, T_SUB)]
    lane = lax.broadcasted_iota(jnp.int32, (T_SUB, LANES), 1)
    ab = [_rms(x_ref[0, rows, :], g_ref[...]).astype(BF16) for rows in subs]
    zuv = [jnp.dot(a, wuv_ref[...], preferred_element_type=F32) for a in ab]
    f = [jnp.dot(a, wf_ref[...], preferred_element_type=F32) + fb_ref[...] for a in ab]

    c3 = []
    carry = carry_ref[...]
    for fi in f:
        logf = jnp.minimum(fi, 0.0) - jnp.log1p(jnp.exp(-jnp.abs(fi)))
        hi, mid, lo = _split3(logf)
        pieces = jnp.where(lane < PIECE_LANES, hi, jnp.where(lane < 2 * PIECE_LANES, mid, lo)).astype(BF16)
        c3.append(jnp.dot(tril_ref[...], pieces, preferred_element_type=F32) + carry)
        carry = c3[-1][T_SUB - 1:T_SUB, :]
    carry_ref[...] = carry

    zqkv = []
    for rows, a, zuv_i in zip(subs, ab, zuv):
        zqkv.append(jnp.dot(a, wqkv_ref[...], preferred_element_type=F32))
        _spatial_gating(rows, zuv_i, lng_ref, lnb_ref, sgw_ref, sgb_ref, sgog_ref, ysg_ref, ysg_scr)
    for rows, zqkv_i, c3_i in zip(subs, zqkv, c3):
        _pack_heads(rows, lane, zqkv_i, c3_i, qp_ref, kp_ref, vp_ref)


def _spatial_gating(rows, zuv, lng_ref, lnb_ref, sgw_ref, sgb_ref, sgog_ref, ysg_ref, ysg_scr):
    r0 = rows.start
    u = _gelu_tanh(zuv[:, :SG_WIDTH])
    v = _gelu_tanh(zuv[:, SG_WIDTH:])
    vc = v - jnp.mean(v, axis=-1, keepdims=True)
    vln = vc * lax.rsqrt(jnp.mean(vc * vc, axis=-1, keepdims=True) + EPS) * lng_ref[...] + lnb_ref[...]
    vb = vln.astype(BF16)
    n_chunks = T_SUB // CHUNK
    ci = lax.broadcasted_iota(jnp.int32, (CHUNK, CHUNK), 0)
    cj = lax.broadcasted_iota(jnp.int32, (CHUNK, CHUNK), 1)
    lane_c = lax.broadcasted_iota(jnp.int32, (CHUNK, LANES), 1)
    for pb in range(SG_WIDTH // LANES):
        vcat = jnp.concatenate(
            [vb[c * CHUNK:(c + 1) * CHUNK, pb * LANES:(pb + 1) * LANES] for c in range(n_chunks)], axis=1)
        res = []
        for g in (2 * pb, 2 * pb + 1):
            w = jnp.where(cj <= ci, sgw_ref[g], 0.0).astype(BF16)
            res.append(jnp.dot(w, vcat, preferred_element_type=F32))
        for c in range(n_chunks):
            mixed = jnp.where(lane_c < HEAD_DIM,
                              res[0][:, c * LANES:(c + 1) * LANES],
                              res[1][:, c * LANES:(c + 1) * LANES])
            mixed = mixed + sgb_ref[:, pb * LANES:(pb + 1) * LANES]
            ysg_scr[r0 + c * CHUNK:r0 + (c + 1) * CHUNK, pb * LANES:(pb + 1) * LANES] = (
                u[c * CHUNK:(c + 1) * CHUNK, pb * LANES:(pb + 1) * LANES] * mixed)
    ysg_ref[0, rows, :] = _rms(ysg_scr[rows, :], sgog_ref[...]).astype(BF16)


def _pack_heads(rows, lane, zqkv, c3, qp_ref, kp_ref, vp_ref):
    csum = (c3 + pltpu.roll(c3, LANES - PIECE_LANES, axis=1)
            + pltpu.roll(c3, LANES - 2 * PIECE_LANES, axis=1))
    chi, cmid, clo = _split3(csum * LOG2E)
    cp = jnp.where(lane < N_HEADS, chi, jnp.where(lane < 2 * N_HEADS, cmid, clo))
    q_bias_lo = jnp.where(lane < PIECE_LANES, cp, jnp.where(lane < BIAS_LANES, 1.0, 0.0))
    k_bias_lo = jnp.where(lane < PIECE_LANES, 1.0,
                          jnp.where(lane < BIAS_LANES, -pltpu.roll(cp, PIECE_LANES, axis=1), 0.0))
    q_bias_hi = pltpu.roll(q_bias_lo, HEAD_DIM, axis=1)
    k_bias_hi = pltpu.roll(k_bias_lo, HEAD_DIM, axis=1)

    for h in range(N_HEADS):
        blk = h // 2
        even = (h % 2 == 0)
        in_head = (lane < HEAD_DIM) if even else (lane >= HEAD_DIM)
        own_bias = ((lane & (N_HEADS - 1)) == h) & ((lane & (HEAD_DIM - 1)) < BIAS_LANES)
        zq = zqkv[:, blk * LANES:(blk + 1) * LANES] * (HEAD_DIM ** -0.5 * LOG2E)
        zk = zqkv[:, ATT_WIDTH + blk * LANES:ATT_WIDTH + (blk + 1) * LANES]
        zv = zqkv[:, 2 * ATT_WIDTH + blk * LANES:2 * ATT_WIDTH + (blk + 1) * LANES]
        qp_ref[0, h, :, rows] = jnp.where(in_head, zq, q_bias_hi if even else q_bias_lo).T.astype(BF16)
        kp_ref[0, h, rows, :] = jnp.where(
            in_head, zk, jnp.where(own_bias, k_bias_hi if even else k_bias_lo, 0.0)).astype(BF16)
        vp_ref[0, h, rows, :] = jnp.where(in_head, zv, 1.0).astype(BF16)


def _attn_kernel(qt_ref, k_ref, v_ref, o_ref, acc_ref, m_ref, s_scr, mc_scr):
    n_full = pl.program_id(2) * STAIRS

    chunks = [slice(c * T_K, (c + 1) * T_K) for c in range(STAIRS)]

    def scores_chunk(b, slot, c, masked):
        start = pl.multiple_of(b * T_K, T_K)
        for hh in range(2):
            kb = k_ref[0, hh, pl.ds(start, T_K), :]
            s = jnp.dot(kb, qt_ref[0, hh, :, chunks[c]], preferred_element_type=F32)
            if masked:
                key = lax.broadcasted_iota(jnp.int32, (T_K, T_K), 0)
                qry = lax.broadcasted_iota(jnp.int32, (T_K, T_K), 1)
                s = jnp.where(key <= qry, s, NEG_BIG)
            s_scr[slot, hh, :, chunks[c]] = s
            mc_scr[slot, hh, :, chunks[c]] = jnp.broadcast_to(jnp.max(s, axis=0, keepdims=True), (SUB, T_K))

    def update_chunk(slot, c, first, vts):
        for hh in range(2):
            m_blk = mc_scr[slot, hh, :, chunks[c]]
            if first:
                m_new = m_blk
            else:
                m_old = m_ref[hh, :, chunks[c]]
                m_new = jnp.maximum(m_old, m_blk)
            p = jnp.exp2(s_scr[slot, hh, :, chunks[c]] - m_new[0:1, :]).astype(BF16)
            pv = jnp.dot(vts[hh], p, preferred_element_type=F32)
            if first:
                acc_ref[hh, :, chunks[c]] = pv
            else:
                alpha = jnp.exp2(m_old - m_new)
                acc_ref[hh, :, chunks[c]] = alpha[0:1, :] * acc_ref[hh, :, chunks[c]] + pv
            m_ref[hh, :, chunks[c]] = m_new

    def stage(nxt=None, cur=None):
        vts = None
        if cur is not None:
            start = pl.multiple_of(cur[0] * T_K, T_K)
            vts = [v_ref[0, hh, pl.ds(start, T_K), :].astype(F32).T.astype(BF16) for hh in range(2)]
        for c in range(STAIRS):
            if nxt is not None and c >= (nxt[2] or 0):
                scores_chunk(nxt[0], nxt[1], c, masked=(nxt[2] == c))
            if cur is not None and c >= cur[2]:
                update_chunk(cur[1], c, cur[3], vts)

    def staircase(first_blk, is_tile_start):
        for k in range(STAIRS - 1):
            stage(nxt=(first_blk + k + 1, (k + 1) % 2, k + 1),
                  cur=(first_blk + k, k % 2, k, is_tile_start and k == 0))
        stage(cur=(first_blk + STAIRS - 1, (STAIRS - 1) % 2, STAIRS - 1, False))

    @pl.when(n_full == 0)
    def _():
        stage(nxt=(0, 0, 0))
        staircase(0, True)

    @pl.when(n_full > 0)
    def _():
        stage(nxt=(0, 0, None))
        stage(nxt=(1, 1, None), cur=(0, 0, 0, True))

        def body(i, carry):
            b = 1 + ATT_UNROLL * i
            for u in range(ATT_UNROLL):
                stage(nxt=(b + u + 1, u % 2, None), cur=(b + u, (u + 1) % 2, 0, False))
            return carry

        lax.fori_loop(0, (n_full - 2) // ATT_UNROLL, body, 0)
        b = n_full - 1 - ATT_TAIL
        for u in range(ATT_TAIL):
            stage(nxt=(b + u + 1, u % 2, None), cur=(b + u, (u + 1) % 2, 0, False))
        stage(nxt=(n_full, 0, 0), cur=(n_full - 1, 1, 0, False))
        staircase(n_full, False)

    a0 = acc_ref[0]
    a1 = acc_ref[1]
    numer = jnp.concatenate([a0[:HEAD_DIM], a1[HEAD_DIM:]], axis=0)
    denom = jnp.concatenate([a0[HEAD_DIM:], a1[:HEAD_DIM]], axis=0)
    o_ref[0] = (numer / denom).T


def _mlp_kernel(x_ref, ya_ref, ys_ref, p_ref, wo_ref, w1_ref, w2_ref, wg_ref, wpe_ref,
                gatt_ref, gpm_ref, gpf_ref, gqf_ref, bg_ref, o_ref):
    subs = [slice(r, r + T_MLP_SUB) for r in range(0, T_MLP, T_MLP_SUB)]
    y = []
    for rows in subs:
        yan = _rms(ya_ref[rows, :], gatt_ref[...]).astype(BF16)
        y.append(jnp.dot(yan, wo_ref[:ATT_WIDTH, :], preferred_element_type=F32)
                 + jnp.dot(ys_ref[rows, :], wo_ref[ATT_WIDTH:, :], preferred_element_type=F32))
    h1 = [x_ref[rows, :] + _rms(yy, gpm_ref[...]) for rows, yy in zip(subs, y)]
    cn = [_rms(hh, gpf_ref[...]).astype(BF16) for hh in h1]
    ff = [None] * len(subs)
    for c in range(D_FF // FF_CHUNK):
        cols = slice(c * FF_CHUNK, (c + 1) * FF_CHUNK)
        for i in range(len(subs)):
            hid = jnp.dot(cn[i], w1_ref[:, cols], preferred_element_type=F32)
            hid = jnp.square(jnp.maximum(hid, 0.0)).astype(BF16)
            part = jnp.dot(hid, w2_ref[cols, :], preferred_element_type=F32)
            ff[i] = part if ff[i] is None else ff[i] + part
    for i, rows in enumerate(subs):
        h2 = h1[i] + _rms(ff[i], gqf_ref[...])
        gate = jax.nn.sigmoid(
            jnp.dot(h2.astype(BF16), wg_ref[...], preferred_element_type=F32) + bg_ref[...])
        pe = jnp.dot(p_ref[rows, :].astype(BF16), wpe_ref[...], preferred_element_type=F32)
        o_ref[rows, :] = h2 + gate * pe


def _layer(h, p, w_in, f_bias, sg_ln_g, sg_ln_b, sg_w, sg_b, att_out_g, sg_out_g, w_out,
           pre_mix_g, post_mix_g, pre_ffn_g, post_ffn_g, w_ff1, w_ff2, ple_w, ple_gate_w, ple_gate_b):
    B, S, D = h.shape
    R = B * S
    assert D == D_MODEL and w_in.shape == (D, 3 * ATT_WIDTH + N_HEADS + 2 * SG_WIDTH)
    assert S % T_Q == 0 and S % T_IN == 0 and R % T_MLP == 0 and D % (B * (S // T_IN)) == 0
    assert T_IN % T_SUB == 0 and T_SUB % CHUNK == 0 and T_MLP % T_MLP_SUB == 0 and D_FF % FF_CHUNK == 0
    row = lambda a: a.reshape(1, -1).astype(F32)

    wqkv = w_in[:, :3 * ATT_WIDTH].astype(BF16)
    wf8 = w_in[:, 3 * ATT_WIDTH:3 * ATT_WIDTH + N_HEADS]
    n_rep = N_PIECES * N_PIECES
    wf = jnp.pad(jnp.tile(wf8, (1, n_rep)), ((0, 0), (0, LANES - n_rep * N_HEADS))).astype(BF16)
    fb = jnp.pad(jnp.tile(f_bias, n_rep), (0, LANES - n_rep * N_HEADS)).reshape(1, LANES).astype(F32)
    tril = jnp.tril(jnp.ones((T_SUB, T_SUB), BF16))
    wuv = w_in[:, 3 * ATT_WIDTH + N_HEADS:].astype(BF16)
    sgb = jnp.repeat(jnp.transpose(sg_b), HEAD_DIM, axis=1).astype(F32)

    const2 = lambda shape: pl.BlockSpec(shape, lambda b, t: (0,) * len(shape))
    n_t = S // T_IN
    slab = lambda w: pl.BlockSpec((w.shape[0] // (B * n_t), w.shape[1]), lambda b, t: (b * n_t + t, 0))
    cast_ws = (w_ff1, w_ff2, ple_gate_w, w_out)
    qp, kp, vp, ysg, w1b, w2b, wgb, wob = pl.pallas_call(
        _inproj_kernel,
        grid=(B, S // T_IN),
        in_specs=[
            pl.BlockSpec((1, T_IN, D), lambda b, t: (b, t, 0)),
            const2((1, D)),
            const2((D, 3 * ATT_WIDTH)),
            const2((D, LANES)),
            const2((1, LANES)),
            const2((T_SUB, T_SUB)),
            const2((D, 2 * SG_WIDTH)),
            const2((1, SG_WIDTH)),
            const2((1, SG_WIDTH)),
            const2((N_SG_GROUPS, CHUNK, CHUNK)),
            const2((CHUNK, SG_WIDTH)),
            const2((1, SG_WIDTH)),
        ] + [slab(w) for w in cast_ws],
        out_specs=[
            pl.BlockSpec((1, N_HEADS, LANES, T_IN), lambda b, t: (b, 0, 0, t)),
            pl.BlockSpec((1, N_HEADS, T_IN, LANES), lambda b, t: (b, 0, t, 0)),
            pl.BlockSpec((1, N_HEADS, T_IN, LANES), lambda b, t: (b, 0, t, 0)),
            pl.BlockSpec((1, T_IN, SG_WIDTH), lambda b, t: (b, t, 0)),
        ] + [slab(w) for w in cast_ws],
        out_shape=[
            jax.ShapeDtypeStruct((B, N_HEADS, LANES, S), BF16),
            jax.ShapeDtypeStruct((B, N_HEADS, S, LANES), BF16),
            jax.ShapeDtypeStruct((B, N_HEADS, S, LANES), BF16),
            jax.ShapeDtypeStruct((B, S, SG_WIDTH), BF16),
        ] + [jax.ShapeDtypeStruct(w.shape, BF16) for w in cast_ws],
        scratch_shapes=[pltpu.VMEM((1, LANES), F32), pltpu.VMEM((T_IN, SG_WIDTH), F32)],
        compiler_params=pltpu.CompilerParams(
            dimension_semantics=("arbitrary", "arbitrary"), vmem_limit_bytes=VMEM_LIMIT),
        name="inproj",
    )(h, row(pre_mix_g), wqkv, wf, fb, tril, wuv, row(sg_ln_g), row(sg_ln_b), sg_w.astype(F32), sgb,
      row(sg_out_g), *cast_ws)

    yatt = pl.pallas_call(
        _attn_kernel,
        grid=(B, HEAD_PAIRS, S // T_Q),
        in_specs=[
            pl.BlockSpec((1, 2, LANES, T_Q), lambda b, hp, i: (b, hp, 0, i)),
            pl.BlockSpec((1, 2, S, LANES), lambda b, hp, i: (b, hp, 0, 0)),
            pl.BlockSpec((1, 2, S, LANES), lambda b, hp, i: (b, hp, 0, 0)),
        ],
        out_specs=pl.BlockSpec((1, T_Q, LANES), lambda b, hp, i: (b, i, hp)),
        out_shape=jax.ShapeDtypeStruct((B, S, ATT_WIDTH), F32),
        scratch_shapes=[pltpu.VMEM((2, LANES, T_Q), F32), pltpu.VMEM((2, SUB, T_Q), F32),
                        pltpu.VMEM((2, 2, T_K, T_Q), F32), pltpu.VMEM((2, 2, SUB, T_Q), F32)],
        compiler_params=pltpu.CompilerParams(
            dimension_semantics=("arbitrary", "arbitrary", "arbitrary"), vmem_limit_bytes=VMEM_LIMIT),
        name="fox_attn",
    )(qp, kp, vp)

    rows = lambda w: pl.BlockSpec((T_MLP, w), lambda i: (i, 0))
    const = lambda shape: pl.BlockSpec(shape, lambda i: (0,) * len(shape), pipeline_mode=pl.Buffered(1))
    out = pl.pallas_call(
        _mlp_kernel,
        grid=(R // T_MLP,),
        in_specs=[
            rows(D), rows(ATT_WIDTH), rows(SG_WIDTH), rows(PLE_DIM),
            const((ATT_WIDTH + SG_WIDTH, D)),
            const((D, D_FF)), const((D_FF, D)),
            const((D, D)), const((PLE_DIM, D)),
            const((1, ATT_WIDTH)), const((1, D)), const((1, D)), const((1, D)), const((1, D)),
        ],
        out_specs=rows(D),
        out_shape=jax.ShapeDtypeStruct((R, D), F32),
        compiler_params=pltpu.CompilerParams(
            dimension_semantics=("arbitrary",), vmem_limit_bytes=VMEM_LIMIT),
        name="mlp",
    )(h.reshape(R, D), yatt.reshape(R, ATT_WIDTH), ysg.reshape(R, SG_WIDTH), p.reshape(R, PLE_DIM),
      wob, w1b, w2b, wgb, ple_w.astype(BF16),
      row(att_out_g), row(post_mix_g), row(pre_ffn_g), row(post_ffn_g), row(ple_gate_b))
    return out.reshape(B, S, D)


def kernel(x, p, w_in, f_bias, sg_ln_g, sg_ln_b, sg_w, sg_b, att_out_g, sg_out_g, w_out,
           pre_mix_g, post_mix_g, pre_ffn_g, post_ffn_g, w_ff1, w_ff2, ple_w, ple_gate_w, ple_gate_b):
    h = x
    for i in range(p.shape[0]):
        h = _layer(h, p[i], w_in[i], f_bias[i], sg_ln_g[i], sg_ln_b[i], sg_w[i], sg_b[i],
                   att_out_g[i], sg_out_g[i], w_out[i], pre_mix_g[i], post_mix_g[i],
                   pre_ffn_g[i], post_ffn_g[i], w_ff1[i], w_ff2[i], ple_w[i], ple_gate_w[i],
                   ple_gate_b[i])
    return h
```

```python
import jax
import jax.numpy as jnp
from jax import lax
from jax.experimental import pallas as pl
from jax.experimental.pallas import tpu as pltpu

D_MODEL = 1024
PLE_DIM = 256
HEAD_DIM = 64
ATT_WIDTH = 512
N_HEADS = 8
SG_WIDTH = 512
N_SG_GROUPS = 8
CHUNK = 128
D_FF = 4096
EPS = 1e-6

LANES = 128
SUB = 8
HEAD_PAIRS = N_HEADS // 2
NEG_BIG = -1e30
LOG2E = 1.4426950408889634
N_PIECES = 3
PIECE_LANES = N_PIECES * N_HEADS
BIAS_LANES = 2 * PIECE_LANES

T_IN = 1024
T_SUB = 256
T_Q = 2048
T_K = 512
STAIRS = T_Q // T_K
PV_ROWS = HEAD_DIM + 16
ATT_UNROLL = 4
ATT_TAIL = (STAIRS - 2) % ATT_UNROLL
assert ATT_UNROLL % 2 == 0 and STAIRS % ATT_UNROLL == 0 and ATT_TAIL % 2 == 0
T_MLP = 512
T_MLP_SUB = 256
FF_CHUNK = 1024

VMEM_LIMIT = 56 * 1024 * 1024

F32 = jnp.float32
BF16 = jnp.bfloat16


def _rms(x, g):
    return x * lax.rsqrt(jnp.mean(x * x, axis=-1, keepdims=True) + EPS) * g


def _gelu_tanh(x):
    return x * (0.5 * (1.0 + jnp.tanh(0.7978845608028654 * (x + 0.044715 * (x * x * x)))))


def _split3(x):
    hi = x.astype(BF16).astype(F32)
    r = x - hi
    mid = r.astype(BF16).astype(F32)
    lo = (r - mid).astype(BF16).astype(F32)
    return hi, mid, lo


def _inproj_kernel(x_ref, g_ref, wqkv_ref, wf_ref, fb_ref, tril_ref, wuv_ref, lng_ref, lnb_ref,
                   sgw_ref, sgb_ref, sgog_ref, w1_ref, w2_ref, wg_ref, wo_ref,
                   qp_ref, kp_ref, vp_ref, ysg_ref, w1b_ref, w2b_ref, wgb_ref, wob_ref,
                   carry_ref, ysg_scr):
    t = pl.program_id(1)

    @pl.when(t == 0)
    def _():
        carry_ref[...] = jnp.zeros_like(carry_ref)

    w1b_ref[...] = w1_ref[...].astype(BF16)
    w2b_ref[...] = w2_ref[...].astype(BF16)
    wgb_ref[...] = wg_ref[...].astype(BF16)
    wob_ref[...] = wo_ref[...].astype(BF16)

    subs = [slice(r, r + T_SUB) for r in range(0, T_IN, T_SUB)]
    lane = lax.broadcasted_iota(jnp.int32, (T_SUB, LANES), 1)
    ab = [_rms(x_ref[0, rows, :], g_ref[...]).astype(BF16) for rows in subs]
    zuv = [jnp.dot(a, wuv_ref[...], preferred_element_type=F32) for a in ab]
    f = [jnp.dot(a, wf_ref[...], preferred_element_type=F32) + fb_ref[...] for a in ab]

    c3 = []
    carry = carry_ref[...]
    for fi in f:
        logf = jnp.minimum(fi, 0.0) - jnp.log1p(jnp.exp(-jnp.abs(fi)))
        hi, mid, lo = _split3(logf)
        pieces = jnp.where(lane < PIECE_LANES, hi, jnp.where(lane < 2 * PIECE_LANES, mid, lo)).astype(BF16)
        c3.append(jnp.dot(tril_ref[...], pieces, preferred_element_type=F32) + carry)
        carry = c3[-1][T_SUB - 1:T_SUB, :]
    carry_ref[...] = carry

    zqkv = []
    for rows, a, zuv_i in zip(subs, ab, zuv):
        zqkv.append(jnp.dot(a, wqkv_ref[...], preferred_element_type=F32))
        _spatial_gating(rows, zuv_i, lng_ref, lnb_ref, sgw_ref, sgb_ref, sgog_ref, ysg_ref, ysg_scr)
    for rows, zqkv_i, c3_i in zip(subs, zqkv, c3):
        _pack_heads(rows, lane, zqkv_i, c3_i, qp_ref, kp_ref, vp_ref)


def _spatial_gating(rows, zuv, lng_ref, lnb_ref, sgw_ref, sgb_ref, sgog_ref, ysg_ref, ysg_scr):
    r0 = rows.start
    u = _gelu_tanh(zuv[:, :SG_WIDTH])
    v = _gelu_tanh(zuv[:, SG_WIDTH:])
    vc = v - jnp.mean(v, axis=-1, keepdims=True)
    vln = vc * lax.rsqrt(jnp.mean(vc * vc, axis=-1, keepdims=True) + EPS) * lng_ref[...] + lnb_ref[...]
    vb = vln.astype(BF16)
    n_chunks = T_SUB // CHUNK
    ci = lax.broadcasted_iota(jnp.int32, (CHUNK, CHUNK), 0)
    cj = lax.broadcasted_iota(jnp.int32, (CHUNK, CHUNK), 1)
    lane_c = lax.broadcasted_iota(jnp.int32, (CHUNK, LANES), 1)
    for pb in range(SG_WIDTH // LANES):
        vcat = jnp.concatenate(
            [vb[c * CHUNK:(c + 1) * CHUNK, pb * LANES:(pb + 1) * LANES] for c in range(n_chunks)], axis=1)
        res = []
        for g in (2 * pb, 2 * pb + 1):
            w = jnp.where(cj <= ci, sgw_ref[g], 0.0).astype(BF16)
            res.append(jnp.dot(w, vcat, preferred_element_type=F32))
        for c in range(n_chunks):
            mixed = jnp.where(lane_c < HEAD_DIM,
                              res[0][:, c * LANES:(c + 1) * LANES],
                              res[1][:, c * LANES:(c + 1) * LANES])
            mixed = mixed + sgb_ref[:, pb * LANES:(pb + 1) * LANES]
            ysg_scr[r0 + c * CHUNK:r0 + (c + 1) * CHUNK, pb * LANES:(pb + 1) * LANES] = (
                u[c * CHUNK:(c + 1) * CHUNK, pb * LANES:(pb + 1) * LANES] * mixed)
    ysg_ref[0, rows, :] = _rms(ysg_scr[rows, :], sgog_ref[...]).astype(BF16)


def _pack_heads(rows, lane, zqkv, c3, qp_ref, kp_ref, vp_ref):
    csum = (c3 + pltpu.roll(c3, LANES - PIECE_LANES, axis=1)
            + pltpu.roll(c3, LANES - 2 * PIECE_LANES, axis=1))
    chi, cmid, clo = _split3(csum * LOG2E)
    cp = jnp.where(lane < N_HEADS, chi, jnp.where(lane < 2 * N_HEADS, cmid, clo))
    q_bias_lo = jnp.where(lane < PIECE_LANES, cp, jnp.where(lane < BIAS_LANES, 1.0, 0.0))
    k_bias_lo = jnp.where(lane < PIECE_LANES, 1.0,
                          jnp.where(lane < BIAS_LANES, -pltpu.roll(cp, PIECE_LANES, axis=1), 0.0))
    q_bias_hi = pltpu.roll(q_bias_lo, HEAD_DIM, axis=1)
    k_bias_hi = pltpu.roll(k_bias_lo, HEAD_DIM, axis=1)

    for h in range(N_HEADS):
        blk = h // 2
        even = (h % 2 == 0)
        in_head = (lane < HEAD_DIM) if even else (lane >= HEAD_DIM)
        own_bias = ((lane & (N_HEADS - 1)) == h) & ((lane & (HEAD_DIM - 1)) < BIAS_LANES)
        zq = zqkv[:, blk * LANES:(blk + 1) * LANES] * (HEAD_DIM ** -0.5 * LOG2E)
        zk = zqkv[:, ATT_WIDTH + blk * LANES:ATT_WIDTH + (blk + 1) * LANES]
        zv = zqkv[:, 2 * ATT_WIDTH + blk * LANES:2 * ATT_WIDTH + (blk + 1) * LANES]
        qp_ref[0, h, :, rows] = jnp.where(in_head, zq, q_bias_hi if even else q_bias_lo).T.astype(BF16)
        kp_ref[0, h, rows, :] = jnp.where(
            in_head, zk, jnp.where(own_bias, k_bias_hi if even else k_bias_lo, 0.0)).astype(BF16)
        vp_ref[0, h, :, rows] = jnp.where(in_head, zv, 1.0).T.astype(BF16)


def _attn_kernel(qt_ref, k_ref, vt_ref, o_ref, acc_ref, m_ref, s_scr, mc_scr):
    n_full = pl.program_id(2) * STAIRS

    chunks = [slice(c * T_K, (c + 1) * T_K) for c in range(STAIRS)]

    def scores_chunk(slot, c, masked, kbs):
        for hh in range(2):
            s = jnp.dot(kbs[hh], qt_ref[0, hh, :, chunks[c]], preferred_element_type=F32)
            if masked:
                key = lax.broadcasted_iota(jnp.int32, (T_K, T_K), 0)
                qry = lax.broadcasted_iota(jnp.int32, (T_K, T_K), 1)
                s = jnp.where(key <= qry, s, NEG_BIG)
            s_scr[slot, hh, :, chunks[c]] = s
            mc_scr[slot, hh, :, chunks[c]] = jnp.broadcast_to(jnp.max(s, axis=0, keepdims=True), (SUB, T_K))

    def update_chunk(slot, c, first, vts):
        for hh in range(2):
            m_blk = mc_scr[slot, hh, :, chunks[c]]
            if first:
                m_new = m_blk
            else:
                m_old = m_ref[hh, :, chunks[c]]
                m_new = jnp.maximum(m_old, m_blk)
            p = jnp.exp2(s_scr[slot, hh, :, chunks[c]] - m_new[0:1, :]).astype(BF16)
            pv = jnp.dot(vts[hh], p, preferred_element_type=F32)
            if first:
                acc_ref[hh, :, chunks[c]] = pv
            else:
                alpha = jnp.exp2(m_old - m_new)
                acc_ref[hh, :, chunks[c]] = alpha[0:1, :] * acc_ref[hh, :, chunks[c]] + pv
            m_ref[hh, :, chunks[c]] = m_new

    def stage(nxt=None, cur=None):
        kbs = vts = None
        if nxt is not None:
            start = pl.multiple_of(nxt[0] * T_K, T_K)
            kbs = [k_ref[0, hh, pl.ds(start, T_K), :] for hh in range(2)]
        if cur is not None:
            start = pl.multiple_of(cur[0] * T_K, T_K)
            vts = [vt_ref[0, 0, :PV_ROWS, pl.ds(start, T_K)], vt_ref[0, 1, LANES - PV_ROWS:, pl.ds(start, T_K)]]
        for c in range(STAIRS):
            if nxt is not None and c >= (nxt[2] or 0):
                scores_chunk(nxt[1], c, nxt[2] == c, kbs)
            if cur is not None and c >= cur[2]:
                update_chunk(cur[1], c, cur[3], vts)

    def staircase(first_blk, is_tile_start):
        for k in range(STAIRS - 1):
            stage(nxt=(first_blk + k + 1, (k + 1) % 2, k + 1),
                  cur=(first_blk + k, k % 2, k, is_tile_start and k == 0))
        stage(cur=(first_blk + STAIRS - 1, (STAIRS - 1) % 2, STAIRS - 1, False))

    @pl.when(n_full == 0)
    def _():
        stage(nxt=(0, 0, 0))
        staircase(0, True)

    @pl.when(n_full > 0)
    def _():
        stage(nxt=(0, 0, None))
        stage(nxt=(1, 1, None), cur=(0, 0, 0, True))

        def body(i, carry):
            b = 1 + ATT_UNROLL * i
            for u in range(ATT_UNROLL):
                stage(nxt=(b + u + 1, u % 2, None), cur=(b + u, (u + 1) % 2, 0, False))
            return carry

        lax.fori_loop(0, (n_full - 2) // ATT_UNROLL, body, 0)
        b = n_full - 1 - ATT_TAIL
        for u in range(ATT_TAIL):
            stage(nxt=(b + u + 1, u % 2, None), cur=(b + u, (u + 1) % 2, 0, False))
        stage(nxt=(n_full, 0, 0), cur=(n_full - 1, 1, 0, False))
        staircase(n_full, False)

    a0 = acc_ref[0]
    a1 = acc_ref[1]
    out0 = a0[:HEAD_DIM] / a0[HEAD_DIM:HEAD_DIM + 1]
    out1 = a1[PV_ROWS - HEAD_DIM:] / a1[0:1]
    o_ref[0] = jnp.concatenate([out0, out1], axis=0).T


def _mlp_kernel(x_ref, ya_ref, ys_ref, p_ref, wo_ref, w1_ref, w2_ref, wg_ref, wpe_ref,
                gatt_ref, gpm_ref, gpf_ref, gqf_ref, bg_ref, o_ref):
    subs = [slice(r, r + T_MLP_SUB) for r in range(0, T_MLP, T_MLP_SUB)]
    y = []
    for rows in subs:
        yan = _rms(ya_ref[rows, :], gatt_ref[...]).astype(BF16)
        y.append(jnp.dot(yan, wo_ref[:ATT_WIDTH, :], preferred_element_type=F32)
                 + jnp.dot(ys_ref[rows, :], wo_ref[ATT_WIDTH:, :], preferred_element_type=F32))
    h1 = [x_ref[rows, :] + _rms(yy, gpm_ref[...]) for rows, yy in zip(subs, y)]
    cn = [_rms(hh, gpf_ref[...]).astype(BF16) for hh in h1]
    ff = [None] * len(subs)
    for c in range(D_FF // FF_CHUNK):
        cols = slice(c * FF_CHUNK, (c + 1) * FF_CHUNK)
        for i in range(len(subs)):
            hid = jnp.dot(cn[i], w1_ref[:, cols], preferred_element_type=F32)
            hid = jnp.square(jnp.maximum(hid, 0.0)).astype(BF16)
            part = jnp.dot(hid, w2_ref[cols, :], preferred_element_type=F32)
            ff[i] = part if ff[i] is None else ff[i] + part
    for i, rows in enumerate(subs):
        h2 = h1[i] + _rms(ff[i], gqf_ref[...])
        gate = jax.nn.sigmoid(
            jnp.dot(h2.astype(BF16), wg_ref[...], preferred_element_type=F32) + bg_ref[...])
        pe = jnp.dot(p_ref[rows, :].astype(BF16), wpe_ref[...], preferred_element_type=F32)
        o_ref[rows, :] = h2 + gate * pe


def _layer(h, p, w_in, f_bias, sg_ln_g, sg_ln_b, sg_w, sg_b, att_out_g, sg_out_g, w_out,
           pre_mix_g, post_mix_g, pre_ffn_g, post_ffn_g, w_ff1, w_ff2, ple_w, ple_gate_w, ple_gate_b):
    B, S, D = h.shape
    R = B * S
    assert D == D_MODEL and w_in.shape == (D, 3 * ATT_WIDTH + N_HEADS + 2 * SG_WIDTH)
    assert S % T_Q == 0 and S % T_IN == 0 and R % T_MLP == 0 and D % (B * (S // T_IN)) == 0
    assert T_IN % T_SUB == 0 and T_SUB % CHUNK == 0 and T_MLP % T_MLP_SUB == 0 and D_FF % FF_CHUNK == 0
    row = lambda a: a.reshape(1, -1).astype(F32)

    wqkv = w_in[:, :3 * ATT_WIDTH].astype(BF16)
    wf8 = w_in[:, 3 * ATT_WIDTH:3 * ATT_WIDTH + N_HEADS]
    n_rep = N_PIECES * N_PIECES
    wf = jnp.pad(jnp.tile(wf8, (1, n_rep)), ((0, 0), (0, LANES - n_rep * N_HEADS))).astype(BF16)
    fb = jnp.pad(jnp.tile(f_bias, n_rep), (0, LANES - n_rep * N_HEADS)).reshape(1, LANES).astype(F32)
    tril = jnp.tril(jnp.ones((T_SUB, T_SUB), BF16))
    wuv = w_in[:, 3 * ATT_WIDTH + N_HEADS:].astype(BF16)
    sgb = jnp.repeat(jnp.transpose(sg_b), HEAD_DIM, axis=1).astype(F32)

    const2 = lambda shape: pl.BlockSpec(shape, lambda b, t: (0,) * len(shape))
    n_t = S // T_IN
    slab = lambda w: pl.BlockSpec((w.shape[0] // (B * n_t), w.shape[1]), lambda b, t: (b * n_t + t, 0))
    cast_ws = (w_ff1, w_ff2, ple_gate_w, w_out)
    qp, kp, vp, ysg, w1b, w2b, wgb, wob = pl.pallas_call(
        _inproj_kernel,
        grid=(B, S // T_IN),
        in_specs=[
            pl.BlockSpec((1, T_IN, D), lambda b, t: (b, t, 0)),
            const2((1, D)),
            const2((D, 3 * ATT_WIDTH)),
            const2((D, LANES)),
            const2((1, LANES)),
            const2((T_SUB, T_SUB)),
            const2((D, 2 * SG_WIDTH)),
            const2((1, SG_WIDTH)),
            const2((1, SG_WIDTH)),
            const2((N_SG_GROUPS, CHUNK, CHUNK)),
            const2((CHUNK, SG_WIDTH)),
            const2((1, SG_WIDTH)),
        ] + [slab(w) for w in cast_ws],
        out_specs=[
            pl.BlockSpec((1, N_HEADS, LANES, T_IN), lambda b, t: (b, 0, 0, t)),
            pl.BlockSpec((1, N_HEADS, T_IN, LANES), lambda b, t: (b, 0, t, 0)),
            pl.BlockSpec((1, N_HEADS, LANES, T_IN), lambda b, t: (b, 0, 0, t)),
            pl.BlockSpec((1, T_IN, SG_WIDTH), lambda b, t: (b, t, 0)),
        ] + [slab(w) for w in cast_ws],
        out_shape=[
            jax.ShapeDtypeStruct((B, N_HEADS, LANES, S), BF16),
            jax.ShapeDtypeStruct((B, N_HEADS, S, LANES), BF16),
            jax.ShapeDtypeStruct((B, N_HEADS, LANES, S), BF16),
            jax.ShapeDtypeStruct((B, S, SG_WIDTH), BF16),
        ] + [jax.ShapeDtypeStruct(w.shape, BF16) for w in cast_ws],
        scratch_shapes=[pltpu.VMEM((1, LANES), F32), pltpu.VMEM((T_IN, SG_WIDTH), F32)],
        compiler_params=pltpu.CompilerParams(
            dimension_semantics=("arbitrary", "arbitrary"), vmem_limit_bytes=VMEM_LIMIT),
        name="inproj",
    )(h, row(pre_mix_g), wqkv, wf, fb, tril, wuv, row(sg_ln_g), row(sg_ln_b), sg_w.astype(F32), sgb,
      row(sg_out_g), *cast_ws)

    yatt = pl.pallas_call(
        _attn_kernel,
        grid=(B, HEAD_PAIRS, S // T_Q),
        in_specs=[
            pl.BlockSpec((1, 2, LANES, T_Q), lambda b, hp, i: (b, hp, 0, i)),
            pl.BlockSpec((1, 2, S, LANES), lambda b, hp, i: (b, hp, 0, 0)),
            pl.BlockSpec((1, 2, LANES, S), lambda b, hp, i: (b, hp, 0, 0)),
        ],
        out_specs=pl.BlockSpec((1, T_Q, LANES), lambda b, hp, i: (b, i, hp)),
        out_shape=jax.ShapeDtypeStruct((B, S, ATT_WIDTH), F32),
        scratch_shapes=[pltpu.VMEM((2, PV_ROWS, T_Q), F32), pltpu.VMEM((2, SUB, T_Q), F32),
                        pltpu.VMEM((2, 2, T_K, T_Q), F32), pltpu.VMEM((2, 2, SUB, T_Q), F32)],
        compiler_params=pltpu.CompilerParams(
            dimension_semantics=("arbitrary", "arbitrary", "arbitrary"), vmem_limit_bytes=VMEM_LIMIT),
        name="fox_attn",
    )(qp, kp, vp)

    rows = lambda w: pl.BlockSpec((T_MLP, w), lambda i: (i, 0))
    const = lambda shape: pl.BlockSpec(shape, lambda i: (0,) * len(shape), pipeline_mode=pl.Buffered(1))
    out = pl.pallas_call(
        _mlp_kernel,
        grid=(R // T_MLP,),
        in_specs=[
            rows(D), rows(ATT_WIDTH), rows(SG_WIDTH), rows(PLE_DIM),
            const((ATT_WIDTH + SG_WIDTH, D)),
            const((D, D_FF)), const((D_FF, D)),
            const((D, D)), const((PLE_DIM, D)),
            const((1, ATT_WIDTH)), const((1, D)), const((1, D)), const((1, D)), const((1, D)),
        ],
        out_specs=rows(D),
        out_shape=jax.ShapeDtypeStruct((R, D), F32),
        compiler_params=pltpu.CompilerParams(
            dimension_semantics=("arbitrary",), vmem_limit_bytes=VMEM_LIMIT),
        name="mlp",
    )(h.reshape(R, D), yatt.reshape(R, ATT_WIDTH), ysg.reshape(R, SG_WIDTH), p.reshape(R, PLE_DIM),
      wob, w1b, w2b, wgb, ple_w.astype(BF16),
      row(att_out_g), row(post_mix_g), row(pre_ffn_g), row(post_ffn_g), row(ple_gate_b))
    return out.reshape(B, S, D)


def kernel(x, p, w_in, f_bias, sg_ln_g, sg_ln_b, sg_w, sg_b, att_out_g, sg_out_g, w_out,
           pre_mix_g, post_mix_g, pre_ffn_g, post_ffn_g, w_ff1, w_ff2, ple_w, ple_gate_w, ple_gate_b):
    h = x
    for i in range(p.shape[0]):
        h = _layer(h, p[i], w_in[i], f_bias[i], sg_ln_g[i], sg_ln_b[i], sg_w[i], sg_b[i],
                   att_out_g[i], sg_out_g[i], w_out[i], pre_mix_g[i], post_mix_g[i],
                   pre_ffn_g[i], post_ffn_g[i], w_ff1[i], w_ff2[i], ple_w[i], ple_gate_w[i],
                   ple_gate_b[i])
    return h
```

```python
import jax
import jax.numpy as jnp
from jax import lax
from jax.experimental import pallas as pl
from jax.experimental.pallas import tpu as pltpu

D_MODEL = 1024
PLE_DIM = 256
HEAD_DIM = 64
ATT_WIDTH = 512
N_HEADS = 8
SG_WIDTH = 512
N_SG_GROUPS = 8
CHUNK = 128
D_FF = 4096
EPS = 1e-6

LANES = 128
SUB = 8
HEAD_PAIRS = N_HEADS // 2
NEG_BIG = -1e30
LOG2E = 1.4426950408889634
N_PIECES = 3
PIECE_LANES = N_PIECES * N_HEADS
BIAS_LANES = 2 * PIECE_LANES

T_IN = 1024
T_SUB = 256
T_Q = 2048
T_K = 512
STAIRS = T_Q // T_K
ATT_UNROLL = 4
ATT_TAIL = (STAIRS - 2) % ATT_UNROLL
assert ATT_UNROLL % 2 == 0 and STAIRS % ATT_UNROLL == 0 and ATT_TAIL % 2 == 0
T_MLP = 1024
T_MLP_SUB = 256
FF_CHUNK = 1024

VMEM_LIMIT = 56 * 1024 * 1024
VMEM_LIMIT_MLP = 62 * 1024 * 1024

F32 = jnp.float32
BF16 = jnp.bfloat16


def _rms(x, g):
    return x * lax.rsqrt(jnp.mean(x * x, axis=-1, keepdims=True) + EPS) * g


def _gelu_tanh(x):
    return x * (0.5 * (1.0 + jnp.tanh(0.7978845608028654 * (x + 0.044715 * (x * x * x)))))


def _split3(x):
    hi = x.astype(BF16).astype(F32)
    r = x - hi
    mid = r.astype(BF16).astype(F32)
    lo = (r - mid).astype(BF16).astype(F32)
    return hi, mid, lo


def _inproj_kernel(x_ref, g_ref, wqkv_ref, wf_ref, fb_ref, tril_ref, wuv_ref, lng_ref, lnb_ref,
                   sgw_ref, sgb_ref, sgog_ref, w1_ref, w2_ref, wg_ref, wo_ref,
                   qp_ref, kp_ref, vp_ref, ysg_ref, w1b_ref, w2b_ref, wgb_ref, wob_ref,
                   carry_ref, ysg_scr):
    t = pl.program_id(1)

    @pl.when(t == 0)
    def _():
        carry_ref[...] = jnp.zeros_like(carry_ref)

    w1b_ref[...] = w1_ref[...].astype(BF16)
    w2b_ref[...] = w2_ref[...].astype(BF16)
    wgb_ref[...] = wg_ref[...].astype(BF16)
    wob_ref[...] = wo_ref[...].astype(BF16)

    subs = [slice(r, r + T_SUB) for r in range(0, T_IN, T_SUB)]
    lane = lax.broadcasted_iota(jnp.int32, (T_SUB, LANES), 1)
    ab = [_rms(x_ref[0, rows, :], g_ref[...]).astype(BF16) for rows in subs]
    zuv = [jnp.dot(a, wuv_ref[...], preferred_element_type=F32) for a in ab]
    f = [jnp.dot(a, wf_ref[...], preferred_element_type=F32) + fb_ref[...] for a in ab]

    c3 = []
    carry = carry_ref[...]
    for fi in f:
        logf = jnp.minimum(fi, 0.0) - jnp.log1p(jnp.exp(-jnp.abs(fi)))
        hi, mid, lo = _split3(logf)
        pieces = jnp.where(lane < PIECE_LANES, hi, jnp.where(lane < 2 * PIECE_LANES, mid, lo)).astype(BF16)
        c3.append(jnp.dot(tril_ref[...], pieces, preferred_element_type=F32) + carry)
        carry = c3[-1][T_SUB - 1:T_SUB, :]
    carry_ref[...] = carry

    zqkv = []
    for rows, a, zuv_i in zip(subs, ab, zuv):
        zqkv.append(jnp.dot(a, wqkv_ref[...], preferred_element_type=F32))
        _spatial_gating(rows, zuv_i, lng_ref, lnb_ref, sgw_ref, sgb_ref, sgog_ref, ysg_ref, ysg_scr)
    for rows, zqkv_i, c3_i in zip(subs, zqkv, c3):
        _pack_heads(rows, lane, zqkv_i, c3_i, qp_ref, kp_ref, vp_ref)


def _spatial_gating(rows, zuv, lng_ref, lnb_ref, sgw_ref, sgb_ref, sgog_ref, ysg_ref, ysg_scr):
    r0 = rows.start
    u = _gelu_tanh(zuv[:, :SG_WIDTH])
    v = _gelu_tanh(zuv[:, SG_WIDTH:])
    vc = v - jnp.mean(v, axis=-1, keepdims=True)
    vln = vc * lax.rsqrt(jnp.mean(vc * vc, axis=-1, keepdims=True) + EPS) * lng_ref[...] + lnb_ref[...]
    vb = vln.astype(BF16)
    n_chunks = T_SUB // CHUNK
    ci = lax.broadcasted_iota(jnp.int32, (CHUNK, CHUNK), 0)
    cj = lax.broadcasted_iota(jnp.int32, (CHUNK, CHUNK), 1)
    lane_c = lax.broadcasted_iota(jnp.int32, (CHUNK, LANES), 1)
    for pb in range(SG_WIDTH // LANES):
        vcat = jnp.concatenate(
            [vb[c * CHUNK:(c + 1) * CHUNK, pb * LANES:(pb + 1) * LANES] for c in range(n_chunks)], axis=1)
        res = []
        for g in (2 * pb, 2 * pb + 1):
            w = jnp.where(cj <= ci, sgw_ref[g], 0.0).astype(BF16)
            res.append(jnp.dot(w, vcat, preferred_element_type=F32))
        for c in range(n_chunks):
            mixed = jnp.where(lane_c < HEAD_DIM,
                              res[0][:, c * LANES:(c + 1) * LANES],
                              res[1][:, c * LANES:(c + 1) * LANES])
            mixed = mixed + sgb_ref[:, pb * LANES:(pb + 1) * LANES]
            ysg_scr[r0 + c * CHUNK:r0 + (c + 1) * CHUNK, pb * LANES:(pb + 1) * LANES] = (
                u[c * CHUNK:(c + 1) * CHUNK, pb * LANES:(pb + 1) * LANES] * mixed)
    ysg_ref[0, rows, :] = _rms(ysg_scr[rows, :], sgog_ref[...]).astype(BF16)


def _pack_heads(rows, lane, zqkv, c3, qp_ref, kp_ref, vp_ref):
    csum = (c3 + pltpu.roll(c3, LANES - PIECE_LANES, axis=1)
            + pltpu.roll(c3, LANES - 2 * PIECE_LANES, axis=1))
    chi, cmid, clo = _split3(csum * LOG2E)
    cp = jnp.where(lane < N_HEADS, chi, jnp.where(lane < 2 * N_HEADS, cmid, clo))
    q_bias_lo = jnp.where(lane < PIECE_LANES, cp, jnp.where(lane < BIAS_LANES, 1.0, 0.0))
    k_bias_lo = jnp.where(lane < PIECE_LANES, 1.0,
                          jnp.where(lane < BIAS_LANES, -pltpu.roll(cp, PIECE_LANES, axis=1), 0.0))
    q_bias_hi = pltpu.roll(q_bias_lo, HEAD_DIM, axis=1)
    k_bias_hi = pltpu.roll(k_bias_lo, HEAD_DIM, axis=1)

    for h in range(N_HEADS):
        blk = h // 2
        even = (h % 2 == 0)
        in_head = (lane < HEAD_DIM) if even else (lane >= HEAD_DIM)
        own_bias = ((lane & (N_HEADS - 1)) == h) & ((lane & (HEAD_DIM - 1)) < BIAS_LANES)
        zq = zqkv[:, blk * LANES:(blk + 1) * LANES] * (HEAD_DIM ** -0.5 * LOG2E)
        zk = zqkv[:, ATT_WIDTH + blk * LANES:ATT_WIDTH + (blk + 1) * LANES]
        zv = zqkv[:, 2 * ATT_WIDTH + blk * LANES:2 * ATT_WIDTH + (blk + 1) * LANES]
        qp_ref[0, h, :, rows] = jnp.where(in_head, zq, q_bias_hi if even else q_bias_lo).T.astype(BF16)
        kp_ref[0, h, rows, :] = jnp.where(
            in_head, zk, jnp.where(own_bias, k_bias_hi if even else k_bias_lo, 0.0)).astype(BF16)
        vp_ref[0, h, :, rows] = jnp.where(in_head, zv, 1.0).T.astype(BF16)


def _attn_kernel(qt_ref, k_ref, vt_ref, o_ref, acc_ref, m_ref, s_scr, mc_scr):
    n_full = pl.program_id(2) * STAIRS

    chunks = [slice(c * T_K, (c + 1) * T_K) for c in range(STAIRS)]

    def scores_chunk(slot, c, masked, kbs):
        for hh in range(2):
            s = jnp.dot(kbs[hh], qt_ref[0, hh, :, chunks[c]], preferred_element_type=F32)
            if masked:
                key = lax.broadcasted_iota(jnp.int32, (T_K, T_K), 0)
                qry = lax.broadcasted_iota(jnp.int32, (T_K, T_K), 1)
                s = jnp.where(key <= qry, s, NEG_BIG)
            s_scr[slot, hh, :, chunks[c]] = s
            mc_scr[slot, hh, :, chunks[c]] = jnp.broadcast_to(jnp.max(s, axis=0, keepdims=True), (SUB, T_K))

    def update_chunk(slot, c, first, vts):
        for hh in range(2):
            m_blk = mc_scr[slot, hh, :, chunks[c]]
            if first:
                m_new = m_blk
            else:
                m_old = m_ref[hh, :, chunks[c]]
                m_new = jnp.maximum(m_old, m_blk)
            p = jnp.exp2(s_scr[slot, hh, :, chunks[c]] - m_new[0:1, :]).astype(BF16)
            pv = jnp.dot(vts[hh], p, preferred_element_type=F32)
            if first:
                acc_ref[hh, :, chunks[c]] = pv
            else:
                alpha = jnp.exp2(m_old - m_new)
                acc_ref[hh, :, chunks[c]] = alpha[0:1, :] * acc_ref[hh, :, chunks[c]] + pv
            m_ref[hh, :, chunks[c]] = m_new

    def stage(nxt=None, cur=None):
        kbs = vts = None
        if nxt is not None:
            start = pl.multiple_of(nxt[0] * T_K, T_K)
            kbs = [k_ref[0, hh, pl.ds(start, T_K), :] for hh in range(2)]
        if cur is not None:
            start = pl.multiple_of(cur[0] * T_K, T_K)
            vts = [vt_ref[0, hh, :, pl.ds(start, T_K)] for hh in range(2)]
        for c in range(STAIRS):
            if nxt is not None and c >= (nxt[2] or 0):
                scores_chunk(nxt[1], c, nxt[2] == c, kbs)
            if cur is not None and c >= cur[2]:
                update_chunk(cur[1], c, cur[3], vts)

    def staircase(first_blk, is_tile_start):
        for k in range(STAIRS - 1):
            stage(nxt=(first_blk + k + 1, (k + 1) % 2, k + 1),
                  cur=(first_blk + k, k % 2, k, is_tile_start and k == 0))
        stage(cur=(first_blk + STAIRS - 1, (STAIRS - 1) % 2, STAIRS - 1, False))

    @pl.when(n_full == 0)
    def _():
        stage(nxt=(0, 0, 0))
        staircase(0, True)

    @pl.when(n_full > 0)
    def _():
        stage(nxt=(0, 0, None))
        stage(nxt=(1, 1, None), cur=(0, 0, 0, True))

        def body(i, carry):
            b = 1 + ATT_UNROLL * i
            for u in range(ATT_UNROLL):
                stage(nxt=(b + u + 1, u % 2, None), cur=(b + u, (u + 1) % 2, 0, False))
            return carry

        lax.fori_loop(0, (n_full - 2) // ATT_UNROLL, body, 0)
        b = n_full - 1 - ATT_TAIL
        for u in range(ATT_TAIL):
            stage(nxt=(b + u + 1, u % 2, None), cur=(b + u, (u + 1) % 2, 0, False))
        stage(nxt=(n_full, 0, 0), cur=(n_full - 1, 1, 0, False))
        staircase(n_full, False)

    a0 = acc_ref[0]
    a1 = acc_ref[1]
    numer = jnp.concatenate([a0[:HEAD_DIM], a1[HEAD_DIM:]], axis=0)
    denom = jnp.concatenate([a0[HEAD_DIM:], a1[:HEAD_DIM]], axis=0)
    o_ref[0] = (numer / denom).T


def _mlp_kernel(x_ref, ya_ref, ys_ref, p_ref, wo_ref, w1_ref, w2_ref, wg_ref, wpe_ref,
                gatt_ref, gpm_ref, gpf_ref, gqf_ref, bg_ref, o_ref):
    subs = [slice(r, r + T_MLP_SUB) for r in range(0, T_MLP, T_MLP_SUB)]
    y = []
    for rows in subs:
        yan = _rms(ya_ref[rows, :], gatt_ref[...]).astype(BF16)
        y.append(jnp.dot(yan, wo_ref[:ATT_WIDTH, :], preferred_element_type=F32)
                 + jnp.dot(ys_ref[rows, :], wo_ref[ATT_WIDTH:, :], preferred_element_type=F32))
    h1 = [x_ref[rows, :] + _rms(yy, gpm_ref[...]) for rows, yy in zip(subs, y)]
    cn = [_rms(hh, gpf_ref[...]).astype(BF16) for hh in h1]
    ff = [None] * len(subs)
    for c in range(D_FF // FF_CHUNK):
        cols = slice(c * FF_CHUNK, (c + 1) * FF_CHUNK)
        for i in range(len(subs)):
            hid = jnp.dot(cn[i], w1_ref[:, cols], preferred_element_type=F32)
            hid = jnp.square(jnp.maximum(hid, 0.0)).astype(BF16)
            part = jnp.dot(hid, w2_ref[cols, :], preferred_element_type=F32)
            ff[i] = part if ff[i] is None else ff[i] + part
    for i, rows in enumerate(subs):
        h2 = h1[i] + _rms(ff[i], gqf_ref[...])
        gate = jax.nn.sigmoid(
            jnp.dot(h2.astype(BF16), wg_ref[...], preferred_element_type=F32) + bg_ref[...])
        pe = jnp.dot(p_ref[rows, :].astype(BF16), wpe_ref[...], preferred_element_type=F32)
        o_ref[rows, :] = h2 + gate * pe


def _layer(h, p, w_in, f_bias, sg_ln_g, sg_ln_b, sg_w, sg_b, att_out_g, sg_out_g, w_out,
           pre_mix_g, post_mix_g, pre_ffn_g, post_ffn_g, w_ff1, w_ff2, ple_w, ple_gate_w, ple_gate_b):
    B, S, D = h.shape
    R = B * S
    assert D == D_MODEL and w_in.shape == (D, 3 * ATT_WIDTH + N_HEADS + 2 * SG_WIDTH)
    assert S % T_Q == 0 and S % T_IN == 0 and R % T_MLP == 0 and D % (B * (S // T_IN)) == 0
    assert T_IN % T_SUB == 0 and T_SUB % CHUNK == 0 and T_MLP % T_MLP_SUB == 0 and D_FF % FF_CHUNK == 0
    row = lambda a: a.reshape(1, -1).astype(F32)

    wqkv = w_in[:, :3 * ATT_WIDTH].astype(BF16)
    wf8 = w_in[:, 3 * ATT_WIDTH:3 * ATT_WIDTH + N_HEADS]
    n_rep = N_PIECES * N_PIECES
    wf = jnp.pad(jnp.tile(wf8, (1, n_rep)), ((0, 0), (0, LANES - n_rep * N_HEADS))).astype(BF16)
    fb = jnp.pad(jnp.tile(f_bias, n_rep), (0, LANES - n_rep * N_HEADS)).reshape(1, LANES).astype(F32)
    tril = jnp.tril(jnp.ones((T_SUB, T_SUB), BF16))
    wuv = w_in[:, 3 * ATT_WIDTH + N_HEADS:].astype(BF16)
    sgb = jnp.repeat(jnp.transpose(sg_b), HEAD_DIM, axis=1).astype(F32)

    const2 = lambda shape: pl.BlockSpec(shape, lambda b, t: (0,) * len(shape))
    n_t = S // T_IN
    slab = lambda w: pl.BlockSpec((w.shape[0] // (B * n_t), w.shape[1]), lambda b, t: (b * n_t + t, 0))
    cast_ws = (w_ff1, w_ff2, ple_gate_w, w_out)
    qp, kp, vp, ysg, w1b, w2b, wgb, wob = pl.pallas_call(
        _inproj_kernel,
        grid=(B, S // T_IN),
        in_specs=[
            pl.BlockSpec((1, T_IN, D), lambda b, t: (b, t, 0)),
            const2((1, D)),
            const2((D, 3 * ATT_WIDTH)),
            const2((D, LANES)),
            const2((1, LANES)),
            const2((T_SUB, T_SUB)),
            const2((D, 2 * SG_WIDTH)),
            const2((1, SG_WIDTH)),
            const2((1, SG_WIDTH)),
            const2((N_SG_GROUPS, CHUNK, CHUNK)),
            const2((CHUNK, SG_WIDTH)),
            const2((1, SG_WIDTH)),
        ] + [slab(w) for w in cast_ws],
        out_specs=[
            pl.BlockSpec((1, N_HEADS, LANES, T_IN), lambda b, t: (b, 0, 0, t)),
            pl.BlockSpec((1, N_HEADS, T_IN, LANES), lambda b, t: (b, 0, t, 0)),
            pl.BlockSpec((1, N_HEADS, LANES, T_IN), lambda b, t: (b, 0, 0, t)),
            pl.BlockSpec((1, T_IN, SG_WIDTH), lambda b, t: (b, t, 0)),
        ] + [slab(w) for w in cast_ws],
        out_shape=[
            jax.ShapeDtypeStruct((B, N_HEADS, LANES, S), BF16),
            jax.ShapeDtypeStruct((B, N_HEADS, S, LANES), BF16),
            jax.ShapeDtypeStruct((B, N_HEADS, LANES, S), BF16),
            jax.ShapeDtypeStruct((B, S, SG_WIDTH), BF16),
        ] + [jax.ShapeDtypeStruct(w.shape, BF16) for w in cast_ws],
        scratch_shapes=[pltpu.VMEM((1, LANES), F32), pltpu.VMEM((T_IN, SG_WIDTH), F32)],
        compiler_params=pltpu.CompilerParams(
            dimension_semantics=("arbitrary", "arbitrary"), vmem_limit_bytes=VMEM_LIMIT),
        name="inproj",
    )(h, row(pre_mix_g), wqkv, wf, fb, tril, wuv, row(sg_ln_g), row(sg_ln_b), sg_w.astype(F32), sgb,
      row(sg_out_g), *cast_ws)

    yatt = pl.pallas_call(
        _attn_kernel,
        grid=(B, HEAD_PAIRS, S // T_Q),
        in_specs=[
            pl.BlockSpec((1, 2, LANES, T_Q), lambda b, hp, i: (b, hp, 0, i)),
            pl.BlockSpec((1, 2, S, LANES), lambda b, hp, i: (b, hp, 0, 0)),
            pl.BlockSpec((1, 2, LANES, S), lambda b, hp, i: (b, hp, 0, 0)),
        ],
        out_specs=pl.BlockSpec((1, T_Q, LANES), lambda b, hp, i: (b, i, hp)),
        out_shape=jax.ShapeDtypeStruct((B, S, ATT_WIDTH), F32),
        scratch_shapes=[pltpu.VMEM((2, LANES, T_Q), F32), pltpu.VMEM((2, SUB, T_Q), F32),
                        pltpu.VMEM((2, 2, T_K, T_Q), F32), pltpu.VMEM((2, 2, SUB, T_Q), F32)],
        compiler_params=pltpu.CompilerParams(
            dimension_semantics=("arbitrary", "arbitrary", "arbitrary"), vmem_limit_bytes=VMEM_LIMIT),
        name="fox_attn",
    )(qp, kp, vp)

    rows = lambda w: pl.BlockSpec((T_MLP, w), lambda i: (i, 0))
    const = lambda shape: pl.BlockSpec(shape, lambda i: (0,) * len(shape), pipeline_mode=pl.Buffered(1))
    out = pl.pallas_call(
        _mlp_kernel,
        grid=(R // T_MLP,),
        in_specs=[
            rows(D), rows(ATT_WIDTH), rows(SG_WIDTH), rows(PLE_DIM),
            const((ATT_WIDTH + SG_WIDTH, D)),
            const((D, D_FF)), const((D_FF, D)),
            const((D, D)), const((PLE_DIM, D)),
            const((1, ATT_WIDTH)), const((1, D)), const((1, D)), const((1, D)), const((1, D)),
        ],
        out_specs=rows(D),
        out_shape=jax.ShapeDtypeStruct((R, D), F32),
        compiler_params=pltpu.CompilerParams(
            dimension_semantics=("arbitrary",), vmem_limit_bytes=VMEM_LIMIT_MLP),
        name="mlp",
    )(h.reshape(R, D), yatt.reshape(R, ATT_WIDTH), ysg.reshape(R, SG_WIDTH), p.reshape(R, PLE_DIM),
      wob, w1b, w2b, wgb, ple_w.astype(BF16),
      row(att_out_g), row(post_mix_g), row(pre_ffn_g), row(post_ffn_g), row(ple_gate_b))
    return out.reshape(B, S, D)


def kernel(x, p, w_in, f_bias, sg_ln_g, sg_ln_b, sg_w, sg_b, att_out_g, sg_out_g, w_out,
           pre_mix_g, post_mix_g, pre_ffn_g, post_ffn_g, w_ff1, w_ff2, ple_w, ple_gate_w, ple_gate_b):
    h = x
    for i in range(p.shape[0]):
        h = _layer(h, p[i], w_in[i], f_bias[i], sg_ln_g[i], sg_ln_b[i], sg_w[i], sg_b[i],
                   att_out_g[i], sg_out_g[i], w_out[i], pre_mix_g[i], post_mix_g[i],
                   pre_ffn_g[i], post_ffn_g[i], w_ff1[i], w_ff2[i], ple_w[i], ple_gate_w[i],
                   ple_gate_b[i])
    return h
```
